```python
import math
import jax, jax.numpy as jnp
from jax import lax
import numpy as np


D_MODEL = 2048
BATCH = 4
SEQ = 4096
DEPTH = 4

N_MIXERS = 3
EPS = 1e-6

GLA_HEADS = 4
GLA_DK = D_MODEL // 2
GLA_DV = D_MODEL
GLA_HK = GLA_DK // GLA_HEADS
GLA_HV = GLA_DV // GLA_HEADS
GLA_RANK = 16
GLA_TAU = 16.0
GLA_CHUNK = 64
GLA_IN = 2 * GLA_DK + 2 * GLA_DV + GLA_RANK

POOL_WINDOWS = (2, 4, 8, 16)
POOL_GROUPS = 4
POOL_GW = D_MODEL // POOL_GROUPS

DIFF_HEADS = 8
DIFF_HD = D_MODEL // DIFF_HEADS // 2
DIFF_VD = 2 * DIFF_HD
Q_BLOCK = 128
REL_BUCKETS = 32
REL_MAX_DIST = 128

FFN_HIDDEN = ((8 * D_MODEL + 3 * 256 - 1) // (3 * 256)) * 256

N_GLA = (DEPTH + 2) // N_MIXERS
N_POOL = (DEPTH + 1) // N_MIXERS
N_DIFF = DEPTH // N_MIXERS

kernel_name = "hybrid_gla_pool_diffattn_trunk"


def rms_norm(x, g):
    xf = x.astype(jnp.float32)
    y = xf * lax.rsqrt(jnp.mean(xf * xf, axis=-1, keepdims=True) + EPS)
    return (y * g.astype(jnp.float32)).astype(x.dtype)


def gla_mixer(h, w_in, w_a2, b_a, g_norm, w_out):
    B, S, _ = h.shape
    f32 = jnp.float32
    C = GLA_CHUNK
    nc = S // C
    proj = h @ w_in
    q, k, v, r, a_lr = jnp.split(
        proj, [GLA_DK, 2 * GLA_DK, 2 * GLA_DK + GLA_DV, 2 * GLA_DK + 2 * GLA_DV], axis=-1)
    log_a = jax.nn.log_sigmoid((a_lr @ w_a2 + b_a).astype(f32)) / GLA_TAU

    def to_chunks(t, hd):
        return t.astype(f32).reshape(B, nc, C, GLA_HEADS, hd).transpose(1, 0, 3, 2, 4)

    qc = to_chunks(q, GLA_HK) * (GLA_HK ** -0.5)
    kc = to_chunks(k, GLA_HK)
    vc = to_chunks(v, GLA_HV)
    bc = jnp.cumsum(to_chunks(log_a, GLA_HK), axis=3)
    causal = jnp.tril(jnp.ones((C, C), dtype=bool))[:, :, None]

    def step(state, inp):
        qb, kb, vb, bb = inp
        o_inter = jnp.einsum('bhtd,bhdv->bhtv', qb * jnp.exp(bb), state)
        diff = bb[:, :, :, None, :] - bb[:, :, None, :, :]
        decay = jnp.exp(jnp.where(causal, diff, -jnp.inf))
        scores = jnp.einsum('bhtd,bhsd,bhtsd->bhts', qb, kb, decay)
        o_intra = jnp.einsum('bhts,bhsv->bhtv', scores, vb)
        b_last = bb[:, :, -1:, :]
        k_dec = kb * jnp.exp(b_last - bb)
        state = jnp.exp(b_last[:, :, 0, :, None]) * state + jnp.einsum('bhsd,bhsv->bhdv', k_dec, vb)
        return state, o_inter + o_intra

    s0 = jnp.zeros((B, GLA_HEADS, GLA_HK, GLA_HV), f32)
    _, o = lax.scan(step, s0, (qc, kc, vc, bc))
    o = o.transpose(1, 0, 3, 2, 4).reshape(B, S, GLA_HEADS, GLA_HV)
    o = rms_norm(o, g_norm).reshape(B, S, GLA_DV) * jax.nn.silu(r.astype(f32))
    return o.astype(h.dtype) @ w_out


def pool_mixer(h, w_pool, scale):
    B, S, D = h.shape
    f32 = jnp.float32
    hf = h.astype(f32)
    cs = jnp.pad(jnp.cumsum(hf, axis=1), ((0, 0), (1, 0), (0, 0)))
    t = jnp.arange(S)
    outs = []
    for g, w in enumerate(POOL_WINDOWS):
        c = cs[:, :, g * POOL_GW:(g + 1) * POOL_GW]
        start = jnp.maximum(t + 1 - w, 0)
        count = (t + 1 - start).astype(f32)
        pooled = (c[:, 1:] - c[:, start]) / count[None, :, None]
        outs.append(pooled - hf[:, :, g * POOL_GW:(g + 1) * POOL_GW])
    y = jnp.stack(outs, axis=2).astype(h.dtype)
    y = jnp.einsum('bsgc,gcd->bsgd', y, w_pool).reshape(B, S, D)
    return y * scale


def rel_bucket(rel):
    n = jnp.maximum(rel, 0)
    max_exact = REL_BUCKETS // 2
    nf = jnp.maximum(n, 1).astype(jnp.float32)
    large = max_exact + (jnp.log(nf / max_exact) / math.log(REL_MAX_DIST / max_exact)
                         * (REL_BUCKETS - max_exact)).astype(jnp.int32)
    large = jnp.minimum(large, REL_BUCKETS - 1)
    return jnp.where(n < max_exact, n, large)


def diff_attn_mixer(h, w_in, q_gain, k_gain, lam_params, sub_gain, w_out, rel_table, layer_idx):
    B, S, D = h.shape
    f32 = jnp.float32
    H2 = 2 * DIFF_HEADS
    proj = h @ w_in
    q, k, v = jnp.split(proj, [D, 2 * D], axis=-1)
    q = rms_norm(q.reshape(B, S, H2, DIFF_HD), q_gain).transpose(0, 2, 1, 3)
    k = rms_norm(k.reshape(B, S, H2, DIFF_HD), k_gain).transpose(0, 2, 1, 3)
    v = v.reshape(B, S, DIFF_HEADS, DIFF_VD).transpose(0, 2, 1, 3)
    lam_init = 0.8 - 0.6 * math.exp(-0.3 * layer_idx)
    lp = lam_params.astype(f32)
    lam = jnp.exp(jnp.sum(lp[0] * lp[1])) - jnp.exp(jnp.sum(lp[2] * lp[3])) + lam_init
    nb = S // Q_BLOCK
    qb = q.reshape(B, H2, nb, Q_BLOCK, DIFF_HD).transpose(2, 0, 1, 3, 4)
    kpos = jnp.arange(S)
    scale = DIFF_HD ** -0.5

    def block(args):
        qblk, bi = args
        qpos = bi * Q_BLOCK + jnp.arange(Q_BLOCK)
        rel = qpos[:, None] - kpos[None, :]
        bias = rel_table[rel_bucket(rel)].astype(f32).transpose(2, 0, 1)
        logits = jnp.einsum('bhqd,bhkd->bhqk', qblk, k).astype(f32) * scale + bias
        logits = jnp.where(rel >= 0, logits, -jnp.inf)
        p = jax.nn.softmax(logits, axis=-1).reshape(B, DIFF_HEADS, 2, Q_BLOCK, S)
        attn = p[:, :, 0] - lam * p[:, :, 1]
        return jnp.einsum('bhqk,bhkd->bhqd', attn.astype(v.dtype), v)

    o = lax.map(block, (qb, jnp.arange(nb)))
    o = o.transpose(1, 0, 3, 2, 4).reshape(B, S, DIFF_HEADS, DIFF_VD)
    o = rms_norm(o, sub_gain) * (1.0 - lam_init)
    return o.reshape(B, S, D).astype(h.dtype) @ w_out


def swiglu(h, w_gu, w_down):
    g, u = jnp.split(h @ w_gu, 2, axis=-1)
    return (jax.nn.silu(g) * u) @ w_down


def setup_inputs(seed: int = 0) -> dict:
    key = jax.random.key(seed)
    ks = jax.random.split(key, 20)
    D = D_MODEL
    nrm = jax.random.normal
    out_scale = (2 * DEPTH) ** -0.5
    return {
        "x": nrm(ks[0], (BATCH, SEQ, D), jnp.float32),
        "norm_g": 1.0 + 0.02 * nrm(ks[1], (DEPTH, 2, D), jnp.float32),
        "gla_w_in": nrm(ks[2], (N_GLA, D, GLA_IN), jnp.float32) * D ** -0.5,
        "gla_w_a2": nrm(ks[3], (N_GLA, GLA_RANK, GLA_DK), jnp.float32) * GLA_RANK ** -0.5,
        "gla_b_a": 0.1 * nrm(ks[4], (N_GLA, GLA_DK), jnp.float32),
        "gla_g_norm": 1.0 + 0.02 * nrm(ks[5], (N_GLA, GLA_HV), jnp.float32),
        "gla_w_out": nrm(ks[6], (N_GLA, GLA_DV, D), jnp.float32) * GLA_DV ** -0.5 * out_scale,
        "pool_w": nrm(ks[7], (N_POOL, POOL_GROUPS, POOL_GW, POOL_GW), jnp.float32) * POOL_GW ** -0.5,
        "pool_scale": 1.0 + 0.1 * nrm(ks[8], (N_POOL, D), jnp.float32),
        "diff_w_in": nrm(ks[9], (N_DIFF, D, 3 * D), jnp.float32) * D ** -0.5,
        "diff_q_gain": 1.0 + 0.02 * nrm(ks[10], (N_DIFF, DIFF_HD), jnp.float32),
        "diff_k_gain": 1.0 + 0.02 * nrm(ks[11], (N_DIFF, DIFF_HD), jnp.float32),
        "diff_lambda": 0.1 * nrm(ks[12], (N_DIFF, 4, DIFF_HD), jnp.float32),
        "diff_sub_gain": 1.0 + 0.02 * nrm(ks[13], (N_DIFF, DIFF_VD), jnp.float32),
        "diff_w_out": nrm(ks[14], (N_DIFF, D, D), jnp.float32) * D ** -0.5 * out_scale,
        "rel_bias": 0.5 * nrm(ks[15], (REL_BUCKETS, 2 * DIFF_HEADS), jnp.float32),
        "ffn_w_gu": nrm(ks[16], (DEPTH, D, 2 * FFN_HIDDEN), jnp.float32) * D ** -0.5,
        "ffn_w_down": nrm(ks[17], (DEPTH, FFN_HIDDEN, D), jnp.float32) * FFN_HIDDEN ** -0.5 * out_scale,
    }


def reference(x, norm_g, gla_w_in, gla_w_a2, gla_b_a, gla_g_norm, gla_w_out,
              pool_w, pool_scale, diff_w_in, diff_q_gain, diff_k_gain, diff_lambda,
              diff_sub_gain, diff_w_out, rel_bias, ffn_w_gu, ffn_w_down):
    for i in range(DEPTH):
        kind = i % N_MIXERS
        slot = i // N_MIXERS
        h = rms_norm(x, norm_g[i, 0])
        if kind == 0:
            y = gla_mixer(h, gla_w_in[slot], gla_w_a2[slot], gla_b_a[slot],
                          gla_g_norm[slot], gla_w_out[slot])
        elif kind == 1:
            y = pool_mixer(h, pool_w[slot], pool_scale[slot])
        else:
            y = diff_attn_mixer(h, diff_w_in[slot], diff_q_gain[slot], diff_k_gain[slot],
                                diff_lambda[slot], diff_sub_gain[slot], diff_w_out[slot],
                                rel_bias, i)
        x = x + y.astype(x.dtype)
        x = x + swiglu(rms_norm(x, norm_g[i, 1]), ffn_w_gu[i], ffn_w_down[i]).astype(x.dtype)
    return x
```

```python
import functools
import math

import jax
import jax.numpy as jnp
from jax import lax
from jax.experimental import pallas as pl
from jax.experimental.pallas import tpu as pltpu

F32 = jnp.float32
BF16 = jnp.bfloat16

EPS = 1e-6
N_MIXERS = 3

GLA_HEADS = 4
GLA_RANK = 16
GLA_TAU = 16.0
GLA_CHUNK = 128
GLA_DIAG = 16
GLA_RANK_PAD = 128

POOL_WINDOWS = (2, 4, 8, 16)
POOL_HALO = 16

DIFF_HEADS = 8
REL_BUCKETS = 32
REL_MAX_DIST = 128

VMEM_LIMIT_BYTES = 56 * 1024 * 1024


def _cparams(*sem):
    return pltpu.CompilerParams(dimension_semantics=sem, vmem_limit_bytes=VMEM_LIMIT_BYTES)


def _rms(x, g):
    ms = jnp.mean(x * x, axis=-1, keepdims=True)
    return x * lax.rsqrt(ms + EPS) * g


def _dot(a, b):
    return jnp.dot(a, b, preferred_element_type=F32)


def _dot_nt(a, b):
    return lax.dot_general(a, b, (((1,), (1,)), ((), ())), preferred_element_type=F32)


def _dot_tn(a, b):
    return lax.dot_general(a, b, (((0,), (0,)), ((), ())), preferred_element_type=F32)


def _norm_matmul_kernel(x_ref, g_ref, w_ref, hg_ref, o_ref, h_ref, *, head_norm_blocks, head_dim):
    j = pl.program_id(1)

    @pl.when(j == 0)
    def _():
        h_ref[...] = _rms(x_ref[...], g_ref[...]).astype(BF16)

    acc = _dot(h_ref[...], w_ref[...])
    if head_norm_blocks == 0:
        o_ref[...] = acc.astype(o_ref.dtype)
    else:
        @pl.when(j < head_norm_blocks)
        def _():
            for c in range(acc.shape[1] // head_dim):
                sl = slice(c * head_dim, (c + 1) * head_dim)
                o_ref[:, sl] = _rms(acc[:, sl], hg_ref[:, sl]).astype(o_ref.dtype)

        @pl.when(j >= head_norm_blocks)
        def _():
            o_ref[...] = acc.astype(o_ref.dtype)


def norm_matmul(x, g, w, head_gain=None, head_norm_cols=0, head_dim=128, bm=1024, bn=1024):
    M, D = x.shape
    N = w.shape[1]
    bm = min(bm, M)
    assert M % bm == 0 and N % bn == 0 and head_norm_cols % bn == 0
    if head_gain is None:
        head_gain = jnp.ones((N,), F32)
    kern = functools.partial(_norm_matmul_kernel, head_norm_blocks=head_norm_cols // bn,
                             head_dim=head_dim)
    return pl.pallas_call(
        kern,
        grid=(M // bm, N // bn),
        in_specs=[
            pl.BlockSpec((bm, D), lambda i, j: (i, 0)),
            pl.BlockSpec((1, D), lambda i, j: (0, 0)),
            pl.BlockSpec((D, bn), lambda i, j: (0, j)),
            pl.BlockSpec((1, bn), lambda i, j: (0, j)),
        ],
        out_specs=pl.BlockSpec((bm, bn), lambda i, j: (i, j)),
        out_shape=jax.ShapeDtypeStruct((M, N), BF16),
        scratch_shapes=[pltpu.VMEM((bm, D), BF16)],
        compiler_params=_cparams("parallel", "arbitrary"),
        name="norm_matmul",
    )(x, g.reshape(1, D), w, head_gain.reshape(1, N))


def _matmul_residual_kernel(a_ref, w_ref, x_ref, o_ref):
    o_ref[...] = x_ref[...] + _dot(a_ref[...], w_ref[...])


def matmul_residual(a, w, x, bm=1024, bn=1024):
    M, K = a.shape
    N = w.shape[1]
    bm = min(bm, M)
    assert M % bm == 0 and N % bn == 0
    return pl.pallas_call(
        _matmul_residual_kernel,
        grid=(M // bm, N // bn),
        in_specs=[
            pl.BlockSpec((bm, K), lambda i, j: (i, 0)),
            pl.BlockSpec((K, bn), lambda i, j: (0, j)),
            pl.BlockSpec((bm, bn), lambda i, j: (i, j)),
        ],
        out_specs=pl.BlockSpec((bm, bn), lambda i, j: (i, j)),
        out_shape=jax.ShapeDtypeStruct((M, N), F32),
        compiler_params=_cparams("parallel", "arbitrary"),
        name="matmul_residual",
    )(a, w, x)


def _ffn_kernel(x_ref, g_ref, wg_ref, wu_ref, wd_ref, o_ref, h_ref):
    j = pl.program_id(1)

    @pl.when(j == 0)
    def _():
        x = x_ref[...]
        h_ref[...] = _rms(x, g_ref[...]).astype(BF16)
        o_ref[...] = x

    h = h_ref[...]
    gate = _dot(h, wg_ref[...])
    up = _dot(h, wu_ref[...])
    act = (gate * jax.nn.sigmoid(gate) * up).astype(BF16)
    o_ref[...] += _dot(act, wd_ref[...])


def ffn(x, g, w_gu, w_down, bm=512, bh=512):
    M, D = x.shape
    FH = w_down.shape[0]
    bm = min(bm, M)
    assert M % bm == 0 and FH % bh == 0
    nh = FH // bh
    return pl.pallas_call(
        _ffn_kernel,
        grid=(M // bm, nh),
        in_specs=[
            pl.BlockSpec((bm, D), lambda i, j: (i, 0)),
            pl.BlockSpec((1, D), lambda i, j: (0, 0)),
            pl.BlockSpec((D, bh), lambda i, j: (0, j)),
            pl.BlockSpec((D, bh), lambda i, j: (0, j + nh)),
            pl.BlockSpec((bh, D), lambda i, j: (j, 0)),
        ],
        out_specs=pl.BlockSpec((bm, D), lambda i, j: (i, 0)),
        out_shape=jax.ShapeDtypeStruct((M, D), F32),
        scratch_shapes=[pltpu.VMEM((bm, D), BF16)],
        compiler_params=_cparams("parallel", "arbitrary"),
        name="ffn",
    )(x, g.reshape(1, D), w_gu, w_gu, w_down)


def _block_rows(v, rows, span):
    return jnp.concatenate(
        [jnp.broadcast_to(v[r:r + 1, :], (span, v.shape[1])) for r in rows], axis=0)


def _gla_kernel(q_ref, k_ref, v_ref, r_ref, a_ref, wa_ref, ba_ref, gn_ref, o_ref, st_ref,
                *, chunk, diag, scale):
    C = chunk

    @pl.when(pl.program_id(2) == 0)
    def _():
        st_ref[...] = jnp.zeros_like(st_ref)

    q = q_ref[0].astype(F32) * scale
    k = k_ref[0].astype(F32)
    v = v_ref[0]
    hk = q.shape[1]

    z = _dot(a_ref[0], wa_ref[...]) + ba_ref[...]
    log_a = (jnp.minimum(z, 0.0) - jnp.log1p(jnp.exp(-jnp.abs(z)))) * (1.0 / GLA_TAU)
    row = lax.broadcasted_iota(jnp.int32, (C, C), 0)
    col = lax.broadcasted_iota(jnp.int32, (C, C), 1)
    tri = (row >= col).astype(BF16)
    a_hi = log_a.astype(BF16)
    rem = log_a - a_hi.astype(F32)
    a_mid = rem.astype(BF16)
    a_lo = (rem - a_mid.astype(F32)).astype(BF16)
    b = _dot(tri, a_hi) + _dot(tri, a_mid) + _dot(tri, a_lo)

    st = st_ref[...]
    o = _dot_nt((q * jnp.exp(b)).astype(BF16), st.astype(BF16))

    rix = lax.broadcasted_iota(jnp.int32, (C, 1), 0)
    scores = jnp.zeros((C, C), F32)
    w = C // 2
    while w >= diag:
        blk = 2 * w
        beta = _block_rows(b, range(w, C, blk), blk)
        e = jnp.exp(-jnp.abs(b - beta))
        right = (rix % blk) >= w
        qs = jnp.where(right, q * e, 0.0).astype(BF16)
        ks = jnp.where(right, 0.0, k * e).astype(BF16)
        part = _dot_nt(qs, ks)
        if blk < C:
            part = jnp.where((row // blk) == (col // blk), part, 0.0)
        scores = scores + part
        w //= 2

    dcol = lax.broadcasted_iota(jnp.int32, (hk, C), 1)
    dsum = jnp.zeros((C, C), F32)
    for s in range(diag):
        rows = range(s, C, diag)
        bs = _block_rows(b, rows, diag)
        ks = _block_rows(k, rows, diag)
        e = jnp.exp(jnp.minimum(b - bs, 0.0))
        p = jnp.where((rix % diag) >= s, q * e * ks, 0.0).astype(BF16)
        sel = ((dcol % diag) == s).astype(BF16)
        dsum = dsum + _dot(p, sel)
    scores = scores + jnp.where((row // diag) == (col // diag), dsum, 0.0)

    o = o + _dot(scores.astype(BF16), v)

    b_last = b[C - 1:C, :]
    k_dec = (k * jnp.exp(b_last - b)).astype(BF16)
    st_ref[...] = st * jnp.exp(b_last) + _dot_tn(v, k_dec)

    r = r_ref[0].astype(F32)
    o_ref[0] = (_rms(o, gn_ref[...]) * (r * jax.nn.sigmoid(r))).astype(o_ref.dtype)


def gla_core(proj, w_a2p, b_a, g_norm, *, dk, dv):
    B, S, _ = proj.shape
    H = GLA_HEADS
    hk, hv = dk // H, dv // H
    C = GLA_CHUNK
    assert S % C == 0
    k_blk0 = dk // hk
    v_blk0 = 2 * dk // hv
    r_blk0 = (2 * dk + dv) // hv
    a_blk = (2 * dk + 2 * dv) // GLA_RANK_PAD
    kern = functools.partial(_gla_kernel, chunk=C, diag=GLA_DIAG, scale=hk ** -0.5)
    return pl.pallas_call(
        kern,
        grid=(B, H, S // C),
        in_specs=[
            pl.BlockSpec((1, C, hk), lambda b, h, c: (b, c, h)),
            pl.BlockSpec((1, C, hk), lambda b, h, c: (b, c, k_blk0 + h)),
            pl.BlockSpec((1, C, hv), lambda b, h, c: (b, c, v_blk0 + h)),
            pl.BlockSpec((1, C, hv), lambda b, h, c: (b, c, r_blk0 + h)),
            pl.BlockSpec((1, C, GLA_RANK_PAD), lambda b, h, c: (b, c, a_blk)),
            pl.BlockSpec((GLA_RANK_PAD, hk), lambda b, h, c: (0, h)),
            pl.BlockSpec((1, hk), lambda b, h, c: (0, h)),
            pl.BlockSpec((1, hv), lambda b, h, c: (0, 0)),
        ],
        out_specs=pl.BlockSpec((1, C, hv), lambda b, h, c: (b, c, h)),
        out_shape=jax.ShapeDtypeStruct((B, S, dv), BF16),
        scratch_shapes=[pltpu.VMEM((hv, hk), F32)],
        compiler_params=_cparams("parallel", "parallel", "arbitrary"),
        name="gla_core",
    )(proj, proj, proj, proj, proj, w_a2p, b_a.reshape(1, dk), g_norm.reshape(1, hv))


def gla_mixer(x, g, w_in, w_a2, b_a, g_norm, w_out):
    B, S, D = x.shape
    dk, dv = w_a2.shape[1], w_out.shape[0]
    n_main = 2 * dk + 2 * dv
    n_pad = n_main + 256
    w_in_p = jnp.pad(w_in, ((0, 0), (0, n_pad - w_in.shape[1]))).astype(BF16)
    w_a2p = jnp.pad(w_a2, ((0, GLA_RANK_PAD - w_a2.shape[0]), (0, 0))).astype(BF16)
    x2 = x.reshape(B * S, D)
    proj = norm_matmul(x2, g, w_in_p, bn=n_pad // 5)
    o = gla_core(proj.reshape(B, S, n_pad), w_a2p, b_a, g_norm, dk=dk, dv=dv)
    return matmul_residual(o.reshape(B * S, dv), w_out.astype(BF16), x2).reshape(B, S, D)


def _pool_kernel(x_ref, g_ref, w_ref, sc_ref, o_ref, hb_ref, *, bm, gw):
    si = pl.program_id(1)
    HL = POOL_HALO

    @pl.when(si == 0)
    def _():
        hb_ref[0:HL, :] = jnp.zeros((HL, hb_ref.shape[1]), F32)

    @pl.when(si > 0)
    def _():
        hb_ref[0:HL, :] = hb_ref[bm:bm + HL, :]

    x = x_ref[0]
    hb_ref[HL:HL + bm, :] = _rms(x, g_ref[...])

    t = si * bm + lax.broadcasted_iota(jnp.int32, (bm, 1), 0)
    for gi, win in enumerate(POOL_WINDOWS):
        cs = slice(gi * gw, (gi + 1) * gw)
        h = hb_ref[HL:HL + bm, cs]
        acc = h
        for d in range(1, win):
            acc = acc + hb_ref[HL - d:HL - d + bm, cs]
        count = jnp.minimum(t + 1, win).astype(F32)
        y = (acc / count - h).astype(BF16)
        o_ref[0, :, cs] = x[:, cs] + _dot(y, w_ref[gi]) * sc_ref[:, cs]


def pool_mixer(x, g, w_pool, scale, bm=256):
    B, S, D = x.shape
    G, gw, _ = w_pool.shape
    assert S % bm == 0 and G == len(POOL_WINDOWS)
    kern = functools.partial(_pool_kernel, bm=bm, gw=gw)
    return pl.pallas_call(
        kern,
        grid=(B, S // bm),
        in_specs=[
            pl.BlockSpec((1, bm, D), lambda b, s: (b, s, 0)),
            pl.BlockSpec((1, D), lambda b, s: (0, 0)),
            pl.BlockSpec((G, gw, gw), lambda b, s: (0, 0, 0)),
            pl.BlockSpec((1, D), lambda b, s: (0, 0)),
        ],
        out_specs=pl.BlockSpec((1, bm, D), lambda b, s: (b, s, 0)),
        out_shape=jax.ShapeDtypeStruct((B, S, D), F32),
        scratch_shapes=[pltpu.VMEM((POOL_HALO + bm, D), F32)],
        compiler_params=_cparams("parallel", "arbitrary"),
        name="pool_mixer",
    )(x, g.reshape(1, D), w_pool.astype(BF16), scale.reshape(1, D))


def _rel_bucket(rel):
    n = jnp.maximum(rel, 0)
    max_exact = REL_BUCKETS // 2
    nf = jnp.maximum(n, 1).astype(F32)
    large = max_exact + (jnp.log(nf / max_exact) / math.log(REL_MAX_DIST / max_exact)
                         * (REL_BUCKETS - max_exact)).astype(jnp.int32)
    large = jnp.minimum(large, REL_BUCKETS - 1)
    return jnp.where(n < max_exact, n, large)


def _diff_attn_kernel(q_ref, k_ref, v_ref, bias_ref, lam_ref, sg_ref, o_ref,
                      m_ref, l_ref, acc_ref, *, bq, hd, lam_init):
    qi = pl.program_id(2)
    qb = q_ref[0]
    qs = (qb[:, :hd], qb[:, hd:])

    m_ref[...] = jnp.full(m_ref.shape, -jnp.inf, F32)
    l_ref[...] = jnp.zeros(l_ref.shape, F32)
    acc_ref[...] = jnp.zeros(acc_ref.shape, F32)

    def step(j, bias_idx, causal):
        off = pl.multiple_of(j * bq, bq)
        kb = k_ref[0, pl.ds(off, bq), :]
        vb = v_ref[0, pl.ds(off, bq), :]
        for c in range(2):
            s = _dot_nt(qs[c], kb[:, c * hd:(c + 1) * hd])
            if bias_idx is not None:
                s = s + bias_ref[0, c, bias_idx]
            if causal:
                row = lax.broadcasted_iota(jnp.int32, s.shape, 0)
                col = lax.broadcasted_iota(jnp.int32, s.shape, 1)
                s = jnp.where(row >= col, s, -jnp.inf)
            m_old = m_ref[c]
            m_new = jnp.maximum(m_old, jnp.max(s, axis=-1, keepdims=True))
            alpha = jnp.exp(m_old - m_new)
            p = jnp.exp(s - m_new)
            l_ref[c] = alpha * l_ref[c] + jnp.sum(p, axis=-1, keepdims=True)
            acc_ref[c] = alpha * acc_ref[c] + _dot(p.astype(BF16), vb)
            m_ref[c] = m_new

    def far_body(j, carry):
        step(j, None, False)
        return carry

    lax.fori_loop(0, jnp.maximum(qi - 1, 0), far_body, 0)

    @pl.when(qi >= 1)
    def _():
        step(qi - 1, 1, False)

    step(qi, 0, True)

    lp = lam_ref[...]
    lam = (jnp.exp(jnp.sum(lp[0:1] * lp[1:2], axis=-1, keepdims=True))
           - jnp.exp(jnp.sum(lp[2:3] * lp[3:4], axis=-1, keepdims=True)) + lam_init)
    o = acc_ref[0] / l_ref[0] - lam * (acc_ref[1] / l_ref[1])
    o_ref[0] = (_rms(o, sg_ref[...]) * (1.0 - lam_init)).astype(o_ref.dtype)


def diff_attention(proj, bias_tiles, lam_params, sub_gain, *, d_model, lam_init, bq=256):
    B, S, _ = proj.shape
    H = DIFF_HEADS
    vd = d_model // H
    hd = vd // 2
    assert S % bq == 0
    kern = functools.partial(_diff_attn_kernel, bq=bq, hd=hd, lam_init=lam_init)
    return pl.pallas_call(
        kern,
        grid=(B, H, S // bq),
        in_specs=[
            pl.BlockSpec((1, bq, vd), lambda b, h, i: (b, i, h)),
            pl.BlockSpec((1, S, vd), lambda b, h, i: (b, 0, H + h)),
            pl.BlockSpec((1, S, vd), lambda b, h, i: (b, 0, 2 * H + h)),
            pl.BlockSpec((1, 2, 2, bq, bq), lambda b, h, i: (h, 0, 0, 0, 0)),
            pl.BlockSpec((4, hd), lambda b, h, i: (0, 0)),
            pl.BlockSpec((1, vd), lambda b, h, i: (0, 0)),
        ],
        out_specs=pl.BlockSpec((1, bq, vd), lambda b, h, i: (b, i, h)),
        out_shape=jax.ShapeDtypeStruct((B, S, d_model), BF16),
        scratch_shapes=[
            pltpu.VMEM((2, bq, 1), F32),
            pltpu.VMEM((2, bq, 1), F32),
            pltpu.VMEM((2, bq, vd), F32),
        ],
        compiler_params=_cparams("parallel", "parallel", "arbitrary"),
        name="diff_attention",
    )(proj, proj, proj, bias_tiles, lam_params, sub_gain.reshape(1, vd))


def _bias_tiles(rel_table, bq):
    H2 = rel_table.shape[1]
    qpos = jnp.arange(bq)[:, None]
    kpos = jnp.arange(bq)[None, :]
    far = rel_table[_rel_bucket(jnp.asarray(2 * bq))]
    tiles = []
    for off in (0, 1):
        rel = qpos + off * bq - kpos
        t = rel_table[_rel_bucket(rel)].astype(F32) - far
        tiles.append(t.transpose(2, 0, 1))
    t = jnp.stack(tiles, axis=1)
    return t.reshape(H2 // 2, 2, 2, bq, bq)


def diff_attn_mixer(x, g, w_in, q_gain, k_gain, lam_params, sub_gain, w_out, rel_table, layer_idx,
                    bq=256):
    B, S, D = x.shape
    assert bq >= REL_MAX_DIST
    hd = D // DIFF_HEADS // 2
    lam_init = 0.8 - 0.6 * math.exp(-0.3 * layer_idx)
    n_heads2 = D // hd
    head_gain = jnp.concatenate([jnp.tile(q_gain, n_heads2) * (hd ** -0.5),
                                 jnp.tile(k_gain, n_heads2), jnp.ones((D,), F32)])
    x2 = x.reshape(B * S, D)
    proj = norm_matmul(x2, g, w_in.astype(BF16), head_gain=head_gain, head_norm_cols=2 * D,
                       head_dim=hd)
    o = diff_attention(proj.reshape(B, S, 3 * D), _bias_tiles(rel_table, bq), lam_params, sub_gain,
                       d_model=D, lam_init=lam_init, bq=bq)
    return matmul_residual(o.reshape(B * S, D), w_out.astype(BF16), x2).reshape(B, S, D)


def kernel(x, norm_g, gla_w_in, gla_w_a2, gla_b_a, gla_g_norm, gla_w_out, pool_w, pool_scale,
           diff_w_in, diff_q_gain, diff_k_gain, diff_lambda, diff_sub_gain, diff_w_out, rel_bias,
           ffn_w_gu, ffn_w_down):
    B, S, D = x.shape
    depth = norm_g.shape[0]
    for i in range(depth):
        kind, slot = i % N_MIXERS, i // N_MIXERS
        if kind == 0:
            x = gla_mixer(x, norm_g[i, 0], gla_w_in[slot], gla_w_a2[slot], gla_b_a[slot],
                          gla_g_norm[slot], gla_w_out[slot])
        elif kind == 1:
            x = pool_mixer(x, norm_g[i, 0], pool_w[slot], pool_scale[slot])
        else:
            x = diff_attn_mixer(x, norm_g[i, 0], diff_w_in[slot], diff_q_gain[slot],
                                diff_k_gain[slot], diff_lambda[slot], diff_sub_gain[slot],
                                diff_w_out[slot], rel_bias, i)
        x = ffn(x.reshape(B * S, D), norm_g[i, 1], ffn_w_gu[i].astype(BF16),
                ffn_w_down[i].astype(BF16)).reshape(B, S, D)
    return x
```

```python
import functools
import math

import jax
import jax.numpy as jnp
from jax import lax
from jax.experimental import pallas as pl
from jax.experimental.pallas import tpu as pltpu

F32 = jnp.float32
BF16 = jnp.bfloat16

EPS = 1e-6
N_MIXERS = 3

GLA_HEADS = 4
GLA_RANK = 16
GLA_TAU = 16.0
GLA_CHUNK = 128
GLA_DIAG = 8
LOG2_E = math.log2(math.e)
GLA_RANK_PAD = 128
GLA_HEADS_PER_STEP = 2

POOL_WINDOWS = (2, 4, 8, 16)
POOL_HALO = 16

DIFF_HEADS = 8
REL_BUCKETS = 32
REL_MAX_DIST = 128

VMEM_LIMIT_BYTES = 56 * 1024 * 1024


def _cparams(*sem):
    return pltpu.CompilerParams(dimension_semantics=sem, vmem_limit_bytes=VMEM_LIMIT_BYTES)


def _rms(x, g):
    ms = jnp.mean(x * x, axis=-1, keepdims=True)
    return x * lax.rsqrt(ms + EPS) * g


def _dot(a, b):
    return jnp.dot(a, b, preferred_element_type=F32)


def _dot_nt(a, b):
    return lax.dot_general(a, b, (((1,), (1,)), ((), ())), preferred_element_type=F32)


def _dot_tn(a, b):
    return lax.dot_general(a, b, (((0,), (0,)), ((), ())), preferred_element_type=F32)


def _norm_matmul_kernel(x_ref, g_ref, w_ref, hg_ref, o_ref, h_ref, *, head_norm_blocks, head_dim):
    j = pl.program_id(1)

    @pl.when(j == 0)
    def _():
        h_ref[...] = _rms(x_ref[...], g_ref[...]).astype(BF16)

    acc = _dot(h_ref[...], w_ref[...])
    if head_norm_blocks == 0:
        o_ref[...] = acc.astype(o_ref.dtype)
    else:
        @pl.when(j < head_norm_blocks)
        def _():
            for c in range(acc.shape[1] // head_dim):
                sl = slice(c * head_dim, (c + 1) * head_dim)
                o_ref[:, sl] = _rms(acc[:, sl], hg_ref[:, sl]).astype(o_ref.dtype)

        @pl.when(j >= head_norm_blocks)
        def _():
            o_ref[...] = acc.astype(o_ref.dtype)


def norm_matmul(x, g, w, head_gain=None, head_norm_cols=0, head_dim=128, bm=1024, bn=1024):
    M, D = x.shape
    N = w.shape[1]
    bm = min(bm, M)
    assert M % bm == 0 and N % bn == 0 and head_norm_cols % bn == 0
    if head_gain is None:
        head_gain = jnp.ones((N,), F32)
    kern = functools.partial(_norm_matmul_kernel, head_norm_blocks=head_norm_cols // bn,
                             head_dim=head_dim)
    return pl.pallas_call(
        kern,
        grid=(M // bm, N // bn),
        in_specs=[
            pl.BlockSpec((bm, D), lambda i, j: (i, 0)),
            pl.BlockSpec((1, D), lambda i, j: (0, 0)),
            pl.BlockSpec((D, bn), lambda i, j: (0, j)),
            pl.BlockSpec((1, bn), lambda i, j: (0, j)),
        ],
        out_specs=pl.BlockSpec((bm, bn), lambda i, j: (i, j)),
        out_shape=jax.ShapeDtypeStruct((M, N), BF16),
        scratch_shapes=[pltpu.VMEM((bm, D), BF16)],
        compiler_params=_cparams("parallel", "arbitrary"),
        name="norm_matmul",
    )(x, g.reshape(1, D), w, head_gain.reshape(1, N))


def _matmul_residual_kernel(a_ref, w_ref, x_ref, o_ref):
    o_ref[...] = x_ref[...] + _dot(a_ref[...], w_ref[...])


def matmul_residual(a, w, x, bm=1024, bn=1024):
    M, K = a.shape
    N = w.shape[1]
    bm = min(bm, M)
    assert M % bm == 0 and N % bn == 0
    return pl.pallas_call(
        _matmul_residual_kernel,
        grid=(M // bm, N // bn),
        in_specs=[
            pl.BlockSpec((bm, K), lambda i, j: (i, 0)),
            pl.BlockSpec((K, bn), lambda i, j: (0, j)),
            pl.BlockSpec((bm, bn), lambda i, j: (i, j)),
        ],
        out_specs=pl.BlockSpec((bm, bn), lambda i, j: (i, j)),
        out_shape=jax.ShapeDtypeStruct((M, N), F32),
        compiler_params=_cparams("parallel", "arbitrary"),
        name="matmul_residual",
    )(a, w, x)


def _ffn_kernel(x_ref, g_ref, wg_ref, wu_ref, wd_ref, o_ref, h_ref):
    j = pl.program_id(1)

    @pl.when(j == 0)
    def _():
        x = x_ref[...]
        h_ref[...] = _rms(x, g_ref[...]).astype(BF16)
        o_ref[...] = x

    h = h_ref[...]
    gate = _dot(h, wg_ref[...])
    up = _dot(h, wu_ref[...])
    act = (gate * jax.nn.sigmoid(gate) * up).astype(BF16)
    o_ref[...] += _dot(act, wd_ref[...])


def ffn(x, g, w_gu, w_down, bm=512, bh=512):
    M, D = x.shape
    FH = w_down.shape[0]
    bm = min(bm, M)
    assert M % bm == 0 and FH % bh == 0
    nh = FH // bh
    return pl.pallas_call(
        _ffn_kernel,
        grid=(M // bm, nh),
        in_specs=[
            pl.BlockSpec((bm, D), lambda i, j: (i, 0)),
            pl.BlockSpec((1, D), lambda i, j: (0, 0)),
            pl.BlockSpec((D, bh), lambda i, j: (0, j)),
            pl.BlockSpec((D, bh), lambda i, j: (0, j + nh)),
            pl.BlockSpec((bh, D), lambda i, j: (j, 0)),
        ],
        out_specs=pl.BlockSpec((bm, D), lambda i, j: (i, 0)),
        out_shape=jax.ShapeDtypeStruct((M, D), F32),
        scratch_shapes=[pltpu.VMEM((bm, D), BF16)],
        compiler_params=_cparams("parallel", "arbitrary"),
        name="ffn",
    )(x, g.reshape(1, D), w_gu, w_gu, w_down)


def _block_rows(ref, rows, span):
    return jnp.concatenate(
        [jnp.broadcast_to(ref[r:r + 1, :], (span, ref.shape[1])) for r in rows], axis=0)


def _gla_kernel(q_ref, k_ref, v_ref, r_ref, a_ref, wa_ref, ba_ref, gn_ref, o_ref, st_ref,
                b_ref, kf_ref, *, chunk, diag, scale, hk, hv, heads):
    C = chunk

    @pl.when(pl.program_id(2) == 0)
    def _():
        st_ref[...] = jnp.zeros_like(st_ref)

    row = lax.broadcasted_iota(jnp.int32, (C, C), 0)
    col = lax.broadcasted_iota(jnp.int32, (C, C), 1)
    rix = lax.broadcasted_iota(jnp.int32, (C, 1), 0)
    dcol = lax.broadcasted_iota(jnp.int32, (hk, C), 1)
    tri = (row >= col).astype(BF16)
    a_lr = a_ref[0]

    for h in range(heads):
        ks_, vs_ = slice(h * hk, (h + 1) * hk), slice(h * hv, (h + 1) * hv)
        q = q_ref[0, :, ks_].astype(F32) * scale
        k = k_ref[0, :, ks_].astype(F32)
        v = v_ref[0, :, vs_]

        z = _dot(a_lr, wa_ref[:, ks_]) + ba_ref[:, ks_]
        log_a = (jnp.minimum(z, 0.0) - jnp.log(1.0 + jnp.exp(-jnp.abs(z)))) * (LOG2_E / GLA_TAU)
        a_hi = log_a.astype(BF16)
        rem = log_a - a_hi.astype(F32)
        a_mid = rem.astype(BF16)
        a_lo = (rem - a_mid.astype(F32)).astype(BF16)
        b = _dot(tri, a_hi) + _dot(tri, a_mid) + _dot(tri, a_lo)
        b_ref[h] = b
        kf_ref[h] = k

        st = st_ref[h]
        o = _dot_nt((q * jnp.exp2(b)).astype(BF16), st.astype(BF16))

        scores = jnp.zeros((C, C), F32)
        w = C // 2
        while w >= diag:
            blk = 2 * w
            beta = _block_rows(b_ref.at[h], range(w, C, blk), blk)
            right = (rix % blk) >= w
            m = (jnp.where(right, q, k) * jnp.exp2(-jnp.abs(b - beta))).astype(BF16)
            keep = ((row // blk) == (col // blk)) & ((row % blk) >= w) & ((col % blk) < w)
            scores = scores + jnp.where(keep, _dot_nt(m, m), 0.0)
            w //= 2

        dsum = jnp.zeros((C, C), F32)
        for s in range(diag):
            rows = range(s, C, diag)
            bs = _block_rows(b_ref.at[h], rows, diag)
            ks = _block_rows(kf_ref.at[h], rows, diag)
            e = jnp.exp2(jnp.where((rix % diag) >= s, b - bs, -jnp.inf))
            p = (q * ks * e).astype(BF16)
            sel = ((dcol % diag) == s).astype(BF16)
            dsum = dsum + _dot(p, sel)
        scores = scores + jnp.where((row // diag) == (col // diag), dsum, 0.0)

        o = o + _dot(scores.astype(BF16), v)

        b_last = b[C - 1:C, :]
        k_dec = (k * jnp.exp2(b_last - b)).astype(BF16)
        st_ref[h] = st * jnp.exp2(b_last) + _dot_tn(v, k_dec)

        r = r_ref[0, :, vs_].astype(F32)
        o_ref[0, :, vs_] = (_rms(o, gn_ref[...]) * (r * jax.nn.sigmoid(r))).astype(o_ref.dtype)


def gla_core(proj, w_a2p, b_a, g_norm, *, dk, dv):
    B, S, _ = proj.shape
    H = GLA_HEADS
    hk, hv = dk // H, dv // H
    C = GLA_CHUNK
    HS = GLA_HEADS_PER_STEP
    assert S % C == 0 and H % HS == 0
    wk, wv = HS * hk, HS * hv
    k_blk0 = dk // wk
    v_blk0 = 2 * dk // wv
    r_blk0 = (2 * dk + dv) // wv
    a_blk = (2 * dk + 2 * dv) // GLA_RANK_PAD
    kern = functools.partial(_gla_kernel, chunk=C, diag=GLA_DIAG, scale=hk ** -0.5,
                             hk=hk, hv=hv, heads=HS)
    return pl.pallas_call(
        kern,
        grid=(B, H // HS, S // C),
        in_specs=[
            pl.BlockSpec((1, C, wk), lambda b, h, c: (b, c, h)),
            pl.BlockSpec((1, C, wk), lambda b, h, c: (b, c, k_blk0 + h)),
            pl.BlockSpec((1, C, wv), lambda b, h, c: (b, c, v_blk0 + h)),
            pl.BlockSpec((1, C, wv), lambda b, h, c: (b, c, r_blk0 + h)),
            pl.BlockSpec((1, C, GLA_RANK_PAD), lambda b, h, c: (b, c, a_blk)),
            pl.BlockSpec((GLA_RANK_PAD, wk), lambda b, h, c: (0, h)),
            pl.BlockSpec((1, wk), lambda b, h, c: (0, h)),
            pl.BlockSpec((1, hv), lambda b, h, c: (0, 0)),
        ],
        out_specs=pl.BlockSpec((1, C, wv), lambda b, h, c: (b, c, h)),
        out_shape=jax.ShapeDtypeStruct((B, S, dv), BF16),
        scratch_shapes=[pltpu.VMEM((HS, hv, hk), F32), pltpu.VMEM((HS, C, hk), F32),
                        pltpu.VMEM((HS, C, hk), F32)],
        compiler_params=_cparams("parallel", "parallel", "arbitrary"),
        name="gla_core",
    )(proj, proj, proj, proj, proj, w_a2p, b_a.reshape(1, dk), g_norm.reshape(1, hv))


def gla_mixer(x, g, w_in, w_a2, b_a, g_norm, w_out):
    B, S, D = x.shape
    dk, dv = w_a2.shape[1], w_out.shape[0]
    n_main = 2 * dk + 2 * dv
    n_pad = n_main + 256
    w_in_p = jnp.pad(w_in, ((0, 0), (0, n_pad - w_in.shape[1]))).astype(BF16)
    w_a2p = jnp.pad(w_a2, ((0, GLA_RANK_PAD - w_a2.shape[0]), (0, 0))).astype(BF16)
    x2 = x.reshape(B * S, D)
    proj = norm_matmul(x2, g, w_in_p, bn=n_pad // 5)
    o = gla_core(proj.reshape(B, S, n_pad), w_a2p, b_a, g_norm, dk=dk, dv=dv)
    return matmul_residual(o.reshape(B * S, dv), w_out.astype(BF16), x2).reshape(B, S, D)


def _pool_kernel(x_ref, g_ref, w_ref, sc_ref, o_ref, hb_ref, *, bm, gw):
    si = pl.program_id(1)
    HL = POOL_HALO

    @pl.when(si == 0)
    def _():
        hb_ref[0:HL, :] = jnp.zeros((HL, hb_ref.shape[1]), F32)

    @pl.when(si > 0)
    def _():
        hb_ref[0:HL, :] = hb_ref[bm:bm + HL, :]

    x = x_ref[0]
    hb_ref[HL:HL + bm, :] = _rms(x, g_ref[...])

    t = si * bm + lax.broadcasted_iota(jnp.int32, (bm, 1), 0)
    for gi, win in enumerate(POOL_WINDOWS):
        cs = slice(gi * gw, (gi + 1) * gw)
        h = hb_ref[HL:HL + bm, cs]
        acc = h
        for d in range(1, win):
            acc = acc + hb_ref[HL - d:HL - d + bm, cs]
        count = jnp.minimum(t + 1, win).astype(F32)
        y = (acc / count - h).astype(BF16)
        o_ref[0, :, cs] = x[:, cs] + _dot(y, w_ref[gi]) * sc_ref[:, cs]


def pool_mixer(x, g, w_pool, scale, bm=256):
    B, S, D = x.shape
    G, gw, _ = w_pool.shape
    assert S % bm == 0 and G == len(POOL_WINDOWS)
    kern = functools.partial(_pool_kernel, bm=bm, gw=gw)
    return pl.pallas_call(
        kern,
        grid=(B, S // bm),
        in_specs=[
            pl.BlockSpec((1, bm, D), lambda b, s: (b, s, 0)),
            pl.BlockSpec((1, D), lambda b, s: (0, 0)),
            pl.BlockSpec((G, gw, gw), lambda b, s: (0, 0, 0)),
            pl.BlockSpec((1, D), lambda b, s: (0, 0)),
        ],
        out_specs=pl.BlockSpec((1, bm, D), lambda b, s: (b, s, 0)),
        out_shape=jax.ShapeDtypeStruct((B, S, D), F32),
        scratch_shapes=[pltpu.VMEM((POOL_HALO + bm, D), F32)],
        compiler_params=_cparams("parallel", "arbitrary"),
        name="pool_mixer",
    )(x, g.reshape(1, D), w_pool.astype(BF16), scale.reshape(1, D))


def _rel_bucket(rel):
    n = jnp.maximum(rel, 0)
    max_exact = REL_BUCKETS // 2
    nf = jnp.maximum(n, 1).astype(F32)
    large = max_exact + (jnp.log(nf / max_exact) / math.log(REL_MAX_DIST / max_exact)
                         * (REL_BUCKETS - max_exact)).astype(jnp.int32)
    large = jnp.minimum(large, REL_BUCKETS - 1)
    return jnp.where(n < max_exact, n, large)


def _diff_attn_kernel(q_ref, k_ref, v_ref, rb_ref, lam_ref, sg_ref, o_ref,
                      m_ref, l_ref, acc_ref, bias_ref, *, blk, hd, lam_init):
    qi = pl.program_id(2)
    L = 2 * blk
    lanes = m_ref.shape[-1]

    @pl.when(qi == 0)
    def _():
        for c in range(2):
            for off in range(2):
                r = jnp.broadcast_to(rb_ref[0, c, off:off + 1, :], (blk, L))
                bias_ref[c, off] = pltpu.roll(r, L - blk + 1, 1, stride=1, stride_axis=0)[:, :blk]

    qb = q_ref[0]
    qs = (qb[:, :hd], qb[:, hd:])

    m_ref[...] = jnp.full(m_ref.shape, -jnp.inf, F32)
    l_ref[...] = jnp.zeros(l_ref.shape, F32)
    acc_ref[...] = jnp.zeros(acc_ref.shape, F32)

    def step(j, off):
        start = pl.multiple_of(j * blk, blk)
        kb = k_ref[0, pl.ds(start, blk), :]
        vb = v_ref[0, pl.ds(start, blk), :]
        for c in range(2):
            s = _dot_nt(qs[c], kb[:, c * hd:(c + 1) * hd])
            if off is not None:
                s = s + bias_ref[c, off]
            m_old = m_ref[c]
            m_new = jnp.maximum(m_old, jnp.max(s, axis=-1, keepdims=True))
            alpha = jnp.exp(m_old - m_new)
            p = jnp.exp(s - pltpu.repeat(m_new, blk // lanes, axis=1))
            psum = p[:, :lanes]
            for t in range(1, blk // lanes):
                psum = psum + p[:, t * lanes:(t + 1) * lanes]
            l_ref[c] = alpha * l_ref[c] + psum
            acc_ref[c] = (pltpu.repeat(alpha, acc_ref.shape[-1] // lanes, axis=1) * acc_ref[c]
                          + _dot(p.astype(BF16), vb))
            m_ref[c] = m_new

    def far_body(j, carry):
        step(j, None)
        return carry

    lax.fori_loop(0, jnp.maximum(qi - 1, 0), far_body, 0)

    @pl.when(qi >= 1)
    def _():
        step(qi - 1, 1)

    step(qi, 0)

    lp = lam_ref[...]
    lam = (jnp.exp(jnp.sum(lp[0:1] * lp[1:2], axis=-1, keepdims=True))
           - jnp.exp(jnp.sum(lp[2:3] * lp[3:4], axis=-1, keepdims=True)) + lam_init)
    l0 = jnp.sum(l_ref[0], axis=-1, keepdims=True)
    l1 = jnp.sum(l_ref[1], axis=-1, keepdims=True)
    o = acc_ref[0] / l0 - lam * (acc_ref[1] / l1)
    o_ref[0] = (_rms(o, sg_ref[...]) * (1.0 - lam_init)).astype(o_ref.dtype)


def diff_attention(proj, rel_vecs, lam_params, sub_gain, *, d_model, lam_init, blk):
    B, S, _ = proj.shape
    H = DIFF_HEADS
    vd = d_model // H
    hd = vd // 2
    lanes = 128
    assert S % blk == 0 and blk % lanes == 0
    kern = functools.partial(_diff_attn_kernel, blk=blk, hd=hd, lam_init=lam_init)
    return pl.pallas_call(
        kern,
        grid=(B, H, S // blk),
        in_specs=[
            pl.BlockSpec((1, blk, vd), lambda b, h, i: (b, i, h)),
            pl.BlockSpec((1, S, vd), lambda b, h, i: (b, 0, H + h)),
            pl.BlockSpec((1, S, vd), lambda b, h, i: (b, 0, 2 * H + h)),
            pl.BlockSpec((1, 2, 2, 2 * blk), lambda b, h, i: (h, 0, 0, 0)),
            pl.BlockSpec((4, hd), lambda b, h, i: (0, 0)),
            pl.BlockSpec((1, vd), lambda b, h, i: (0, 0)),
        ],
        out_specs=pl.BlockSpec((1, blk, vd), lambda b, h, i: (b, i, h)),
        out_shape=jax.ShapeDtypeStruct((B, S, d_model), BF16),
        scratch_shapes=[
            pltpu.VMEM((2, blk, lanes), F32),
            pltpu.VMEM((2, blk, lanes), F32),
            pltpu.VMEM((2, blk, vd), F32),
            pltpu.VMEM((2, 2, blk, blk), F32),
        ],
        compiler_params=_cparams("parallel", "parallel", "arbitrary"),
        name="diff_attention",
    )(proj, proj, proj, rel_vecs, lam_params, sub_gain.reshape(1, vd))


def _rel_bias_vectors(rel_table, blk):
    H2 = rel_table.shape[1]
    far = rel_table[REL_BUCKETS - 1]
    y = jnp.arange(2 * blk)
    vecs = []
    for off in (0, 1):
        rel = off * blk + blk - 1 - y
        v = rel_table[_rel_bucket(rel)].astype(F32) - far
        vecs.append(jnp.where((rel >= 0)[:, None], v, -jnp.inf).T)
    return jnp.stack(vecs, axis=1).reshape(H2 // 2, 2, 2, 2 * blk)


def diff_attn_mixer(x, g, w_in, q_gain, k_gain, lam_params, sub_gain, w_out, rel_table, layer_idx,
                    blk=512):
    B, S, D = x.shape
    assert blk >= REL_MAX_DIST
    hd = D // DIFF_HEADS // 2
    lam_init = 0.8 - 0.6 * math.exp(-0.3 * layer_idx)
    n_heads2 = D // hd
    head_gain = jnp.concatenate([jnp.tile(q_gain, n_heads2) * (hd ** -0.5),
                                 jnp.tile(k_gain, n_heads2), jnp.ones((D,), F32)])
    x2 = x.reshape(B * S, D)
    proj = norm_matmul(x2, g, w_in.astype(BF16), head_gain=head_gain, head_norm_cols=2 * D,
                       head_dim=hd)
    o = diff_attention(proj.reshape(B, S, 3 * D), _rel_bias_vectors(rel_table, blk), lam_params,
                       sub_gain, d_model=D, lam_init=lam_init, blk=blk)
    return matmul_residual(o.reshape(B * S, D), w_out.astype(BF16), x2).reshape(B, S, D)


def kernel(x, norm_g, gla_w_in, gla_w_a2, gla_b_a, gla_g_norm, gla_w_out, pool_w, pool_scale,
           diff_w_in, diff_q_gain, diff_k_gain, diff_lambda, diff_sub_gain, diff_w_out, rel_bias,
           ffn_w_gu, ffn_w_down):
    B, S, D = x.shape
    depth = norm_g.shape[0]
    for i in range(depth):
        kind, slot = i % N_MIXERS, i // N_MIXERS
        if kind == 0:
            x = gla_mixer(x, norm_g[i, 0], gla_w_in[slot], gla_w_a2[slot], gla_b_a[slot],
                          gla_g_norm[slot], gla_w_out[slot])
        elif kind == 1:
            x = pool_mixer(x, norm_g[i, 0], pool_w[slot], pool_scale[slot])
        else:
            x = diff_attn_mixer(x, norm_g[i, 0], diff_w_in[slot], diff_q_gain[slot],
                                diff_k_gain[slot], diff_lambda[slot], diff_sub_gain[slot],
                                diff_w_out[slot], rel_bias, i)
        x = ffn(x.reshape(B * S, D), norm_g[i, 1], ffn_w_gu[i].astype(BF16),
                ffn_w_down[i].astype(BF16)).reshape(B, S, D)
    return x
```

```python
import functools
import math

import jax
import jax.numpy as jnp
from jax import lax
from jax.experimental import pallas as pl
from jax.experimental.pallas import tpu as pltpu

F32 = jnp.float32
BF16 = jnp.bfloat16

EPS = 1e-6
N_MIXERS = 3

GLA_HEADS = 4
GLA_RANK = 16
GLA_TAU = 16.0
GLA_CHUNK = 128
GLA_DIAG = 8
LOG2_E = math.log2(math.e)
GLA_RANK_PAD = 128
GLA_HEADS_PER_STEP = 2
GLA_IN_PAD = 256

POOL_WINDOWS = (2, 4, 8, 16)
POOL_HALO = 16

DIFF_HEADS = 8
DIFF_STRIP = 32
REL_BUCKETS = 32
REL_MAX_DIST = 128

VMEM_LIMIT_BYTES = 56 * 1024 * 1024


def _cparams(*sem):
    return pltpu.CompilerParams(dimension_semantics=sem, vmem_limit_bytes=VMEM_LIMIT_BYTES)


def _rms(x, g):
    ms = jnp.mean(x * x, axis=-1, keepdims=True)
    return x * lax.rsqrt(ms + EPS) * g


def _dot(a, b):
    return jnp.dot(a, b, preferred_element_type=F32)


def _dot_nt(a, b):
    return lax.dot_general(a, b, (((1,), (1,)), ((), ())), preferred_element_type=F32)


def _dot_tn(a, b):
    return lax.dot_general(a, b, (((0,), (0,)), ((), ())), preferred_element_type=F32)


def _norm_matmul_kernel(x_ref, g_ref, w_ref, hg_ref, o_ref, h_ref, *, head_norm_blocks, head_dim):
    j = pl.program_id(1)

    @pl.when(j == 0)
    def _():
        h_ref[...] = _rms(x_ref[...], g_ref[...]).astype(BF16)

    acc = _dot(h_ref[...], w_ref[...])
    if head_norm_blocks == 0:
        o_ref[...] = acc.astype(o_ref.dtype)
    else:
        @pl.when(j < head_norm_blocks)
        def _():
            for c in range(acc.shape[1] // head_dim):
                sl = slice(c * head_dim, (c + 1) * head_dim)
                o_ref[:, sl] = _rms(acc[:, sl], hg_ref[:, sl]).astype(o_ref.dtype)

        @pl.when(j >= head_norm_blocks)
        def _():
            o_ref[...] = acc.astype(o_ref.dtype)


def norm_matmul(x, g, w, layer, head_gain=None, head_norm_cols=0, head_dim=128, bm=1024, bn=1024):
    M, D = x.shape
    N = w.shape[2]
    bm = min(bm, M)
    assert M % bm == 0 and N % bn == 0 and head_norm_cols % bn == 0
    if head_gain is None:
        head_gain = jnp.ones((N,), F32)
    kern = functools.partial(_norm_matmul_kernel, head_norm_blocks=head_norm_cols // bn,
                             head_dim=head_dim)
    return pl.pallas_call(
        kern,
        grid=(M // bm, N // bn),
        in_specs=[
            pl.BlockSpec((bm, D), lambda i, j: (i, 0)),
            pl.BlockSpec((1, D), lambda i, j: (0, 0)),
            pl.BlockSpec((None, D, bn), lambda i, j: (layer, 0, j)),
            pl.BlockSpec((1, bn), lambda i, j: (0, j)),
        ],
        out_specs=pl.BlockSpec((bm, bn), lambda i, j: (i, j)),
        out_shape=jax.ShapeDtypeStruct((M, N), BF16),
        scratch_shapes=[pltpu.VMEM((bm, D), BF16)],
        compiler_params=_cparams("parallel", "arbitrary"),
        name="norm_matmul",
    )(x, g.reshape(1, D), w, head_gain.reshape(1, N))


def _matmul_residual_kernel(a_ref, w_ref, x_ref, o_ref):
    o_ref[...] = x_ref[...] + _dot(a_ref[...], w_ref[...])


def matmul_residual(a, w, layer, x, bm=1024, bn=1024):
    M, K = a.shape
    N = w.shape[2]
    bm = min(bm, M)
    assert M % bm == 0 and N % bn == 0
    return pl.pallas_call(
        _matmul_residual_kernel,
        grid=(M // bm, N // bn),
        in_specs=[
            pl.BlockSpec((bm, K), lambda i, j: (i, 0)),
            pl.BlockSpec((None, K, bn), lambda i, j: (layer, 0, j)),
            pl.BlockSpec((bm, bn), lambda i, j: (i, j)),
        ],
        out_specs=pl.BlockSpec((bm, bn), lambda i, j: (i, j)),
        out_shape=jax.ShapeDtypeStruct((M, N), F32),
        compiler_params=_cparams("parallel", "arbitrary"),
        name="matmul_residual",
    )(a, w, x)


def _ffn_kernel(x_ref, g_ref, wg_ref, wu_ref, wd_ref, o_ref, h_ref):
    j = pl.program_id(1)

    @pl.when(j == 0)
    def _():
        x = x_ref[...]
        h_ref[...] = _rms(x, g_ref[...]).astype(BF16)
        o_ref[...] = x

    h = h_ref[...]
    gate = _dot(h, wg_ref[...])
    up = _dot(h, wu_ref[...])
    act = (gate * jax.nn.sigmoid(gate) * up).astype(BF16)
    o_ref[...] += _dot(act, wd_ref[...])


def ffn(x, g, w_gu, w_down, layer, bm=512, bh=512):
    M, D = x.shape
    FH = w_down.shape[1]
    bm = min(bm, M)
    assert M % bm == 0 and FH % bh == 0
    nh = FH // bh
    return pl.pallas_call(
        _ffn_kernel,
        grid=(M // bm, nh),
        in_specs=[
            pl.BlockSpec((bm, D), lambda i, j: (i, 0)),
            pl.BlockSpec((1, D), lambda i, j: (0, 0)),
            pl.BlockSpec((None, D, bh), lambda i, j: (layer, 0, j)),
            pl.BlockSpec((None, D, bh), lambda i, j: (layer, 0, j + nh)),
            pl.BlockSpec((None, bh, D), lambda i, j: (layer, j, 0)),
        ],
        out_specs=pl.BlockSpec((bm, D), lambda i, j: (i, 0)),
        out_shape=jax.ShapeDtypeStruct((M, D), F32),
        scratch_shapes=[pltpu.VMEM((bm, D), BF16)],
        compiler_params=_cparams("parallel", "arbitrary"),
        name="ffn",
    )(x, g.reshape(1, D), w_gu, w_gu, w_down)


def _block_rows(ref, rows, span):
    return jnp.concatenate(
        [jnp.broadcast_to(ref[r:r + 1, :], (span, ref.shape[1])) for r in rows], axis=0)


def _gla_kernel(q_ref, k_ref, v_ref, r_ref, a_ref, wa_ref, ba_ref, gn_ref, o_ref, st_ref,
                b_ref, kf_ref, *, chunk, diag, scale, hk, hv, heads):
    C = chunk

    @pl.when(pl.program_id(2) == 0)
    def _():
        st_ref[...] = jnp.zeros_like(st_ref)

    row = lax.broadcasted_iota(jnp.int32, (C, C), 0)
    col = lax.broadcasted_iota(jnp.int32, (C, C), 1)
    rix = lax.broadcasted_iota(jnp.int32, (C, 1), 0)
    dcol = lax.broadcasted_iota(jnp.int32, (hk, C), 1)
    tri = (row >= col).astype(BF16)
    a_lr = a_ref[0]

    for h in range(heads):
        ks_, vs_ = slice(h * hk, (h + 1) * hk), slice(h * hv, (h + 1) * hv)
        q = q_ref[0, :, ks_].astype(F32) * scale
        k = k_ref[0, :, ks_].astype(F32)
        v = v_ref[0, :, vs_]

        z = _dot(a_lr, wa_ref[:, ks_]) + ba_ref[:, ks_]
        log_a = (jnp.minimum(z, 0.0) - jnp.log(1.0 + jnp.exp(-jnp.abs(z)))) * (LOG2_E / GLA_TAU)
        a_hi = log_a.astype(BF16)
        rem = log_a - a_hi.astype(F32)
        a_mid = rem.astype(BF16)
        a_lo = (rem - a_mid.astype(F32)).astype(BF16)
        b = _dot(tri, a_hi) + _dot(tri, a_mid) + _dot(tri, a_lo)
        b_ref[h] = b
        kf_ref[h] = k

        st = st_ref[h]
        o = _dot_nt((q * jnp.exp2(b)).astype(BF16), st.astype(BF16))

        scores = jnp.zeros((C, C), F32)
        w = C // 2
        while w >= diag:
            blk = 2 * w
            beta = _block_rows(b_ref.at[h], range(w, C, blk), blk)
            right = (rix % blk) >= w
            m = (jnp.where(right, q, k) * jnp.exp2(-jnp.abs(b - beta))).astype(BF16)
            keep = ((row // blk) == (col // blk)) & ((row % blk) >= w) & ((col % blk) < w)
            scores = scores + jnp.where(keep, _dot_nt(m, m), 0.0)
            w //= 2

        dsum = jnp.zeros((C, C), F32)
        for s in range(diag):
            rows = range(s, C, diag)
            bs = _block_rows(b_ref.at[h], rows, diag)
            ks = _block_rows(kf_ref.at[h], rows, diag)
            e = jnp.exp2(jnp.where((rix % diag) >= s, b - bs, -jnp.inf))
            p = (q * ks * e).astype(BF16)
            sel = ((dcol % diag) == s).astype(BF16)
            dsum = dsum + _dot(p, sel)
        scores = scores + jnp.where((row // diag) == (col // diag), dsum, 0.0)

        o = o + _dot(scores.astype(BF16), v)

        b_last = b[C - 1:C, :]
        k_dec = (k * jnp.exp2(b_last - b)).astype(BF16)
        st_ref[h] = st * jnp.exp2(b_last) + _dot_tn(v, k_dec)

        r = r_ref[0, :, vs_].astype(F32)
        o_ref[0, :, vs_] = (_rms(o, gn_ref[...]) * (r * jax.nn.sigmoid(r))).astype(o_ref.dtype)


def gla_core(proj, w_a2p, layer, b_a, g_norm, *, dk, dv):
    B, S, _ = proj.shape
    H = GLA_HEADS
    hk, hv = dk // H, dv // H
    C = GLA_CHUNK
    HS = GLA_HEADS_PER_STEP
    assert S % C == 0 and H % HS == 0
    wk, wv = HS * hk, HS * hv
    k_blk0 = dk // wk
    v_blk0 = 2 * dk // wv
    r_blk0 = (2 * dk + dv) // wv
    a_blk = (2 * dk + 2 * dv) // GLA_RANK_PAD
    kern = functools.partial(_gla_kernel, chunk=C, diag=GLA_DIAG, scale=hk ** -0.5,
                             hk=hk, hv=hv, heads=HS)
    return pl.pallas_call(
        kern,
        grid=(B, H // HS, S // C),
        in_specs=[
            pl.BlockSpec((1, C, wk), lambda b, h, c: (b, c, h)),
            pl.BlockSpec((1, C, wk), lambda b, h, c: (b, c, k_blk0 + h)),
            pl.BlockSpec((1, C, wv), lambda b, h, c: (b, c, v_blk0 + h)),
            pl.BlockSpec((1, C, wv), lambda b, h, c: (b, c, r_blk0 + h)),
            pl.BlockSpec((1, C, GLA_RANK_PAD), lambda b, h, c: (b, c, a_blk)),
            pl.BlockSpec((None, GLA_RANK_PAD, wk), lambda b, h, c: (layer, 0, h)),
            pl.BlockSpec((1, wk), lambda b, h, c: (0, h)),
            pl.BlockSpec((1, hv), lambda b, h, c: (0, 0)),
        ],
        out_specs=pl.BlockSpec((1, C, wv), lambda b, h, c: (b, c, h)),
        out_shape=jax.ShapeDtypeStruct((B, S, dv), BF16),
        scratch_shapes=[pltpu.VMEM((HS, hv, hk), F32), pltpu.VMEM((HS, C, hk), F32),
                        pltpu.VMEM((HS, C, hk), F32)],
        compiler_params=_cparams("parallel", "parallel", "arbitrary"),
        name="gla_core",
    )(proj, proj, proj, proj, proj, w_a2p, b_a.reshape(1, dk), g_norm.reshape(1, hv))


def gla_weights(w_in, w_a2, w_out):
    dk, dv = w_a2.shape[2], w_out.shape[1]
    n_pad = 2 * dk + 2 * dv + GLA_IN_PAD
    w_in_p = jnp.pad(w_in, ((0, 0), (0, 0), (0, n_pad - w_in.shape[2]))).astype(BF16)
    w_a2p = jnp.pad(w_a2, ((0, 0), (0, GLA_RANK_PAD - w_a2.shape[1]), (0, 0))).astype(BF16)
    return w_in_p, w_a2p, w_out.astype(BF16)


def gla_mixer(x, g, weights, layer, b_a, g_norm):
    w_in_p, w_a2p, w_out = weights
    B, S, D = x.shape
    dk, dv = w_a2p.shape[2], w_out.shape[1]
    n_pad = w_in_p.shape[2]
    x2 = x.reshape(B * S, D)
    proj = norm_matmul(x2, g, w_in_p, layer, bn=n_pad // 5)
    o = gla_core(proj.reshape(B, S, n_pad), w_a2p, layer, b_a, g_norm, dk=dk, dv=dv)
    return matmul_residual(o.reshape(B * S, dv), w_out, layer, x2).reshape(B, S, D)


def _pool_kernel(x_ref, g_ref, w_ref, sc_ref, o_ref, hb_ref, *, bm, gw):
    si = pl.program_id(1)
    HL = POOL_HALO

    @pl.when(si == 0)
    def _():
        hb_ref[0:HL, :] = jnp.zeros((HL, hb_ref.shape[1]), F32)

    @pl.when(si > 0)
    def _():
        hb_ref[0:HL, :] = hb_ref[bm:bm + HL, :]

    x = x_ref[0]
    hb_ref[HL:HL + bm, :] = _rms(x, g_ref[...])

    t = si * bm + lax.broadcasted_iota(jnp.int32, (bm, 1), 0)
    for gi, win in enumerate(POOL_WINDOWS):
        cs = slice(gi * gw, (gi + 1) * gw)
        h = hb_ref[HL:HL + bm, cs]
        acc = h
        for d in range(1, win):
            acc = acc + hb_ref[HL - d:HL - d + bm, cs]
        count = jnp.minimum(t + 1, win).astype(F32)
        y = (acc / count - h).astype(BF16)
        o_ref[0, :, cs] = x[:, cs] + _dot(y, w_ref[gi]) * sc_ref[:, cs]


def pool_mixer(x, g, w_pool, scale, bm=256):
    B, S, D = x.shape
    G, gw, _ = w_pool.shape
    assert S % bm == 0 and G == len(POOL_WINDOWS)
    kern = functools.partial(_pool_kernel, bm=bm, gw=gw)
    return pl.pallas_call(
        kern,
        grid=(B, S // bm),
        in_specs=[
            pl.BlockSpec((1, bm, D), lambda b, s: (b, s, 0)),
            pl.BlockSpec((1, D), lambda b, s: (0, 0)),
            pl.BlockSpec((G, gw, gw), lambda b, s: (0, 0, 0)),
            pl.BlockSpec((1, D), lambda b, s: (0, 0)),
        ],
        out_specs=pl.BlockSpec((1, bm, D), lambda b, s: (b, s, 0)),
        out_shape=jax.ShapeDtypeStruct((B, S, D), F32),
        scratch_shapes=[pltpu.VMEM((POOL_HALO + bm, D), F32)],
        compiler_params=_cparams("parallel", "arbitrary"),
        name="pool_mixer",
    )(x, g.reshape(1, D), w_pool.astype(BF16), scale.reshape(1, D))


def _rel_bucket(rel):
    n = jnp.maximum(rel, 0)
    max_exact = REL_BUCKETS // 2
    nf = jnp.maximum(n, 1).astype(F32)
    large = max_exact + (jnp.log(nf / max_exact) / math.log(REL_MAX_DIST / max_exact)
                         * (REL_BUCKETS - max_exact)).astype(jnp.int32)
    large = jnp.minimum(large, REL_BUCKETS - 1)
    return jnp.where(n < max_exact, n, large)


def _diff_attn_kernel(q_ref, k_ref, v_ref, rb_ref, lam_ref, sg_ref, o_ref,
                      m_ref, l_ref, acc_ref, bias_ref, s_ref, p_ref, a_ref, *, blk, hd, lam_init):
    qi = pl.program_id(2)
    L = 2 * blk
    lanes = m_ref.shape[-1]

    @pl.when(qi == 0)
    def _():
        for c in range(2):
            for off in range(2):
                r = jnp.broadcast_to(rb_ref[0, c, off:off + 1, :], (blk, L))
                bias_ref[c, off] = pltpu.roll(r, L - blk + 1, 1, stride=1, stride_axis=0)[:, :blk]

    qb = q_ref[0]
    qs = (qb[:, :hd], qb[:, hd:])

    m_ref[...] = jnp.full(m_ref.shape, -jnp.inf, F32)
    l_ref[...] = jnp.zeros(l_ref.shape, F32)
    acc_ref[...] = jnp.zeros(acc_ref.shape, F32)

    def logits(j, c):
        start = pl.multiple_of(j * blk, blk)
        s_ref[c] = _dot_nt(qs[c], k_ref[0, pl.ds(start, blk), c * hd:(c + 1) * hd])

    def step(j, off, has_next=True):
        start = pl.multiple_of(j * blk, blk)
        vb = v_ref[0, pl.ds(start, blk), :]
        for c in range(2):
            for r0 in range(0, blk, DIFF_STRIP):
                rows = slice(r0, r0 + DIFF_STRIP)
                s = s_ref[c, rows, :]
                if off is not None:
                    s = s + bias_ref[c, off, rows, :]
                m_old = m_ref[c, rows, :]
                m_new = jnp.maximum(m_old, jnp.max(s, axis=-1, keepdims=True))
                alpha = jnp.exp2(m_old - m_new)
                p = jnp.exp2(s - jnp.concatenate([m_new] * (blk // lanes), axis=1))
                psum = p[:, :lanes]
                for t in range(1, blk // lanes):
                    psum = psum + p[:, t * lanes:(t + 1) * lanes]
                l_ref[c, rows, :] = alpha * l_ref[c, rows, :] + psum
                p_ref[c, rows, :] = p.astype(BF16)
                a_ref[c, rows, :] = alpha
                m_ref[c, rows, :] = m_new
            if has_next:
                logits(j + 1, c)
            alpha = jnp.concatenate([a_ref[c]] * (acc_ref.shape[-1] // lanes), axis=1)
            acc_ref[c] = alpha * acc_ref[c] + _dot(p_ref[c], vb)

    def far_body(j, carry):
        step(j, None)
        return carry

    logits(0, 0)
    logits(0, 1)
    lax.fori_loop(0, jnp.maximum(qi - 1, 0), far_body, 0)

    @pl.when(qi >= 1)
    def _():
        step(qi - 1, 1)

    step(qi, 0, has_next=False)

    lp = lam_ref[...]
    lam = (jnp.exp(jnp.sum(lp[0:1] * lp[1:2], axis=-1, keepdims=True))
           - jnp.exp(jnp.sum(lp[2:3] * lp[3:4], axis=-1, keepdims=True)) + lam_init)
    l0 = jnp.sum(l_ref[0], axis=-1, keepdims=True)
    l1 = jnp.sum(l_ref[1], axis=-1, keepdims=True)
    o = acc_ref[0] / l0 - lam * (acc_ref[1] / l1)
    o_ref[0] = (_rms(o, sg_ref[...]) * (1.0 - lam_init)).astype(o_ref.dtype)


def diff_attention(proj, rel_vecs, lam_params, sub_gain, *, d_model, lam_init, blk):
    B, S, _ = proj.shape
    H = DIFF_HEADS
    vd = d_model // H
    hd = vd // 2
    lanes = 128
    assert S % blk == 0 and blk % lanes == 0
    kern = functools.partial(_diff_attn_kernel, blk=blk, hd=hd, lam_init=lam_init)
    return pl.pallas_call(
        kern,
        grid=(B, H, S // blk),
        in_specs=[
            pl.BlockSpec((1, blk, vd), lambda b, h, i: (b, i, h)),
            pl.BlockSpec((1, S, vd), lambda b, h, i: (b, 0, H + h)),
            pl.BlockSpec((1, S, vd), lambda b, h, i: (b, 0, 2 * H + h)),
            pl.BlockSpec((1, 2, 2, 2 * blk), lambda b, h, i: (h, 0, 0, 0)),
            pl.BlockSpec((4, hd), lambda b, h, i: (0, 0)),
            pl.BlockSpec((1, vd), lambda b, h, i: (0, 0)),
        ],
        out_specs=pl.BlockSpec((1, blk, vd), lambda b, h, i: (b, i, h)),
        out_shape=jax.ShapeDtypeStruct((B, S, d_model), BF16),
        scratch_shapes=[
            pltpu.VMEM((2, blk, lanes), F32),
            pltpu.VMEM((2, blk, lanes), F32),
            pltpu.VMEM((2, blk, vd), F32),
            pltpu.VMEM((2, 2, blk, blk), F32),
            pltpu.VMEM((2, blk, blk), F32),
            pltpu.VMEM((2, blk, blk), BF16),
            pltpu.VMEM((2, blk, lanes), F32),
        ],
        compiler_params=_cparams("parallel", "parallel", "arbitrary"),
        name="diff_attention",
    )(proj, proj, proj, rel_vecs, lam_params, sub_gain.reshape(1, vd))


def _rel_bias_vectors(rel_table, blk):
    H2 = rel_table.shape[1]
    far = rel_table[REL_BUCKETS - 1]
    y = jnp.arange(2 * blk)
    vecs = []
    for off in (0, 1):
        rel = off * blk + blk - 1 - y
        v = rel_table[_rel_bucket(rel)].astype(F32) - far
        vecs.append(jnp.where((rel >= 0)[:, None], v * LOG2_E, -jnp.inf).T)
    return jnp.stack(vecs, axis=1).reshape(H2 // 2, 2, 2, 2 * blk)


def diff_attn_mixer(x, g, w_in, w_out, layer, q_gain, k_gain, lam_params, sub_gain, rel_table,
                    layer_idx, blk=512):
    B, S, D = x.shape
    assert blk >= REL_MAX_DIST
    hd = D // DIFF_HEADS // 2
    lam_init = 0.8 - 0.6 * math.exp(-0.3 * layer_idx)
    n_heads2 = D // hd
    head_gain = jnp.concatenate([jnp.tile(q_gain, n_heads2) * (hd ** -0.5 * LOG2_E),
                                 jnp.tile(k_gain, n_heads2), jnp.ones((D,), F32)])
    x2 = x.reshape(B * S, D)
    proj = norm_matmul(x2, g, w_in, layer, head_gain=head_gain, head_norm_cols=2 * D, head_dim=hd)
    o = diff_attention(proj.reshape(B, S, 3 * D), _rel_bias_vectors(rel_table, blk), lam_params,
                       sub_gain, d_model=D, lam_init=lam_init, blk=blk)
    return matmul_residual(o.reshape(B * S, D), w_out, layer, x2).reshape(B, S, D)


def kernel(x, norm_g, gla_w_in, gla_w_a2, gla_b_a, gla_g_norm, gla_w_out, pool_w, pool_scale,
           diff_w_in, diff_q_gain, diff_k_gain, diff_lambda, diff_sub_gain, diff_w_out, rel_bias,
           ffn_w_gu, ffn_w_down):
    B, S, D = x.shape
    depth = norm_g.shape[0]
    gla_w = gla_weights(gla_w_in, gla_w_a2, gla_w_out)
    diff_w_in_b, diff_w_out_b = diff_w_in.astype(BF16), diff_w_out.astype(BF16)
    ffn_w_gu_b, ffn_w_down_b = ffn_w_gu.astype(BF16), ffn_w_down.astype(BF16)
    for i in range(depth):
        kind, slot = i % N_MIXERS, i // N_MIXERS
        if kind == 0:
            x = gla_mixer(x, norm_g[i, 0], gla_w, slot, gla_b_a[slot], gla_g_norm[slot])
        elif kind == 1:
            x = pool_mixer(x, norm_g[i, 0], pool_w[slot], pool_scale[slot])
        else:
            x = diff_attn_mixer(x, norm_g[i, 0], diff_w_in_b, diff_w_out_b, slot, diff_q_gain[slot],
                                diff_k_gain[slot], diff_lambda[slot], diff_sub_gain[slot],
                                rel_bias, i)
        x = ffn(x.reshape(B * S, D), norm_g[i, 1], ffn_w_gu_b, ffn_w_down_b, i).reshape(B, S, D)
    return x
```

```python
import functools
import math

import jax
import jax.numpy as jnp
from jax import lax
from jax.experimental import pallas as pl
from jax.experimental.pallas import tpu as pltpu

F32 = jnp.float32
BF16 = jnp.bfloat16

EPS = 1e-6
N_MIXERS = 3

GLA_HEADS = 4
GLA_RANK = 16
GLA_TAU = 16.0
GLA_CHUNK = 128
GLA_DIAG = 8
LOG2_E = math.log2(math.e)
GLA_RANK_PAD = 128
GLA_HEADS_PER_STEP = 4
GLA_IN_PAD = 256

POOL_WINDOWS = (2, 4, 8, 16)
POOL_HALO = 16

DIFF_HEADS = 8
DIFF_STRIP = 32
DIFF_HEADS_PER_STEP = 2
REL_BUCKETS = 32
REL_MAX_DIST = 128

VMEM_LIMIT_BYTES = 56 * 1024 * 1024


def _cparams(*sem):
    return pltpu.CompilerParams(dimension_semantics=sem, vmem_limit_bytes=VMEM_LIMIT_BYTES)


def _rms(x, g):
    ms = jnp.mean(x * x, axis=-1, keepdims=True)
    return x * lax.rsqrt(ms + EPS) * g


def _dot(a, b):
    return jnp.dot(a, b, preferred_element_type=F32)


def _dot_nt(a, b):
    return lax.dot_general(a, b, (((1,), (1,)), ((), ())), preferred_element_type=F32)


def _dot_tn(a, b):
    return lax.dot_general(a, b, (((0,), (0,)), ((), ())), preferred_element_type=F32)


def _norm_matmul_kernel(x_ref, g_ref, w_ref, hg_ref, o_ref, h_ref, *, head_norm_blocks, head_dim):
    j = pl.program_id(1)

    @pl.when(j == 0)
    def _():
        h_ref[...] = _rms(x_ref[...], g_ref[...]).astype(BF16)

    acc = _dot(h_ref[...], w_ref[...])
    if head_norm_blocks == 0:
        o_ref[...] = acc.astype(o_ref.dtype)
    else:
        @pl.when(j < head_norm_blocks)
        def _():
            for c in range(acc.shape[1] // head_dim):
                sl = slice(c * head_dim, (c + 1) * head_dim)
                o_ref[:, sl] = _rms(acc[:, sl], hg_ref[:, sl]).astype(o_ref.dtype)

        @pl.when(j >= head_norm_blocks)
        def _():
            o_ref[...] = acc.astype(o_ref.dtype)


def norm_matmul(x, g, w, layer, head_gain=None, head_norm_cols=0, head_dim=128, bm=1024, bn=1024):
    M, D = x.shape
    N = w.shape[2]
    bm = min(bm, M)
    assert M % bm == 0 and N % bn == 0 and head_norm_cols % bn == 0
    if head_gain is None:
        head_gain = jnp.ones((N,), F32)
    kern = functools.partial(_norm_matmul_kernel, head_norm_blocks=head_norm_cols // bn,
                             head_dim=head_dim)
    return pl.pallas_call(
        kern,
        grid=(M // bm, N // bn),
        in_specs=[
            pl.BlockSpec((bm, D), lambda i, j: (i, 0)),
            pl.BlockSpec((1, D), lambda i, j: (0, 0)),
            pl.BlockSpec((None, D, bn), lambda i, j: (layer, 0, j)),
            pl.BlockSpec((1, bn), lambda i, j: (0, j)),
        ],
        out_specs=pl.BlockSpec((bm, bn), lambda i, j: (i, j)),
        out_shape=jax.ShapeDtypeStruct((M, N), BF16),
        scratch_shapes=[pltpu.VMEM((bm, D), BF16)],
        compiler_params=_cparams("parallel", "arbitrary"),
        name="norm_matmul",
    )(x, g.reshape(1, D), w, head_gain.reshape(1, N))


def _matmul_residual_kernel(a_ref, w_ref, x_ref, o_ref):
    o_ref[...] = x_ref[...] + _dot(a_ref[...], w_ref[...])


def matmul_residual(a, w, layer, x, bm=1024, bn=1024):
    M, K = a.shape
    N = w.shape[2]
    bm = min(bm, M)
    assert M % bm == 0 and N % bn == 0
    return pl.pallas_call(
        _matmul_residual_kernel,
        grid=(M // bm, N // bn),
        in_specs=[
            pl.BlockSpec((bm, K), lambda i, j: (i, 0)),
            pl.BlockSpec((None, K, bn), lambda i, j: (layer, 0, j)),
            pl.BlockSpec((bm, bn), lambda i, j: (i, j)),
        ],
        out_specs=pl.BlockSpec((bm, bn), lambda i, j: (i, j)),
        out_shape=jax.ShapeDtypeStruct((M, N), F32),
        compiler_params=_cparams("parallel", "arbitrary"),
        name="matmul_residual",
    )(a, w, x)


def _ffn_kernel(x_ref, g_ref, wg_ref, wu_ref, wd_ref, o_ref, h_ref):
    j = pl.program_id(1)

    @pl.when(j == 0)
    def _():
        x = x_ref[...]
        h_ref[...] = _rms(x, g_ref[...]).astype(BF16)
        o_ref[...] = x

    h = h_ref[...]
    gate = _dot(h, wg_ref[...])
    up = _dot(h, wu_ref[...])
    act = (gate * jax.nn.sigmoid(gate) * up).astype(BF16)
    o_ref[...] += _dot(act, wd_ref[...])


def ffn(x, g, w_gu, w_down, layer, bm=512, bh=512):
    M, D = x.shape
    FH = w_down.shape[1]
    bm = min(bm, M)
    assert M % bm == 0 and FH % bh == 0
    nh = FH // bh
    return pl.pallas_call(
        _ffn_kernel,
        grid=(M // bm, nh),
        in_specs=[
            pl.BlockSpec((bm, D), lambda i, j: (i, 0)),
            pl.BlockSpec((1, D), lambda i, j: (0, 0)),
            pl.BlockSpec((None, D, bh), lambda i, j: (layer, 0, j)),
            pl.BlockSpec((None, D, bh), lambda i, j: (layer, 0, j + nh)),
            pl.BlockSpec((None, bh, D), lambda i, j: (layer, j, 0)),
        ],
        out_specs=pl.BlockSpec((bm, D), lambda i, j: (i, 0)),
        out_shape=jax.ShapeDtypeStruct((M, D), F32),
        scratch_shapes=[pltpu.VMEM((bm, D), BF16)],
        compiler_params=_cparams("parallel", "arbitrary"),
        name="ffn",
    )(x, g.reshape(1, D), w_gu, w_gu, w_down)


def _block_rows(ref, rows, span):
    return jnp.concatenate(
        [jnp.broadcast_to(ref[r:r + 1, :], (span, ref.shape[1])) for r in rows], axis=0)


def _gla_kernel(q_ref, k_ref, v_ref, r_ref, a_ref, wa_ref, ba_ref, gn_ref, o_ref, st_ref,
                b_ref, kf_ref, *, chunk, diag, scale, hk, hv, heads):
    C = chunk

    @pl.when(pl.program_id(2) == 0)
    def _():
        st_ref[...] = jnp.zeros_like(st_ref)

    row = lax.broadcasted_iota(jnp.int32, (C, C), 0)
    col = lax.broadcasted_iota(jnp.int32, (C, C), 1)
    rix = lax.broadcasted_iota(jnp.int32, (C, 1), 0)
    dcol = lax.broadcasted_iota(jnp.int32, (hk, C), 1)
    tri = (row >= col).astype(BF16)
    a_lr = a_ref[0]

    for h in range(heads):
        ks_, vs_ = slice(h * hk, (h + 1) * hk), slice(h * hv, (h + 1) * hv)
        q = q_ref[0, :, ks_].astype(F32) * scale
        k = k_ref[0, :, ks_].astype(F32)
        v = v_ref[0, :, vs_]

        z = _dot(a_lr, wa_ref[:, ks_]) + ba_ref[:, ks_]
        log_a = (jnp.minimum(z, 0.0) - jnp.log(1.0 + jnp.exp(-jnp.abs(z)))) * (LOG2_E / GLA_TAU)
        a_hi = log_a.astype(BF16)
        rem = log_a - a_hi.astype(F32)
        a_mid = rem.astype(BF16)
        a_lo = (rem - a_mid.astype(F32)).astype(BF16)
        b = _dot(tri, a_hi) + _dot(tri, a_mid) + _dot(tri, a_lo)
        b_ref[h] = b
        kf_ref[h] = k

        st = st_ref[h]
        o = _dot_nt((q * jnp.exp2(b)).astype(BF16), st.astype(BF16))

        scores = jnp.zeros((C, C), F32)
        w = C // 2
        while w >= diag:
            blk = 2 * w
            beta = _block_rows(b_ref.at[h], range(w, C, blk), blk)
            right = (rix % blk) >= w
            m = (jnp.where(right, q, k) * jnp.exp2(-jnp.abs(b - beta))).astype(BF16)
            keep = ((row // blk) == (col // blk)) & ((row % blk) >= w) & ((col % blk) < w)
            scores = scores + jnp.where(keep, _dot_nt(m, m), 0.0)
            w //= 2

        dsum = jnp.zeros((C, C), F32)
        for s in range(diag):
            rows = range(s, C, diag)
            bs = _block_rows(b_ref.at[h], rows, diag)
            ks = _block_rows(kf_ref.at[h], rows, diag)
            e = jnp.exp2(jnp.where((rix % diag) >= s, b - bs, -jnp.inf))
            p = (q * ks * e).astype(BF16)
            sel = ((dcol % diag) == s).astype(BF16)
            dsum = dsum + _dot(p, sel)
        scores = scores + jnp.where((row // diag) == (col // diag), dsum, 0.0)

        o = o + _dot(scores.astype(BF16), v)

        b_last = b[C - 1:C, :]
        k_dec = (k * jnp.exp2(b_last - b)).astype(BF16)
        st_ref[h] = st * jnp.exp2(b_last) + _dot_tn(v, k_dec)

        r = r_ref[0, :, vs_].astype(F32)
        o_ref[0, :, vs_] = (_rms(o, gn_ref[...]) * (r * jax.nn.sigmoid(r))).astype(o_ref.dtype)


def gla_core(proj, w_a2p, layer, b_a, g_norm, *, dk, dv):
    B, S, _ = proj.shape
    H = GLA_HEADS
    hk, hv = dk // H, dv // H
    C = GLA_CHUNK
    HS = GLA_HEADS_PER_STEP
    assert S % C == 0 and H % HS == 0
    wk, wv = HS * hk, HS * hv
    k_blk0 = dk // wk
    v_blk0 = 2 * dk // wv
    r_blk0 = (2 * dk + dv) // wv
    a_blk = (2 * dk + 2 * dv) // GLA_RANK_PAD
    kern = functools.partial(_gla_kernel, chunk=C, diag=GLA_DIAG, scale=hk ** -0.5,
                             hk=hk, hv=hv, heads=HS)
    return pl.pallas_call(
        kern,
        grid=(B, H // HS, S // C),
        in_specs=[
            pl.BlockSpec((1, C, wk), lambda b, h, c: (b, c, h)),
            pl.BlockSpec((1, C, wk), lambda b, h, c: (b, c, k_blk0 + h)),
            pl.BlockSpec((1, C, wv), lambda b, h, c: (b, c, v_blk0 + h)),
            pl.BlockSpec((1, C, wv), lambda b, h, c: (b, c, r_blk0 + h)),
            pl.BlockSpec((1, C, GLA_RANK_PAD), lambda b, h, c: (b, c, a_blk)),
            pl.BlockSpec((None, GLA_RANK_PAD, wk), lambda b, h, c: (layer, 0, h)),
            pl.BlockSpec((1, wk), lambda b, h, c: (0, h)),
            pl.BlockSpec((1, hv), lambda b, h, c: (0, 0)),
        ],
        out_specs=pl.BlockSpec((1, C, wv), lambda b, h, c: (b, c, h)),
        out_shape=jax.ShapeDtypeStruct((B, S, dv), BF16),
        scratch_shapes=[pltpu.VMEM((HS, hv, hk), F32), pltpu.VMEM((HS, C, hk), F32),
                        pltpu.VMEM((HS, C, hk), F32)],
        compiler_params=_cparams("parallel", "parallel", "arbitrary"),
        name="gla_core",
    )(proj, proj, proj, proj, proj, w_a2p, b_a.reshape(1, dk), g_norm.reshape(1, hv))


def gla_weights(w_in, w_a2, w_out):
    dk, dv = w_a2.shape[2], w_out.shape[1]
    n_pad = 2 * dk + 2 * dv + GLA_IN_PAD
    w_in_p = jnp.pad(w_in, ((0, 0), (0, 0), (0, n_pad - w_in.shape[2]))).astype(BF16)
    w_a2p = jnp.pad(w_a2, ((0, 0), (0, GLA_RANK_PAD - w_a2.shape[1]), (0, 0))).astype(BF16)
    return w_in_p, w_a2p, w_out.astype(BF16)


def gla_mixer(x, g, weights, layer, b_a, g_norm):
    w_in_p, w_a2p, w_out = weights
    B, S, D = x.shape
    dk, dv = w_a2p.shape[2], w_out.shape[1]
    n_pad = w_in_p.shape[2]
    x2 = x.reshape(B * S, D)
    proj = norm_matmul(x2, g, w_in_p, layer, bn=n_pad // 5)
    o = gla_core(proj.reshape(B, S, n_pad), w_a2p, layer, b_a, g_norm, dk=dk, dv=dv)
    return matmul_residual(o.reshape(B * S, dv), w_out, layer, x2).reshape(B, S, D)


def _pool_kernel(x_ref, g_ref, w_ref, sc_ref, o_ref, hb_ref, *, bm, gw):
    si = pl.program_id(1)
    HL = POOL_HALO

    @pl.when(si == 0)
    def _():
        hb_ref[0:HL, :] = jnp.zeros((HL, hb_ref.shape[1]), F32)

    @pl.when(si > 0)
    def _():
        hb_ref[0:HL, :] = hb_ref[bm:bm + HL, :]

    x = x_ref[0]
    hb_ref[HL:HL + bm, :] = _rms(x, g_ref[...])

    t = si * bm + lax.broadcasted_iota(jnp.int32, (bm, 1), 0)
    for gi, win in enumerate(POOL_WINDOWS):
        cs = slice(gi * gw, (gi + 1) * gw)
        h = hb_ref[HL:HL + bm, cs]
        acc = h
        for d in range(1, win):
            acc = acc + hb_ref[HL - d:HL - d + bm, cs]
        count = jnp.minimum(t + 1, win).astype(F32)
        y = (acc / count - h).astype(BF16)
        o_ref[0, :, cs] = x[:, cs] + _dot(y, w_ref[gi]) * sc_ref[:, cs]


def pool_mixer(x, g, w_pool, scale, bm=256):
    B, S, D = x.shape
    G, gw, _ = w_pool.shape
    assert S % bm == 0 and G == len(POOL_WINDOWS)
    kern = functools.partial(_pool_kernel, bm=bm, gw=gw)
    return pl.pallas_call(
        kern,
        grid=(B, S // bm),
        in_specs=[
            pl.BlockSpec((1, bm, D), lambda b, s: (b, s, 0)),
            pl.BlockSpec((1, D), lambda b, s: (0, 0)),
            pl.BlockSpec((G, gw, gw), lambda b, s: (0, 0, 0)),
            pl.BlockSpec((1, D), lambda b, s: (0, 0)),
        ],
        out_specs=pl.BlockSpec((1, bm, D), lambda b, s: (b, s, 0)),
        out_shape=jax.ShapeDtypeStruct((B, S, D), F32),
        scratch_shapes=[pltpu.VMEM((POOL_HALO + bm, D), F32)],
        compiler_params=_cparams("parallel", "arbitrary"),
        name="pool_mixer",
    )(x, g.reshape(1, D), w_pool.astype(BF16), scale.reshape(1, D))


def _rel_bucket(rel):
    n = jnp.maximum(rel, 0)
    max_exact = REL_BUCKETS // 2
    nf = jnp.maximum(n, 1).astype(F32)
    large = max_exact + (jnp.log(nf / max_exact) / math.log(REL_MAX_DIST / max_exact)
                         * (REL_BUCKETS - max_exact)).astype(jnp.int32)
    large = jnp.minimum(large, REL_BUCKETS - 1)
    return jnp.where(n < max_exact, n, large)


def _diff_attn_kernel(q_ref, k_ref, v_ref, rb_ref, lam_ref, sg_ref, o_ref,
                      m_ref, l_ref, acc_ref, bias_ref, s_ref, p_ref, a_ref, *, blk, hd, lam_init):
    qi = pl.program_id(2)
    L = 2 * blk
    lanes = m_ref.shape[-1]
    units = m_ref.shape[0]
    vd = 2 * hd

    @pl.when(qi == 0)
    def _():
        for u in range(units):
            for off in range(2):
                r = jnp.broadcast_to(rb_ref[u, off:off + 1, :], (blk, L))
                bias_ref[u, off] = pltpu.roll(r, L - blk + 1, 1, stride=1, stride_axis=0)[:, :blk]

    m_ref[...] = jnp.full(m_ref.shape, -jnp.inf, F32)
    l_ref[...] = jnp.zeros(l_ref.shape, F32)
    acc_ref[...] = jnp.zeros(acc_ref.shape, F32)

    def logits(j, u):
        start = pl.multiple_of(j * blk, blk)
        cols = slice(u * hd, (u + 1) * hd)
        s_ref[u] = _dot_nt(q_ref[0, :, cols], k_ref[0, pl.ds(start, blk), cols])

    def step(j, off, has_next=True):
        start = pl.multiple_of(j * blk, blk)
        for u in range(units):
            for r0 in range(0, blk, DIFF_STRIP):
                rows = slice(r0, r0 + DIFF_STRIP)
                s = s_ref[u, rows, :]
                if off is not None:
                    s = s + bias_ref[u, off, rows, :]
                m_old = m_ref[u, rows, :]
                m_new = jnp.maximum(m_old, jnp.max(s, axis=-1, keepdims=True))
                alpha = jnp.exp2(m_old - m_new)
                p = jnp.exp2(s - jnp.concatenate([m_new] * (blk // lanes), axis=1))
                psum = p[:, :lanes]
                for t in range(1, blk // lanes):
                    psum = psum + p[:, t * lanes:(t + 1) * lanes]
                l_ref[u, rows, :] = alpha * l_ref[u, rows, :] + psum
                p_ref[u, rows, :] = p.astype(BF16)
                a_ref[u, rows, :] = alpha
                m_ref[u, rows, :] = m_new
            if has_next:
                logits(j + 1, u)
            alpha = jnp.concatenate([a_ref[u]] * (vd // lanes), axis=1)
            vb = v_ref[0, pl.ds(start, blk), (u // 2) * vd:(u // 2 + 1) * vd]
            acc_ref[u] = alpha * acc_ref[u] + _dot(p_ref[u], vb)

    def far_body(j, carry):
        step(j, None)
        return carry

    for u in range(units):
        logits(0, u)
    lax.fori_loop(0, jnp.maximum(qi - 1, 0), far_body, 0)

    @pl.when(qi >= 1)
    def _():
        step(qi - 1, 1)

    step(qi, 0, has_next=False)

    lp = lam_ref[...]
    lam = (jnp.exp(jnp.sum(lp[0:1] * lp[1:2], axis=-1, keepdims=True))
           - jnp.exp(jnp.sum(lp[2:3] * lp[3:4], axis=-1, keepdims=True)) + lam_init)
    for h in range(units // 2):
        l0 = jnp.sum(l_ref[2 * h], axis=-1, keepdims=True)
        l1 = jnp.sum(l_ref[2 * h + 1], axis=-1, keepdims=True)
        o = acc_ref[2 * h] / l0 - lam * (acc_ref[2 * h + 1] / l1)
        o_ref[0, :, h * vd:(h + 1) * vd] = (_rms(o, sg_ref[...]) * (1.0 - lam_init)).astype(o_ref.dtype)


def diff_attention(proj, rel_vecs, lam_params, sub_gain, *, d_model, lam_init, blk):
    B, S, _ = proj.shape
    H = DIFF_HEADS
    HS = DIFF_HEADS_PER_STEP
    vd = d_model // H
    hd = vd // 2
    lanes = 128
    units = 2 * HS
    wd = HS * vd
    HG = H // HS
    assert S % blk == 0 and blk % lanes == 0 and H % HS == 0
    kern = functools.partial(_diff_attn_kernel, blk=blk, hd=hd, lam_init=lam_init)
    return pl.pallas_call(
        kern,
        grid=(B, HG, S // blk),
        in_specs=[
            pl.BlockSpec((1, blk, wd), lambda b, h, i: (b, i, h)),
            pl.BlockSpec((1, S, wd), lambda b, h, i: (b, 0, HG + h)),
            pl.BlockSpec((1, S, wd), lambda b, h, i: (b, 0, 2 * HG + h)),
            pl.BlockSpec((units, 2, 2 * blk), lambda b, h, i: (h, 0, 0)),
            pl.BlockSpec((4, hd), lambda b, h, i: (0, 0)),
            pl.BlockSpec((1, vd), lambda b, h, i: (0, 0)),
        ],
        out_specs=pl.BlockSpec((1, blk, wd), lambda b, h, i: (b, i, h)),
        out_shape=jax.ShapeDtypeStruct((B, S, d_model), BF16),
        scratch_shapes=[
            pltpu.VMEM((units, blk, lanes), F32),
            pltpu.VMEM((units, blk, lanes), F32),
            pltpu.VMEM((units, blk, vd), F32),
            pltpu.VMEM((units, 2, blk, blk), F32),
            pltpu.VMEM((units, blk, blk), F32),
            pltpu.VMEM((units, blk, blk), BF16),
            pltpu.VMEM((units, blk, lanes), F32),
        ],
        compiler_params=_cparams("parallel", "parallel", "arbitrary"),
        name="diff_attention",
    )(proj, proj, proj, rel_vecs, lam_params, sub_gain.reshape(1, vd))


def _rel_bias_vectors(rel_table, blk):
    H2 = rel_table.shape[1]
    far = rel_table[REL_BUCKETS - 1]
    y = jnp.arange(2 * blk)
    vecs = []
    for off in (0, 1):
        rel = off * blk + blk - 1 - y
        v = rel_table[_rel_bucket(rel)].astype(F32) - far
        vecs.append(jnp.where((rel >= 0)[:, None], v * LOG2_E, -jnp.inf).T)
    return jnp.stack(vecs, axis=1)


def diff_attn_mixer(x, g, w_in, w_out, layer, q_gain, k_gain, lam_params, sub_gain, rel_table,
                    layer_idx, blk=512):
    B, S, D = x.shape
    assert blk >= REL_MAX_DIST
    hd = D // DIFF_HEADS // 2
    lam_init = 0.8 - 0.6 * math.exp(-0.3 * layer_idx)
    n_heads2 = D // hd
    head_gain = jnp.concatenate([jnp.tile(q_gain, n_heads2) * (hd ** -0.5 * LOG2_E),
                                 jnp.tile(k_gain, n_heads2), jnp.ones((D,), F32)])
    x2 = x.reshape(B * S, D)
    proj = norm_matmul(x2, g, w_in, layer, head_gain=head_gain, head_norm_cols=2 * D, head_dim=hd)
    o = diff_attention(proj.reshape(B, S, 3 * D), _rel_bias_vectors(rel_table, blk), lam_params,
                       sub_gain, d_model=D, lam_init=lam_init, blk=blk)
    return matmul_residual(o.reshape(B * S, D), w_out, layer, x2).reshape(B, S, D)


def kernel(x, norm_g, gla_w_in, gla_w_a2, gla_b_a, gla_g_norm, gla_w_out, pool_w, pool_scale,
           diff_w_in, diff_q_gain, diff_k_gain, diff_lambda, diff_sub_gain, diff_w_out, rel_bias,
           ffn_w_gu, ffn_w_down):
    B, S, D = x.shape
    depth = norm_g.shape[0]
    gla_w = gla_weights(gla_w_in, gla_w_a2, gla_w_out)
    diff_w_in_b, diff_w_out_b = diff_w_in.astype(BF16), diff_w_out.astype(BF16)
    ffn_w_gu_b, ffn_w_down_b = ffn_w_gu.astype(BF16), ffn_w_down.astype(BF16)
    for i in range(depth):
        kind, slot = i % N_MIXERS, i // N_MIXERS
        if kind == 0:
            x = gla_mixer(x, norm_g[i, 0], gla_w, slot, gla_b_a[slot], gla_g_norm[slot])
        elif kind == 1:
            x = pool_mixer(x, norm_g[i, 0], pool_w[slot], pool_scale[slot])
        else:
            x = diff_attn_mixer(x, norm_g[i, 0], diff_w_in_b, diff_w_out_b, slot, diff_q_gain[slot],
                                diff_k_gain[slot], diff_lambda[slot], diff_sub_gain[slot],
                                rel_bias, i)
        x = ffn(x.reshape(B * S, D), norm_g[i, 1], ffn_w_gu_b, ffn_w_down_b, i).reshape(B, S, D)
    return x
```

```python
import functools
import math

import jax
import jax.numpy as jnp
from jax import lax
from jax.experimental import pallas as pl
from jax.experimental.pallas import tpu as pltpu

F32 = jnp.float32
BF16 = jnp.bfloat16

EPS = 1e-6
N_MIXERS = 3

GLA_HEADS = 4
GLA_RANK = 16
GLA_TAU = 16.0
GLA_CHUNK = 128
GLA_DIAG = 8
LOG2_E = math.log2(math.e)
GLA_RANK_PAD = 128
GLA_HEADS_PER_STEP = 4
GLA_IN_PAD = 256

POOL_WINDOWS = (2, 4, 8, 16)
POOL_HALO = 16

DIFF_HEADS = 8
DIFF_STRIP = 32
DIFF_HEADS_PER_STEP = 2
REL_BUCKETS = 32
REL_MAX_DIST = 128

VMEM_LIMIT_BYTES = 56 * 1024 * 1024


def _cparams(*sem):
    return pltpu.CompilerParams(dimension_semantics=sem, vmem_limit_bytes=VMEM_LIMIT_BYTES)


def _rms(x, g):
    ms = jnp.mean(x * x, axis=-1, keepdims=True)
    return x * lax.rsqrt(ms + EPS) * g


def _dot(a, b):
    return jnp.dot(a, b, preferred_element_type=F32)


def _dot_nt(a, b):
    return lax.dot_general(a, b, (((1,), (1,)), ((), ())), preferred_element_type=F32)


def _dot_tn(a, b):
    return lax.dot_general(a, b, (((0,), (0,)), ((), ())), preferred_element_type=F32)


def _norm_matmul_kernel(x_ref, g_ref, w_ref, hg_ref, o_ref, h_ref, *, head_norm_blocks, head_dim):
    j = pl.program_id(1)

    def column_block(first, head_norm):
        if first:
            h_ref[...] = _rms(x_ref[...], g_ref[...]).astype(BF16)
        acc = _dot(h_ref[...], w_ref[...])
        if head_norm:
            for c in range(acc.shape[1] // head_dim):
                sl = slice(c * head_dim, (c + 1) * head_dim)
                o_ref[:, sl] = _rms(acc[:, sl], hg_ref[:, sl]).astype(o_ref.dtype)
        else:
            o_ref[...] = acc.astype(o_ref.dtype)

    pl.when(j == 0)(functools.partial(column_block, True, head_norm_blocks > 0))
    if head_norm_blocks > 1:
        pl.when((j > 0) & (j < head_norm_blocks))(functools.partial(column_block, False, True))
    pl.when(j >= max(head_norm_blocks, 1))(functools.partial(column_block, False, False))


def norm_matmul(x, g, w, layer, head_gain=None, head_norm_cols=0, head_dim=128, bm=1024, bn=1024):
    M, D = x.shape
    N = w.shape[2]
    bm = min(bm, M)
    assert M % bm == 0 and N % bn == 0 and head_norm_cols % bn == 0
    if head_gain is None:
        head_gain = jnp.ones((N,), F32)
    kern = functools.partial(_norm_matmul_kernel, head_norm_blocks=head_norm_cols // bn,
                             head_dim=head_dim)
    return pl.pallas_call(
        kern,
        grid=(M // bm, N // bn),
        in_specs=[
            pl.BlockSpec((bm, D), lambda i, j: (i, 0)),
            pl.BlockSpec((1, D), lambda i, j: (0, 0)),
            pl.BlockSpec((None, D, bn), lambda i, j: (layer, 0, j)),
            pl.BlockSpec((1, bn), lambda i, j: (0, j)),
        ],
        out_specs=pl.BlockSpec((bm, bn), lambda i, j: (i, j)),
        out_shape=jax.ShapeDtypeStruct((M, N), BF16),
        scratch_shapes=[pltpu.VMEM((bm, D), BF16)],
        compiler_params=_cparams("parallel", "arbitrary"),
        name="norm_matmul",
    )(x, g.reshape(1, D), w, head_gain.reshape(1, N))


def _matmul_residual_kernel(a_ref, w_ref, x_ref, o_ref):
    o_ref[...] = x_ref[...] + _dot(a_ref[...], w_ref[...])


def matmul_residual(a, w, layer, x, bm=1024, bn=1024):
    M, K = a.shape
    N = w.shape[2]
    bm = min(bm, M)
    assert M % bm == 0 and N % bn == 0
    return pl.pallas_call(
        _matmul_residual_kernel,
        grid=(M // bm, N // bn),
        in_specs=[
            pl.BlockSpec((bm, K), lambda i, j: (i, 0)),
            pl.BlockSpec((None, K, bn), lambda i, j: (layer, 0, j)),
            pl.BlockSpec((bm, bn), lambda i, j: (i, j)),
        ],
        out_specs=pl.BlockSpec((bm, bn), lambda i, j: (i, j)),
        out_shape=jax.ShapeDtypeStruct((M, N), F32),
        compiler_params=_cparams("parallel", "arbitrary"),
        name="matmul_residual",
    )(a, w, x)


def _ffn_kernel(x_ref, g_ref, wg_ref, wu_ref, wd_ref, o_ref, h_ref):
    j = pl.program_id(1)

    def hidden_chunk(first):
        if first:
            h_ref[...] = _rms(x_ref[...], g_ref[...]).astype(BF16)
        h = h_ref[...]
        gate = _dot(h, wg_ref[...])
        up = _dot(h, wu_ref[...])
        act = (gate * jax.nn.sigmoid(gate) * up).astype(BF16)
        down = _dot(act, wd_ref[...])
        if first:
            o_ref[...] = x_ref[...] + down
        else:
            o_ref[...] += down

    pl.when(j == 0)(functools.partial(hidden_chunk, True))
    pl.when(j > 0)(functools.partial(hidden_chunk, False))


def ffn(x, g, w_gu, w_down, layer, bm=512, bh=512):
    M, D = x.shape
    FH = w_down.shape[1]
    bm = min(bm, M)
    assert M % bm == 0 and FH % bh == 0
    nh = FH // bh
    return pl.pallas_call(
        _ffn_kernel,
        grid=(M // bm, nh),
        in_specs=[
            pl.BlockSpec((bm, D), lambda i, j: (i, 0)),
            pl.BlockSpec((1, D), lambda i, j: (0, 0)),
            pl.BlockSpec((None, D, bh), lambda i, j: (layer, 0, j)),
            pl.BlockSpec((None, D, bh), lambda i, j: (layer, 0, j + nh)),
            pl.BlockSpec((None, bh, D), lambda i, j: (layer, j, 0)),
        ],
        out_specs=pl.BlockSpec((bm, D), lambda i, j: (i, 0)),
        out_shape=jax.ShapeDtypeStruct((M, D), F32),
        scratch_shapes=[pltpu.VMEM((bm, D), BF16)],
        compiler_params=_cparams("parallel", "arbitrary"),
        name="ffn",
    )(x, g.reshape(1, D), w_gu, w_gu, w_down)


def _block_rows(ref, rows, span):
    return jnp.concatenate(
        [jnp.broadcast_to(ref[r:r + 1, :], (span, ref.shape[1])) for r in rows], axis=0)


def _gla_kernel(q_ref, k_ref, v_ref, r_ref, a_ref, wa_ref, ba_ref, gn_ref, o_ref, st_ref,
                b_ref, kf_ref, *, chunk, diag, scale, hk, hv, heads):
    C = chunk

    @pl.when(pl.program_id(2) == 0)
    def _():
        st_ref[...] = jnp.zeros_like(st_ref)

    row = lax.broadcasted_iota(jnp.int32, (C, C), 0)
    col = lax.broadcasted_iota(jnp.int32, (C, C), 1)
    rix = lax.broadcasted_iota(jnp.int32, (C, 1), 0)
    dcol = lax.broadcasted_iota(jnp.int32, (hk, C), 1)
    tri = (row >= col).astype(BF16)
    a_lr = a_ref[0]

    for h in range(heads):
        ks_, vs_ = slice(h * hk, (h + 1) * hk), slice(h * hv, (h + 1) * hv)
        q = q_ref[0, :, ks_].astype(F32) * scale
        k = k_ref[0, :, ks_].astype(F32)
        v = v_ref[0, :, vs_]

        z = _dot(a_lr, wa_ref[:, ks_]) + ba_ref[:, ks_]
        log_a = (jnp.minimum(z, 0.0) - jnp.log(1.0 + jnp.exp(-jnp.abs(z)))) * (LOG2_E / GLA_TAU)
        a_hi = log_a.astype(BF16)
        rem = log_a - a_hi.astype(F32)
        a_mid = rem.astype(BF16)
        a_lo = (rem - a_mid.astype(F32)).astype(BF16)
        b = _dot(tri, a_hi) + _dot(tri, a_mid) + _dot(tri, a_lo)
        b_ref[h] = b
        kf_ref[h] = k

        st = st_ref[h]
        o = _dot_nt((q * jnp.exp2(b)).astype(BF16), st.astype(BF16))

        scores = jnp.zeros((C, C), F32)
        w = C // 2
        while w >= diag:
            blk = 2 * w
            beta = _block_rows(b_ref.at[h], range(w, C, blk), blk)
            right = (rix % blk) >= w
            m = (jnp.where(right, q, k) * jnp.exp2(-jnp.abs(b - beta))).astype(BF16)
            keep = ((row // blk) == (col // blk)) & ((row % blk) >= w) & ((col % blk) < w)
            scores = scores + jnp.where(keep, _dot_nt(m, m), 0.0)
            w //= 2

        dsum = jnp.zeros((C, C), F32)
        for s in range(diag):
            rows = range(s, C, diag)
            bs = _block_rows(b_ref.at[h], rows, diag)
            ks = _block_rows(kf_ref.at[h], rows, diag)
            e = jnp.exp2(jnp.where((rix % diag) >= s, b - bs, -jnp.inf))
            p = (q * ks * e).astype(BF16)
            sel = ((dcol % diag) == s).astype(BF16)
            dsum = dsum + _dot(p, sel)
        scores = scores + jnp.where((row // diag) == (col // diag), dsum, 0.0)

        o = o + _dot(scores.astype(BF16), v)

        b_last = b[C - 1:C, :]
        k_dec = (k * jnp.exp2(b_last - b)).astype(BF16)
        st_ref[h] = st * jnp.exp2(b_last) + _dot_tn(v, k_dec)

        r = r_ref[0, :, vs_].astype(F32)
        o_ref[0, :, vs_] = (_rms(o, gn_ref[...]) * (r * jax.nn.sigmoid(r))).astype(o_ref.dtype)


def gla_core(proj, w_a2p, layer, b_a, g_norm, *, dk, dv):
    B, S, _ = proj.shape
    H = GLA_HEADS
    hk, hv = dk // H, dv // H
    C = GLA_CHUNK
    HS = GLA_HEADS_PER_STEP
    assert S % C == 0 and H % HS == 0
    wk, wv = HS * hk, HS * hv
    k_blk0 = dk // wk
    v_blk0 = 2 * dk // wv
    r_blk0 = (2 * dk + dv) // wv
    a_blk = (2 * dk + 2 * dv) // GLA_RANK_PAD
    kern = functools.partial(_gla_kernel, chunk=C, diag=GLA_DIAG, scale=hk ** -0.5,
                             hk=hk, hv=hv, heads=HS)
    return pl.pallas_call(
        kern,
        grid=(B, H // HS, S // C),
        in_specs=[
            pl.BlockSpec((1, C, wk), lambda b, h, c: (b, c, h)),
            pl.BlockSpec((1, C, wk), lambda b, h, c: (b, c, k_blk0 + h)),
            pl.BlockSpec((1, C, wv), lambda b, h, c: (b, c, v_blk0 + h)),
            pl.BlockSpec((1, C, wv), lambda b, h, c: (b, c, r_blk0 + h)),
            pl.BlockSpec((1, C, GLA_RANK_PAD), lambda b, h, c: (b, c, a_blk)),
            pl.BlockSpec((None, GLA_RANK_PAD, wk), lambda b, h, c: (layer, 0, h)),
            pl.BlockSpec((1, wk), lambda b, h, c: (0, h)),
            pl.BlockSpec((1, hv), lambda b, h, c: (0, 0)),
        ],
        out_specs=pl.BlockSpec((1, C, wv), lambda b, h, c: (b, c, h)),
        out_shape=jax.ShapeDtypeStruct((B, S, dv), BF16),
        scratch_shapes=[pltpu.VMEM((HS, hv, hk), F32), pltpu.VMEM((HS, C, hk), F32),
                        pltpu.VMEM((HS, C, hk), F32)],
        compiler_params=_cparams("parallel", "parallel", "arbitrary"),
        name="gla_core",
    )(proj, proj, proj, proj, proj, w_a2p, b_a.reshape(1, dk), g_norm.reshape(1, hv))


def gla_weights(w_in, w_a2, w_out):
    dk, dv = w_a2.shape[2], w_out.shape[1]
    n_pad = 2 * dk + 2 * dv + GLA_IN_PAD
    w_in_p = jnp.pad(w_in, ((0, 0), (0, 0), (0, n_pad - w_in.shape[2]))).astype(BF16)
    w_a2p = jnp.pad(w_a2, ((0, 0), (0, GLA_RANK_PAD - w_a2.shape[1]), (0, 0))).astype(BF16)
    return w_in_p, w_a2p, w_out.astype(BF16)


def gla_mixer(x, g, weights, layer, b_a, g_norm):
    w_in_p, w_a2p, w_out = weights
    B, S, D = x.shape
    dk, dv = w_a2p.shape[2], w_out.shape[1]
    n_pad = w_in_p.shape[2]
    x2 = x.reshape(B * S, D)
    proj = norm_matmul(x2, g, w_in_p, layer, bn=n_pad // 5)
    o = gla_core(proj.reshape(B, S, n_pad), w_a2p, layer, b_a, g_norm, dk=dk, dv=dv)
    return matmul_residual(o.reshape(B * S, dv), w_out, layer, x2).reshape(B, S, D)


def _pool_kernel(x_ref, g_ref, w_ref, sc_ref, o_ref, hb_ref, *, bm, gw):
    si = pl.program_id(1)
    HL = POOL_HALO

    @pl.when(si == 0)
    def _():
        hb_ref[0:HL, :] = jnp.zeros((HL, hb_ref.shape[1]), F32)

    @pl.when(si > 0)
    def _():
        hb_ref[0:HL, :] = hb_ref[bm:bm + HL, :]

    x = x_ref[0]
    hb_ref[HL:HL + bm, :] = _rms(x, g_ref[...])

    t = si * bm + lax.broadcasted_iota(jnp.int32, (bm, 1), 0)
    for gi, win in enumerate(POOL_WINDOWS):
        cs = slice(gi * gw, (gi + 1) * gw)
        h = hb_ref[HL:HL + bm, cs]
        acc = h
        for d in range(1, win):
            acc = acc + hb_ref[HL - d:HL - d + bm, cs]
        count = jnp.minimum(t + 1, win).astype(F32)
        y = (acc / count - h).astype(BF16)
        o_ref[0, :, cs] = x[:, cs] + _dot(y, w_ref[gi]) * sc_ref[:, cs]


def pool_mixer(x, g, w_pool, scale, bm=256):
    B, S, D = x.shape
    G, gw, _ = w_pool.shape
    assert S % bm == 0 and G == len(POOL_WINDOWS)
    kern = functools.partial(_pool_kernel, bm=bm, gw=gw)
    return pl.pallas_call(
        kern,
        grid=(B, S // bm),
        in_specs=[
            pl.BlockSpec((1, bm, D), lambda b, s: (b, s, 0)),
            pl.BlockSpec((1, D), lambda b, s: (0, 0)),
            pl.BlockSpec((G, gw, gw), lambda b, s: (0, 0, 0)),
            pl.BlockSpec((1, D), lambda b, s: (0, 0)),
        ],
        out_specs=pl.BlockSpec((1, bm, D), lambda b, s: (b, s, 0)),
        out_shape=jax.ShapeDtypeStruct((B, S, D), F32),
        scratch_shapes=[pltpu.VMEM((POOL_HALO + bm, D), F32)],
        compiler_params=_cparams("parallel", "arbitrary"),
        name="pool_mixer",
    )(x, g.reshape(1, D), w_pool.astype(BF16), scale.reshape(1, D))


def _rel_bucket(rel):
    n = jnp.maximum(rel, 0)
    max_exact = REL_BUCKETS // 2
    nf = jnp.maximum(n, 1).astype(F32)
    large = max_exact + (jnp.log(nf / max_exact) / math.log(REL_MAX_DIST / max_exact)
                         * (REL_BUCKETS - max_exact)).astype(jnp.int32)
    large = jnp.minimum(large, REL_BUCKETS - 1)
    return jnp.where(n < max_exact, n, large)


def _diff_attn_kernel(q_ref, k_ref, v_ref, rb_ref, lam_ref, sg_ref, o_ref,
                      m_ref, l_ref, acc_ref, bias_ref, s_ref, p_ref, a_ref, *, blk, hd, lam_init):
    qi = pl.program_id(2)
    L = 2 * blk
    lanes = m_ref.shape[-1]
    units = m_ref.shape[0]
    vd = 2 * hd

    @pl.when((qi == 0) & (pl.program_id(1) == 0))
    def _():
        for u in range(units):
            for off in range(2):
                r = jnp.broadcast_to(rb_ref[u, off:off + 1, :], (blk, L))
                bias_ref[u, off] = pltpu.roll(r, L - blk + 1, 1, stride=1, stride_axis=0)[:, :blk]

    m_ref[...] = jnp.full(m_ref.shape, -jnp.inf, F32)
    l_ref[...] = jnp.zeros(l_ref.shape, F32)
    acc_ref[...] = jnp.zeros(acc_ref.shape, F32)

    def logits(j, u):
        start = pl.multiple_of(j * blk, blk)
        cols = slice(u * hd, (u + 1) * hd)
        s_ref[u] = _dot_nt(q_ref[0, :, cols], k_ref[0, pl.ds(start, blk), cols])

    def step(j, off, has_next=True):
        start = pl.multiple_of(j * blk, blk)
        for u in range(units):
            for r0 in range(0, blk, DIFF_STRIP):
                rows = slice(r0, r0 + DIFF_STRIP)
                s = s_ref[u, rows, :]
                if off is not None:
                    s = s + bias_ref[u, off, rows, :]
                m_old = m_ref[u, rows, :]
                m_new = jnp.maximum(m_old, jnp.max(s, axis=-1, keepdims=True))
                alpha = jnp.exp2(m_old - m_new)
                p = jnp.exp2(s - jnp.concatenate([m_new] * (blk // lanes), axis=1))
                psum = p[:, :lanes]
                for t in range(1, blk // lanes):
                    psum = psum + p[:, t * lanes:(t + 1) * lanes]
                l_ref[u, rows, :] = alpha * l_ref[u, rows, :] + psum
                p_ref[u, rows, :] = p.astype(BF16)
                a_ref[u, rows, :] = alpha
                m_ref[u, rows, :] = m_new
            if has_next:
                logits(j + 1, u)
            alpha = jnp.concatenate([a_ref[u]] * (vd // lanes), axis=1)
            vb = v_ref[0, pl.ds(start, blk), (u // 2) * vd:(u // 2 + 1) * vd]
            acc_ref[u] = alpha * acc_ref[u] + _dot(p_ref[u], vb)

    def far_body(j, carry):
        step(j, None)
        return carry

    for u in range(units):
        logits(0, u)
    lax.fori_loop(0, jnp.maximum(qi - 1, 0), far_body, 0)

    @pl.when(qi >= 1)
    def _():
        step(qi - 1, 1)

    step(qi, 0, has_next=False)

    lp = lam_ref[...]
    lam = (jnp.exp(jnp.sum(lp[0:1] * lp[1:2], axis=-1, keepdims=True))
           - jnp.exp(jnp.sum(lp[2:3] * lp[3:4], axis=-1, keepdims=True)) + lam_init)
    for h in range(units // 2):
        l0 = jnp.sum(l_ref[2 * h], axis=-1, keepdims=True)
        l1 = jnp.sum(l_ref[2 * h + 1], axis=-1, keepdims=True)
        o = acc_ref[2 * h] / l0 - lam * (acc_ref[2 * h + 1] / l1)
        o_ref[0, :, h * vd:(h + 1) * vd] = (_rms(o, sg_ref[...]) * (1.0 - lam_init)).astype(o_ref.dtype)


def diff_attention(proj, rel_vecs, lam_params, sub_gain, *, d_model, lam_init, blk):
    B, S, _ = proj.shape
    H = DIFF_HEADS
    HS = DIFF_HEADS_PER_STEP
    vd = d_model // H
    hd = vd // 2
    lanes = 128
    units = 2 * HS
    wd = HS * vd
    HG = H // HS
    assert S % blk == 0 and blk % lanes == 0 and H % HS == 0
    kern = functools.partial(_diff_attn_kernel, blk=blk, hd=hd, lam_init=lam_init)
    return pl.pallas_call(
        kern,
        grid=(HG, B, S // blk),
        in_specs=[
            pl.BlockSpec((1, blk, wd), lambda h, b, i: (b, i, h)),
            pl.BlockSpec((1, S, wd), lambda h, b, i: (b, 0, HG + h)),
            pl.BlockSpec((1, S, wd), lambda h, b, i: (b, 0, 2 * HG + h)),
            pl.BlockSpec((units, 2, 2 * blk), lambda h, b, i: (h, 0, 0)),
            pl.BlockSpec((4, hd), lambda h, b, i: (0, 0)),
            pl.BlockSpec((1, vd), lambda h, b, i: (0, 0)),
        ],
        out_specs=pl.BlockSpec((1, blk, wd), lambda h, b, i: (b, i, h)),
        out_shape=jax.ShapeDtypeStruct((B, S, d_model), BF16),
        scratch_shapes=[
            pltpu.VMEM((units, blk, lanes), F32),
            pltpu.VMEM((units, blk, lanes), F32),
            pltpu.VMEM((units, blk, vd), F32),
            pltpu.VMEM((units, 2, blk, blk), F32),
            pltpu.VMEM((units, blk, blk), F32),
            pltpu.VMEM((units, blk, blk), BF16),
            pltpu.VMEM((units, blk, lanes), F32),
        ],
        compiler_params=_cparams("arbitrary", "arbitrary", "arbitrary"),
        name="diff_attention",
    )(proj, proj, proj, rel_vecs, lam_params, sub_gain.reshape(1, vd))


def _rel_bias_vectors(rel_table, blk):
    H2 = rel_table.shape[1]
    far = rel_table[REL_BUCKETS - 1]
    y = jnp.arange(2 * blk)
    vecs = []
    for off in (0, 1):
        rel = off * blk + blk - 1 - y
        v = rel_table[_rel_bucket(rel)].astype(F32) - far
        vecs.append(jnp.where((rel >= 0)[:, None], v * LOG2_E, -jnp.inf).T)
    return jnp.stack(vecs, axis=1)


def diff_attn_mixer(x, g, w_in, w_out, layer, q_gain, k_gain, lam_params, sub_gain, rel_table,
                    layer_idx, blk=512):
    B, S, D = x.shape
    assert blk >= REL_MAX_DIST
    hd = D // DIFF_HEADS // 2
    lam_init = 0.8 - 0.6 * math.exp(-0.3 * layer_idx)
    n_heads2 = D // hd
    head_gain = jnp.concatenate([jnp.tile(q_gain, n_heads2) * (hd ** -0.5 * LOG2_E),
                                 jnp.tile(k_gain, n_heads2), jnp.ones((D,), F32)])
    x2 = x.reshape(B * S, D)
    proj = norm_matmul(x2, g, w_in, layer, head_gain=head_gain, head_norm_cols=2 * D, head_dim=hd)
    o = diff_attention(proj.reshape(B, S, 3 * D), _rel_bias_vectors(rel_table, blk), lam_params,
                       sub_gain, d_model=D, lam_init=lam_init, blk=blk)
    return matmul_residual(o.reshape(B * S, D), w_out, layer, x2).reshape(B, S, D)


def kernel(x, norm_g, gla_w_in, gla_w_a2, gla_b_a, gla_g_norm, gla_w_out, pool_w, pool_scale,
           diff_w_in, diff_q_gain, diff_k_gain, diff_lambda, diff_sub_gain, diff_w_out, rel_bias,
           ffn_w_gu, ffn_w_down):
    B, S, D = x.shape
    depth = norm_g.shape[0]
    gla_w = gla_weights(gla_w_in, gla_w_a2, gla_w_out)
    diff_w_in_b, diff_w_out_b = diff_w_in.astype(BF16), diff_w_out.astype(BF16)
    ffn_w_gu_b, ffn_w_down_b = ffn_w_gu.astype(BF16), ffn_w_down.astype(BF16)
    for i in range(depth):
        kind, slot = i % N_MIXERS, i // N_MIXERS
        if kind == 0:
            x = gla_mixer(x, norm_g[i, 0], gla_w, slot, gla_b_a[slot], gla_g_norm[slot])
        elif kind == 1:
            x = pool_mixer(x, norm_g[i, 0], pool_w[slot], pool_scale[slot])
        else:
            x = diff_attn_mixer(x, norm_g[i, 0], diff_w_in_b, diff_w_out_b, slot, diff_q_gain[slot],
                                diff_k_gain[slot], diff_lambda[slot], diff_sub_gain[slot],
                                rel_bias, i)
        x = ffn(x.reshape(B * S, D), norm_g[i, 1], ffn_w_gu_b, ffn_w_down_b, i).reshape(B, S, D)
    return x
```

```python
import functools
import math

import jax
import jax.numpy as jnp
from jax import lax
from jax.experimental import pallas as pl
from jax.experimental.pallas import tpu as pltpu

F32 = jnp.float32
BF16 = jnp.bfloat16

EPS = 1e-6
N_MIXERS = 3

GLA_HEADS = 4
GLA_RANK = 16
GLA_TAU = 16.0
GLA_CHUNK = 128
GLA_DIAG = 8
LOG2_E = math.log2(math.e)
GLA_RANK_PAD = 128
GLA_HEADS_PER_STEP = 4
GLA_IN_PAD = 256

POOL_WINDOWS = (2, 4, 8, 16)
POOL_HALO = 16

DIFF_HEADS = 8
DIFF_STRIP = 32
DIFF_HEADS_PER_STEP = 2
REL_BUCKETS = 32
REL_MAX_DIST = 128

VMEM_LIMIT_BYTES = 56 * 1024 * 1024


def _cparams(*sem):
    return pltpu.CompilerParams(dimension_semantics=sem, vmem_limit_bytes=VMEM_LIMIT_BYTES)


def _rms(x, g):
    ms = jnp.mean(x * x, axis=-1, keepdims=True)
    return x * lax.rsqrt(ms + EPS) * g


def _dot(a, b):
    return jnp.dot(a, b, preferred_element_type=F32)


def _dot_nt(a, b):
    return lax.dot_general(a, b, (((1,), (1,)), ((), ())), preferred_element_type=F32)


def _dot_tn(a, b):
    return lax.dot_general(a, b, (((0,), (0,)), ((), ())), preferred_element_type=F32)


def _norm_matmul_kernel(x_ref, g_ref, w_ref, hg_ref, o_ref, h_ref, *, head_norm_blocks, head_dim):
    j = pl.program_id(1)

    def column_block(first, head_norm):
        if first:
            h_ref[...] = _rms(x_ref[...], g_ref[...]).astype(BF16)
        acc = _dot(h_ref[...], w_ref[...])
        if head_norm:
            for c in range(acc.shape[1] // head_dim):
                sl = slice(c * head_dim, (c + 1) * head_dim)
                o_ref[:, sl] = _rms(acc[:, sl], hg_ref[:, sl]).astype(o_ref.dtype)
        else:
            o_ref[...] = acc.astype(o_ref.dtype)

    pl.when(j == 0)(functools.partial(column_block, True, head_norm_blocks > 0))
    if head_norm_blocks > 1:
        pl.when((j > 0) & (j < head_norm_blocks))(functools.partial(column_block, False, True))
    pl.when(j >= max(head_norm_blocks, 1))(functools.partial(column_block, False, False))


def norm_matmul(x, g, w, layer, head_gain=None, head_norm_cols=0, head_dim=128, bm=1024, bn=1024):
    M, D = x.shape
    N = w.shape[2]
    bm = min(bm, M)
    assert M % bm == 0 and N % bn == 0 and head_norm_cols % bn == 0
    if head_gain is None:
        head_gain = jnp.ones((N,), F32)
    kern = functools.partial(_norm_matmul_kernel, head_norm_blocks=head_norm_cols // bn,
                             head_dim=head_dim)
    return pl.pallas_call(
        kern,
        grid=(M // bm, N // bn),
        in_specs=[
            pl.BlockSpec((bm, D), lambda i, j: (i, 0)),
            pl.BlockSpec((1, D), lambda i, j: (0, 0)),
            pl.BlockSpec((None, D, bn), lambda i, j: (layer, 0, j)),
            pl.BlockSpec((1, bn), lambda i, j: (0, j)),
        ],
        out_specs=pl.BlockSpec((bm, bn), lambda i, j: (i, j)),
        out_shape=jax.ShapeDtypeStruct((M, N), BF16),
        scratch_shapes=[pltpu.VMEM((bm, D), BF16)],
        compiler_params=_cparams("parallel", "arbitrary"),
        name="norm_matmul",
    )(x, g.reshape(1, D), w, head_gain.reshape(1, N))


def _matmul_residual_kernel(a_ref, w_ref, x_ref, o_ref):
    o_ref[...] = x_ref[...] + _dot(a_ref[...], w_ref[...])


def matmul_residual(a, w, layer, x, bm=1024, bn=1024):
    M, K = a.shape
    N = w.shape[2]
    bm = min(bm, M)
    assert M % bm == 0 and N % bn == 0
    return pl.pallas_call(
        _matmul_residual_kernel,
        grid=(M // bm, N // bn),
        in_specs=[
            pl.BlockSpec((bm, K), lambda i, j: (i, 0)),
            pl.BlockSpec((None, K, bn), lambda i, j: (layer, 0, j)),
            pl.BlockSpec((bm, bn), lambda i, j: (i, j)),
        ],
        out_specs=pl.BlockSpec((bm, bn), lambda i, j: (i, j)),
        out_shape=jax.ShapeDtypeStruct((M, N), F32),
        compiler_params=_cparams("parallel", "arbitrary"),
        name="matmul_residual",
    )(a, w, x)


def _ffn_kernel(x_hbm, g_ref, wg_ref, wu_ref, wd_ref, o_ref, h_ref, xbuf, xsem):
    i, j = pl.program_id(0), pl.program_id(1)
    bm = xbuf.shape[0]

    def x_copy(row_block):
        return pltpu.make_async_copy(x_hbm.at[pl.ds(row_block * bm, bm), :], xbuf, xsem)

    @pl.when((i == 0) & (j == 0))
    def _():
        x_copy(0).start()

    def hidden_chunk(first):
        if first:
            x_copy(i).wait()
            h_ref[...] = _rms(xbuf[...], g_ref[...]).astype(BF16)
        h = h_ref[...]
        gate = _dot(h, wg_ref[...])
        up = _dot(h, wu_ref[...])
        act = (gate * jax.nn.sigmoid(gate) * up).astype(BF16)
        down = _dot(act, wd_ref[...])
        if first:
            o_ref[...] = xbuf[...] + down
        else:
            o_ref[...] += down

    pl.when(j == 0)(functools.partial(hidden_chunk, True))
    pl.when(j > 0)(functools.partial(hidden_chunk, False))

    @pl.when((j == 1) & (i + 1 < pl.num_programs(0)))
    def _():
        x_copy(i + 1).start()


def ffn(x, g, w_gu, w_down, layer, bm=1024, bh=512):
    M, D = x.shape
    FH = w_down.shape[1]
    bm = min(bm, M)
    assert M % bm == 0 and FH % bh == 0
    nh = FH // bh
    assert nh >= 2
    return pl.pallas_call(
        _ffn_kernel,
        grid=(M // bm, nh),
        in_specs=[
            pl.BlockSpec(memory_space=pl.ANY),
            pl.BlockSpec((1, D), lambda i, j: (0, 0)),
            pl.BlockSpec((None, D, bh), lambda i, j: (layer, 0, j)),
            pl.BlockSpec((None, D, bh), lambda i, j: (layer, 0, j + nh)),
            pl.BlockSpec((None, bh, D), lambda i, j: (layer, j, 0)),
        ],
        out_specs=pl.BlockSpec((bm, D), lambda i, j: (i, 0)),
        out_shape=jax.ShapeDtypeStruct((M, D), F32),
        scratch_shapes=[pltpu.VMEM((bm, D), BF16), pltpu.VMEM((bm, D), F32),
                        pltpu.SemaphoreType.DMA(())],
        compiler_params=_cparams("arbitrary", "arbitrary"),
        name="ffn",
    )(x, g.reshape(1, D), w_gu, w_gu, w_down)


def _block_rows(ref, rows, span):
    return jnp.concatenate(
        [jnp.broadcast_to(ref[r:r + 1, :], (span, ref.shape[1])) for r in rows], axis=0)


def _gla_kernel(q_ref, k_ref, v_ref, r_ref, a_ref, wa_ref, ba_ref, gn_ref, o_ref, st_ref,
                b_ref, kf_ref, *, chunk, diag, scale, hk, hv, heads):
    C = chunk

    @pl.when(pl.program_id(2) == 0)
    def _():
        st_ref[...] = jnp.zeros_like(st_ref)

    row = lax.broadcasted_iota(jnp.int32, (C, C), 0)
    col = lax.broadcasted_iota(jnp.int32, (C, C), 1)
    rix = lax.broadcasted_iota(jnp.int32, (C, 1), 0)
    dcol = lax.broadcasted_iota(jnp.int32, (hk, C), 1)
    tri = (row >= col).astype(BF16)
    a_lr = a_ref[0]

    for h in range(heads):
        ks_, vs_ = slice(h * hk, (h + 1) * hk), slice(h * hv, (h + 1) * hv)
        q = q_ref[0, :, ks_].astype(F32) * scale
        k = k_ref[0, :, ks_].astype(F32)
        v = v_ref[0, :, vs_]

        z = _dot(a_lr, wa_ref[:, ks_]) + ba_ref[:, ks_]
        log_a = (jnp.minimum(z, 0.0) - jnp.log(1.0 + jnp.exp(-jnp.abs(z)))) * (LOG2_E / GLA_TAU)
        a_hi = log_a.astype(BF16)
        rem = log_a - a_hi.astype(F32)
        a_mid = rem.astype(BF16)
        a_lo = (rem - a_mid.astype(F32)).astype(BF16)
        b = _dot(tri, a_hi) + _dot(tri, a_mid) + _dot(tri, a_lo)
        b_ref[h] = b
        kf_ref[h] = k

        st = st_ref[h]
        o = _dot_nt((q * jnp.exp2(b)).astype(BF16), st.astype(BF16))

        scores = jnp.zeros((C, C), F32)
        w = C // 2
        while w >= diag:
            blk = 2 * w
            beta = _block_rows(b_ref.at[h], range(w, C, blk), blk)
            right = (rix % blk) >= w
            m = (jnp.where(right, q, k) * jnp.exp2(-jnp.abs(b - beta))).astype(BF16)
            keep = ((row // blk) == (col // blk)) & ((row % blk) >= w) & ((col % blk) < w)
            scores = scores + jnp.where(keep, _dot_nt(m, m), 0.0)
            w //= 2

        dsum = jnp.zeros((C, C), F32)
        for s in range(diag):
            rows = range(s, C, diag)
            bs = _block_rows(b_ref.at[h], rows, diag)
            ks = _block_rows(kf_ref.at[h], rows, diag)
            e = jnp.exp2(jnp.where((rix % diag) >= s, b - bs, -jnp.inf))
            p = (q * ks * e).astype(BF16)
            sel = ((dcol % diag) == s).astype(BF16)
            dsum = dsum + _dot(p, sel)
        scores = scores + jnp.where((row // diag) == (col // diag), dsum, 0.0)

        o = o + _dot(scores.astype(BF16), v)

        b_last = b[C - 1:C, :]
        k_dec = (k * jnp.exp2(b_last - b)).astype(BF16)
        st_ref[h] = st * jnp.exp2(b_last) + _dot_tn(v, k_dec)

        r = r_ref[0, :, vs_].astype(F32)
        o_ref[0, :, vs_] = (_rms(o, gn_ref[...]) * (r * jax.nn.sigmoid(r))).astype(o_ref.dtype)


def gla_core(proj, w_a2p, layer, b_a, g_norm, *, dk, dv):
    B, S, _ = proj.shape
    H = GLA_HEADS
    hk, hv = dk // H, dv // H
    C = GLA_CHUNK
    HS = GLA_HEADS_PER_STEP
    assert S % C == 0 and H % HS == 0
    wk, wv = HS * hk, HS * hv
    k_blk0 = dk // wk
    v_blk0 = 2 * dk // wv
    r_blk0 = (2 * dk + dv) // wv
    a_blk = (2 * dk + 2 * dv) // GLA_RANK_PAD
    kern = functools.partial(_gla_kernel, chunk=C, diag=GLA_DIAG, scale=hk ** -0.5,
                             hk=hk, hv=hv, heads=HS)
    return pl.pallas_call(
        kern,
        grid=(B, H // HS, S // C),
        in_specs=[
            pl.BlockSpec((1, C, wk), lambda b, h, c: (b, c, h)),
            pl.BlockSpec((1, C, wk), lambda b, h, c: (b, c, k_blk0 + h)),
            pl.BlockSpec((1, C, wv), lambda b, h, c: (b, c, v_blk0 + h)),
            pl.BlockSpec((1, C, wv), lambda b, h, c: (b, c, r_blk0 + h)),
            pl.BlockSpec((1, C, GLA_RANK_PAD), lambda b, h, c: (b, c, a_blk)),
            pl.BlockSpec((None, GLA_RANK_PAD, wk), lambda b, h, c: (layer, 0, h)),
            pl.BlockSpec((1, wk), lambda b, h, c: (0, h)),
            pl.BlockSpec((1, hv), lambda b, h, c: (0, 0)),
        ],
        out_specs=pl.BlockSpec((1, C, wv), lambda b, h, c: (b, c, h)),
        out_shape=jax.ShapeDtypeStruct((B, S, dv), BF16),
        scratch_shapes=[pltpu.VMEM((HS, hv, hk), F32), pltpu.VMEM((HS, C, hk), F32),
                        pltpu.VMEM((HS, C, hk), F32)],
        compiler_params=_cparams("parallel", "parallel", "arbitrary"),
        name="gla_core",
    )(proj, proj, proj, proj, proj, w_a2p, b_a.reshape(1, dk), g_norm.reshape(1, hv))


def gla_weights(w_in, w_a2, w_out):
    dk, dv = w_a2.shape[2], w_out.shape[1]
    n_pad = 2 * dk + 2 * dv + GLA_IN_PAD
    w_in_p = jnp.pad(w_in, ((0, 0), (0, 0), (0, n_pad - w_in.shape[2]))).astype(BF16)
    w_a2p = jnp.pad(w_a2, ((0, 0), (0, GLA_RANK_PAD - w_a2.shape[1]), (0, 0))).astype(BF16)
    return w_in_p, w_a2p, w_out.astype(BF16)


def gla_mixer(x, g, weights, layer, b_a, g_norm):
    w_in_p, w_a2p, w_out = weights
    B, S, D = x.shape
    dk, dv = w_a2p.shape[2], w_out.shape[1]
    n_pad = w_in_p.shape[2]
    x2 = x.reshape(B * S, D)
    proj = norm_matmul(x2, g, w_in_p, layer, bn=n_pad // 5)
    o = gla_core(proj.reshape(B, S, n_pad), w_a2p, layer, b_a, g_norm, dk=dk, dv=dv)
    return matmul_residual(o.reshape(B * S, dv), w_out, layer, x2).reshape(B, S, D)


def _pool_kernel(x_ref, g_ref, w_ref, sc_ref, o_ref, hb_ref, *, bm, gw):
    si = pl.program_id(1)
    HL = POOL_HALO

    @pl.when(si == 0)
    def _():
        hb_ref[0:HL, :] = jnp.zeros((HL, hb_ref.shape[1]), F32)

    @pl.when(si > 0)
    def _():
        hb_ref[0:HL, :] = hb_ref[bm:bm + HL, :]

    x = x_ref[0]
    hb_ref[HL:HL + bm, :] = _rms(x, g_ref[...])

    t = si * bm + lax.broadcasted_iota(jnp.int32, (bm, 1), 0)
    for gi, win in enumerate(POOL_WINDOWS):
        cs = slice(gi * gw, (gi + 1) * gw)
        h = hb_ref[HL:HL + bm, cs]
        acc = h
        for d in range(1, win):
            acc = acc + hb_ref[HL - d:HL - d + bm, cs]
        count = jnp.minimum(t + 1, win).astype(F32)
        y = (acc / count - h).astype(BF16)
        o_ref[0, :, cs] = x[:, cs] + _dot(y, w_ref[gi]) * sc_ref[:, cs]


def pool_mixer(x, g, w_pool, scale, bm=256):
    B, S, D = x.shape
    G, gw, _ = w_pool.shape
    assert S % bm == 0 and G == len(POOL_WINDOWS)
    kern = functools.partial(_pool_kernel, bm=bm, gw=gw)
    return pl.pallas_call(
        kern,
        grid=(B, S // bm),
        in_specs=[
            pl.BlockSpec((1, bm, D), lambda b, s: (b, s, 0)),
            pl.BlockSpec((1, D), lambda b, s: (0, 0)),
            pl.BlockSpec((G, gw, gw), lambda b, s: (0, 0, 0)),
            pl.BlockSpec((1, D), lambda b, s: (0, 0)),
        ],
        out_specs=pl.BlockSpec((1, bm, D), lambda b, s: (b, s, 0)),
        out_shape=jax.ShapeDtypeStruct((B, S, D), F32),
        scratch_shapes=[pltpu.VMEM((POOL_HALO + bm, D), F32)],
        compiler_params=_cparams("parallel", "arbitrary"),
        name="pool_mixer",
    )(x, g.reshape(1, D), w_pool.astype(BF16), scale.reshape(1, D))


def _rel_bucket(rel):
    n = jnp.maximum(rel, 0)
    max_exact = REL_BUCKETS // 2
    nf = jnp.maximum(n, 1).astype(F32)
    large = max_exact + (jnp.log(nf / max_exact) / math.log(REL_MAX_DIST / max_exact)
                         * (REL_BUCKETS - max_exact)).astype(jnp.int32)
    large = jnp.minimum(large, REL_BUCKETS - 1)
    return jnp.where(n < max_exact, n, large)


def _diff_attn_kernel(q_ref, k_ref, v_ref, rb_ref, lam_ref, sg_ref, o_ref,
                      m_ref, l_ref, acc_ref, bias_ref, s_ref, p_ref, a_ref, *, blk, hd, lam_init):
    qi = pl.program_id(2)
    L = 2 * blk
    lanes = m_ref.shape[-1]
    units = m_ref.shape[0]
    vd = 2 * hd

    @pl.when((qi == 0) & (pl.program_id(1) == 0))
    def _():
        for u in range(units):
            for off in range(2):
                r = jnp.broadcast_to(rb_ref[u, off:off + 1, :], (blk, L))
                bias_ref[u, off] = pltpu.roll(r, L - blk + 1, 1, stride=1, stride_axis=0)[:, :blk]

    m_ref[...] = jnp.full(m_ref.shape, -jnp.inf, F32)
    l_ref[...] = jnp.zeros(l_ref.shape, F32)
    acc_ref[...] = jnp.zeros(acc_ref.shape, F32)

    def logits(j, u):
        start = pl.multiple_of(j * blk, blk)
        cols = slice(u * hd, (u + 1) * hd)
        s_ref[u] = _dot_nt(q_ref[0, :, cols], k_ref[0, pl.ds(start, blk), cols])

    def step(j, off, has_next=True):
        start = pl.multiple_of(j * blk, blk)
        for u in range(units):
            for r0 in range(0, blk, DIFF_STRIP):
                rows = slice(r0, r0 + DIFF_STRIP)
                s = s_ref[u, rows, :]
                if off is not None:
                    s = s + bias_ref[u, off, rows, :]
                m_old = m_ref[u, rows, :]
                m_new = jnp.maximum(m_old, jnp.max(s, axis=-1, keepdims=True))
                alpha = jnp.exp2(m_old - m_new)
                p = jnp.exp2(s - jnp.concatenate([m_new] * (blk // lanes), axis=1))
                psum = p[:, :lanes]
                for t in range(1, blk // lanes):
                    psum = psum + p[:, t * lanes:(t + 1) * lanes]
                l_ref[u, rows, :] = alpha * l_ref[u, rows, :] + psum
                p_ref[u, rows, :] = p.astype(BF16)
                a_ref[u, rows, :] = alpha
                m_ref[u, rows, :] = m_new
            if has_next:
                logits(j + 1, u)
            alpha = jnp.concatenate([a_ref[u]] * (vd // lanes), axis=1)
            vb = v_ref[0, pl.ds(start, blk), (u // 2) * vd:(u // 2 + 1) * vd]
            acc_ref[u] = alpha * acc_ref[u] + _dot(p_ref[u], vb)

    def far_body(j, carry):
        step(j, None)
        return carry

    for u in range(units):
        logits(0, u)
    lax.fori_loop(0, jnp.maximum(qi - 1, 0), far_body, 0)

    @pl.when(qi >= 1)
    def _():
        step(qi - 1, 1)

    step(qi, 0, has_next=False)

    lp = lam_ref[...]
    lam = (jnp.exp(jnp.sum(lp[0:1] * lp[1:2], axis=-1, keepdims=True))
           - jnp.exp(jnp.sum(lp[2:3] * lp[3:4], axis=-1, keepdims=True)) + lam_init)
    for h in range(units // 2):
        l0 = jnp.sum(l_ref[2 * h], axis=-1, keepdims=True)
        l1 = jnp.sum(l_ref[2 * h + 1], axis=-1, keepdims=True)
        o = acc_ref[2 * h] / l0 - lam * (acc_ref[2 * h + 1] / l1)
        o_ref[0, :, h * vd:(h + 1) * vd] = (_rms(o, sg_ref[...]) * (1.0 - lam_init)).astype(o_ref.dtype)


def diff_attention(proj, rel_vecs, lam_params, sub_gain, *, d_model, lam_init, blk):
    B, S, _ = proj.shape
    H = DIFF_HEADS
    HS = DIFF_HEADS_PER_STEP
    vd = d_model // H
    hd = vd // 2
    lanes = 128
    units = 2 * HS
    wd = HS * vd
    HG = H // HS
    assert S % blk == 0 and blk % lanes == 0 and H % HS == 0
    kern = functools.partial(_diff_attn_kernel, blk=blk, hd=hd, lam_init=lam_init)
    return pl.pallas_call(
        kern,
        grid=(HG, B, S // blk),
        in_specs=[
            pl.BlockSpec((1, blk, wd), lambda h, b, i: (b, i, h)),
            pl.BlockSpec((1, S, wd), lambda h, b, i: (b, 0, HG + h)),
            pl.BlockSpec((1, S, wd), lambda h, b, i: (b, 0, 2 * HG + h)),
            pl.BlockSpec((units, 2, 2 * blk), lambda h, b, i: (h, 0, 0)),
            pl.BlockSpec((4, hd), lambda h, b, i: (0, 0)),
            pl.BlockSpec((1, vd), lambda h, b, i: (0, 0)),
        ],
        out_specs=pl.BlockSpec((1, blk, wd), lambda h, b, i: (b, i, h)),
        out_shape=jax.ShapeDtypeStruct((B, S, d_model), BF16),
        scratch_shapes=[
            pltpu.VMEM((units, blk, lanes), F32),
            pltpu.VMEM((units, blk, lanes), F32),
            pltpu.VMEM((units, blk, vd), F32),
            pltpu.VMEM((units, 2, blk, blk), F32),
            pltpu.VMEM((units, blk, blk), F32),
            pltpu.VMEM((units, blk, blk), BF16),
            pltpu.VMEM((units, blk, lanes), F32),
        ],
        compiler_params=_cparams("arbitrary", "arbitrary", "arbitrary"),
        name="diff_attention",
    )(proj, proj, proj, rel_vecs, lam_params, sub_gain.reshape(1, vd))


def _rel_bias_vectors(rel_table, blk):
    H2 = rel_table.shape[1]
    far = rel_table[REL_BUCKETS - 1]
    y = jnp.arange(2 * blk)
    vecs = []
    for off in (0, 1):
        rel = off * blk + blk - 1 - y
        v = rel_table[_rel_bucket(rel)].astype(F32) - far
        vecs.append(jnp.where((rel >= 0)[:, None], v * LOG2_E, -jnp.inf).T)
    return jnp.stack(vecs, axis=1)


def diff_attn_mixer(x, g, w_in, w_out, layer, q_gain, k_gain, lam_params, sub_gain, rel_table,
                    layer_idx, blk=512):
    B, S, D = x.shape
    assert blk >= REL_MAX_DIST
    hd = D // DIFF_HEADS // 2
    lam_init = 0.8 - 0.6 * math.exp(-0.3 * layer_idx)
    n_heads2 = D // hd
    head_gain = jnp.concatenate([jnp.tile(q_gain, n_heads2) * (hd ** -0.5 * LOG2_E),
                                 jnp.tile(k_gain, n_heads2), jnp.ones((D,), F32)])
    x2 = x.reshape(B * S, D)
    proj = norm_matmul(x2, g, w_in, layer, head_gain=head_gain, head_norm_cols=2 * D, head_dim=hd)
    o = diff_attention(proj.reshape(B, S, 3 * D), _rel_bias_vectors(rel_table, blk), lam_params,
                       sub_gain, d_model=D, lam_init=lam_init, blk=blk)
    return matmul_residual(o.reshape(B * S, D), w_out, layer, x2).reshape(B, S, D)


def kernel(x, norm_g, gla_w_in, gla_w_a2, gla_b_a, gla_g_norm, gla_w_out, pool_w, pool_scale,
           diff_w_in, diff_q_gain, diff_k_gain, diff_lambda, diff_sub_gain, diff_w_out, rel_bias,
           ffn_w_gu, ffn_w_down):
    B, S, D = x.shape
    depth = norm_g.shape[0]
    gla_w = gla_weights(gla_w_in, gla_w_a2, gla_w_out)
    diff_w_in_b, diff_w_out_b = diff_w_in.astype(BF16), diff_w_out.astype(BF16)
    ffn_w_gu_b, ffn_w_down_b = ffn_w_gu.astype(BF16), ffn_w_down.astype(BF16)
    for i in range(depth):
        kind, slot = i % N_MIXERS, i // N_MIXERS
        if kind == 0:
            x = gla_mixer(x, norm_g[i, 0], gla_w, slot, gla_b_a[slot], gla_g_norm[slot])
        elif kind == 1:
            x = pool_mixer(x, norm_g[i, 0], pool_w[slot], pool_scale[slot])
        else:
            x = diff_attn_mixer(x, norm_g[i, 0], diff_w_in_b, diff_w_out_b, slot, diff_q_gain[slot],
                                diff_k_gain[slot], diff_lambda[slot], diff_sub_gain[slot],
                                rel_bias, i)
        x = ffn(x.reshape(B * S, D), norm_g[i, 1], ffn_w_gu_b, ffn_w_down_b, i).reshape(B, S, D)
    return x
```

```python
import functools
import math

import jax
import jax.numpy as jnp
from jax import lax
from jax.experimental import pallas as pl
from jax.experimental.pallas import tpu as pltpu

F32 = jnp.float32
BF16 = jnp.bfloat16

EPS = 1e-6
N_MIXERS = 3

GLA_HEADS = 4
GLA_RANK = 16
GLA_TAU = 16.0
GLA_CHUNK = 128
GLA_DIAG = 8
LOG2_E = math.log2(math.e)
GLA_RANK_PAD = 128
GLA_HEADS_PER_STEP = 4
GLA_BATCH_PER_STEP = 2
GLA_IN_PAD = 256

POOL_WINDOWS = (2, 4, 8, 16)
POOL_HALO = 16

DIFF_HEADS = 8
DIFF_STRIP = 32
DIFF_HEADS_PER_STEP = 2
REL_BUCKETS = 32
REL_MAX_DIST = 128

VMEM_LIMIT_BYTES = 56 * 1024 * 1024


def _cparams(*sem):
    return pltpu.CompilerParams(dimension_semantics=sem, vmem_limit_bytes=VMEM_LIMIT_BYTES)


def _rms(x, g):
    ms = jnp.mean(x * x, axis=-1, keepdims=True)
    return x * lax.rsqrt(ms + EPS) * g


def _dot(a, b):
    return jnp.dot(a, b, preferred_element_type=F32)


def _dot_nt(a, b):
    return lax.dot_general(a, b, (((1,), (1,)), ((), ())), preferred_element_type=F32)


def _dot_tn(a, b):
    return lax.dot_general(a, b, (((0,), (0,)), ((), ())), preferred_element_type=F32)


def _norm_matmul_kernel(x_ref, g_ref, w_ref, hg_ref, o_ref, h_ref, *, head_norm_blocks, head_dim):
    j = pl.program_id(1)

    def column_block(first, head_norm):
        if first:
            h_ref[...] = _rms(x_ref[...], g_ref[...]).astype(BF16)
        acc = _dot(h_ref[...], w_ref[...])
        if head_norm:
            for c in range(acc.shape[1] // head_dim):
                sl = slice(c * head_dim, (c + 1) * head_dim)
                o_ref[:, sl] = _rms(acc[:, sl], hg_ref[:, sl]).astype(o_ref.dtype)
        else:
            o_ref[...] = acc.astype(o_ref.dtype)

    pl.when(j == 0)(functools.partial(column_block, True, head_norm_blocks > 0))
    if head_norm_blocks > 1:
        pl.when((j > 0) & (j < head_norm_blocks))(functools.partial(column_block, False, True))
    pl.when(j >= max(head_norm_blocks, 1))(functools.partial(column_block, False, False))


def norm_matmul(x, g, w, layer, head_gain=None, head_norm_cols=0, head_dim=128, bm=1024, bn=1024):
    M, D = x.shape
    N = w.shape[2]
    bm = min(bm, M)
    assert M % bm == 0 and N % bn == 0 and head_norm_cols % bn == 0
    if head_gain is None:
        head_gain = jnp.ones((N,), F32)
    kern = functools.partial(_norm_matmul_kernel, head_norm_blocks=head_norm_cols // bn,
                             head_dim=head_dim)
    return pl.pallas_call(
        kern,
        grid=(M // bm, N // bn),
        in_specs=[
            pl.BlockSpec((bm, D), lambda i, j: (i, 0)),
            pl.BlockSpec((1, D), lambda i, j: (0, 0)),
            pl.BlockSpec((None, D, bn), lambda i, j: (layer, 0, j)),
            pl.BlockSpec((1, bn), lambda i, j: (0, j)),
        ],
        out_specs=pl.BlockSpec((bm, bn), lambda i, j: (i, j)),
        out_shape=jax.ShapeDtypeStruct((M, N), BF16),
        scratch_shapes=[pltpu.VMEM((bm, D), BF16)],
        compiler_params=_cparams("parallel", "arbitrary"),
        name="norm_matmul",
    )(x, g.reshape(1, D), w, head_gain.reshape(1, N))


def _matmul_residual_kernel(a_ref, w_ref, x_ref, o_ref):
    o_ref[...] = x_ref[...] + _dot(a_ref[...], w_ref[...])


def matmul_residual(a, w, layer, x, bm=512, bn=2048):
    M, K = a.shape
    N = w.shape[2]
    bm = min(bm, M)
    assert M % bm == 0 and N % bn == 0
    return pl.pallas_call(
        _matmul_residual_kernel,
        grid=(M // bm, N // bn),
        in_specs=[
            pl.BlockSpec((bm, K), lambda i, j: (i, 0)),
            pl.BlockSpec((None, K, bn), lambda i, j: (layer, 0, j)),
            pl.BlockSpec((bm, bn), lambda i, j: (i, j)),
        ],
        out_specs=pl.BlockSpec((bm, bn), lambda i, j: (i, j)),
        out_shape=jax.ShapeDtypeStruct((M, N), F32),
        compiler_params=_cparams("parallel", "arbitrary"),
        name="matmul_residual",
    )(a, w, x)


def _ffn_kernel(x_hbm, g_ref, wg_ref, wu_ref, wd_ref, o_ref, h_ref, xbuf, xsem):
    i, j = pl.program_id(0), pl.program_id(1)
    bm = xbuf.shape[0]

    def x_copy(row_block):
        return pltpu.make_async_copy(x_hbm.at[pl.ds(row_block * bm, bm), :], xbuf, xsem)

    @pl.when((i == 0) & (j == 0))
    def _():
        x_copy(0).start()

    def hidden_chunk(first):
        if first:
            x_copy(i).wait()
            h_ref[...] = _rms(xbuf[...], g_ref[...]).astype(BF16)
        h = h_ref[...]
        gate = _dot(h, wg_ref[...])
        up = _dot(h, wu_ref[...])
        act = (gate * jax.nn.sigmoid(gate) * up).astype(BF16)
        down = _dot(act, wd_ref[...])
        if first:
            o_ref[...] = xbuf[...] + down
        else:
            o_ref[...] += down

    pl.when(j == 0)(functools.partial(hidden_chunk, True))
    pl.when(j > 0)(functools.partial(hidden_chunk, False))

    @pl.when((j == 1) & (i + 1 < pl.num_programs(0)))
    def _():
        x_copy(i + 1).start()


def ffn(x, g, w_gu, w_down, layer, bm=1024, bh=512):
    M, D = x.shape
    FH = w_down.shape[1]
    bm = min(bm, M)
    assert M % bm == 0 and FH % bh == 0
    nh = FH // bh
    assert nh >= 2
    return pl.pallas_call(
        _ffn_kernel,
        grid=(M // bm, nh),
        in_specs=[
            pl.BlockSpec(memory_space=pl.ANY),
            pl.BlockSpec((1, D), lambda i, j: (0, 0)),
            pl.BlockSpec((None, D, bh), lambda i, j: (layer, 0, j)),
            pl.BlockSpec((None, D, bh), lambda i, j: (layer, 0, j + nh)),
            pl.BlockSpec((None, bh, D), lambda i, j: (layer, j, 0)),
        ],
        out_specs=pl.BlockSpec((bm, D), lambda i, j: (i, 0)),
        out_shape=jax.ShapeDtypeStruct((M, D), F32),
        scratch_shapes=[pltpu.VMEM((bm, D), BF16), pltpu.VMEM((bm, D), F32),
                        pltpu.SemaphoreType.DMA(())],
        compiler_params=_cparams("arbitrary", "arbitrary"),
        name="ffn",
    )(x, g.reshape(1, D), w_gu, w_gu, w_down)


def _block_rows(ref, rows, span):
    return jnp.concatenate(
        [jnp.broadcast_to(ref[r:r + 1, :], (span, ref.shape[1])) for r in rows], axis=0)


def _gla_kernel(q_ref, k_ref, v_ref, r_ref, a_ref, wa_ref, ba_ref, gn_ref, o_ref, st_ref,
                b_ref, kf_ref, *, chunk, diag, scale, hk, hv, heads):
    C = chunk

    @pl.when(pl.program_id(2) == 0)
    def _():
        st_ref[...] = jnp.zeros_like(st_ref)

    row = lax.broadcasted_iota(jnp.int32, (C, C), 0)
    col = lax.broadcasted_iota(jnp.int32, (C, C), 1)
    rix = lax.broadcasted_iota(jnp.int32, (C, 1), 0)
    dcol = lax.broadcasted_iota(jnp.int32, (hk, C), 1)
    tri = (row >= col).astype(BF16)

    for n in range(q_ref.shape[0] * heads):
        bb, h = divmod(n, heads)
        ks_, vs_ = slice(h * hk, (h + 1) * hk), slice(h * hv, (h + 1) * hv)
        q = q_ref[bb, :, ks_].astype(F32) * scale
        k = k_ref[bb, :, ks_].astype(F32)
        v = v_ref[bb, :, vs_]

        z = _dot(a_ref[bb], wa_ref[:, ks_]) + ba_ref[:, ks_]
        log_a = (jnp.minimum(z, 0.0) - jnp.log(1.0 + jnp.exp(-jnp.abs(z)))) * (LOG2_E / GLA_TAU)
        a_hi = log_a.astype(BF16)
        rem = log_a - a_hi.astype(F32)
        a_mid = rem.astype(BF16)
        a_lo = (rem - a_mid.astype(F32)).astype(BF16)
        b = _dot(tri, a_hi) + _dot(tri, a_mid) + _dot(tri, a_lo)
        b_ref[n] = b
        kf_ref[n] = k

        st = st_ref[n]
        o = _dot_nt((q * jnp.exp2(b)).astype(BF16), st.astype(BF16))

        scores = jnp.zeros((C, C), F32)
        w = C // 2
        while w >= diag:
            blk = 2 * w
            beta = _block_rows(b_ref.at[n], range(w, C, blk), blk)
            right = (rix % blk) >= w
            m = (jnp.where(right, q, k) * jnp.exp2(-jnp.abs(b - beta))).astype(BF16)
            keep = ((row // blk) == (col // blk)) & ((row % blk) >= w) & ((col % blk) < w)
            scores = scores + jnp.where(keep, _dot_nt(m, m), 0.0)
            w //= 2

        dsum = jnp.zeros((C, C), F32)
        for s in range(diag):
            rows = range(s, C, diag)
            bs = _block_rows(b_ref.at[n], rows, diag)
            ks = _block_rows(kf_ref.at[n], rows, diag)
            e = jnp.exp2(jnp.where((rix % diag) >= s, b - bs, -jnp.inf))
            p = (q * ks * e).astype(BF16)
            sel = ((dcol % diag) == s).astype(BF16)
            dsum = dsum + _dot(p, sel)
        scores = scores + jnp.where((row // diag) == (col // diag), dsum, 0.0)

        o = o + _dot(scores.astype(BF16), v)

        b_last = b[C - 1:C, :]
        k_dec = (k * jnp.exp2(b_last - b)).astype(BF16)
        st_ref[n] = st * jnp.exp2(b_last) + _dot_tn(v, k_dec)

        r = r_ref[bb, :, vs_].astype(F32)
        o_ref[bb, :, vs_] = (_rms(o, gn_ref[...]) * (r * jax.nn.sigmoid(r))).astype(o_ref.dtype)


def gla_core(proj, w_a2p, layer, b_a, g_norm, *, dk, dv):
    B, S, _ = proj.shape
    H = GLA_HEADS
    hk, hv = dk // H, dv // H
    C = GLA_CHUNK
    HS = GLA_HEADS_PER_STEP
    NB = min(GLA_BATCH_PER_STEP, B)
    assert S % C == 0 and H % HS == 0 and B % NB == 0
    wk, wv = HS * hk, HS * hv
    k_blk0 = dk // wk
    v_blk0 = 2 * dk // wv
    r_blk0 = (2 * dk + dv) // wv
    a_blk = (2 * dk + 2 * dv) // GLA_RANK_PAD
    kern = functools.partial(_gla_kernel, chunk=C, diag=GLA_DIAG, scale=hk ** -0.5,
                             hk=hk, hv=hv, heads=HS)
    return pl.pallas_call(
        kern,
        grid=(B // NB, H // HS, S // C),
        in_specs=[
            pl.BlockSpec((NB, C, wk), lambda b, h, c: (b, c, h)),
            pl.BlockSpec((NB, C, wk), lambda b, h, c: (b, c, k_blk0 + h)),
            pl.BlockSpec((NB, C, wv), lambda b, h, c: (b, c, v_blk0 + h)),
            pl.BlockSpec((NB, C, wv), lambda b, h, c: (b, c, r_blk0 + h)),
            pl.BlockSpec((NB, C, GLA_RANK_PAD), lambda b, h, c: (b, c, a_blk)),
            pl.BlockSpec((None, GLA_RANK_PAD, wk), lambda b, h, c: (layer, 0, h)),
            pl.BlockSpec((1, wk), lambda b, h, c: (0, h)),
            pl.BlockSpec((1, hv), lambda b, h, c: (0, 0)),
        ],
        out_specs=pl.BlockSpec((NB, C, wv), lambda b, h, c: (b, c, h)),
        out_shape=jax.ShapeDtypeStruct((B, S, dv), BF16),
        scratch_shapes=[pltpu.VMEM((NB * HS, hv, hk), F32), pltpu.VMEM((NB * HS, C, hk), F32),
                        pltpu.VMEM((NB * HS, C, hk), F32)],
        compiler_params=_cparams("parallel", "parallel", "arbitrary"),
        name="gla_core",
    )(proj, proj, proj, proj, proj, w_a2p, b_a.reshape(1, dk), g_norm.reshape(1, hv))


def gla_weights(w_in, w_a2, w_out):
    dk, dv = w_a2.shape[2], w_out.shape[1]
    n_pad = 2 * dk + 2 * dv + GLA_IN_PAD
    w_in_p = jnp.pad(w_in, ((0, 0), (0, 0), (0, n_pad - w_in.shape[2]))).astype(BF16)
    w_a2p = jnp.pad(w_a2, ((0, 0), (0, GLA_RANK_PAD - w_a2.shape[1]), (0, 0))).astype(BF16)
    return w_in_p, w_a2p, w_out.astype(BF16)


def gla_mixer(x, g, weights, layer, b_a, g_norm):
    w_in_p, w_a2p, w_out = weights
    B, S, D = x.shape
    dk, dv = w_a2p.shape[2], w_out.shape[1]
    n_pad = w_in_p.shape[2]
    x2 = x.reshape(B * S, D)
    proj = norm_matmul(x2, g, w_in_p, layer, bn=n_pad // 5)
    o = gla_core(proj.reshape(B, S, n_pad), w_a2p, layer, b_a, g_norm, dk=dk, dv=dv)
    return matmul_residual(o.reshape(B * S, dv), w_out, layer, x2).reshape(B, S, D)


def _pool_kernel(x_ref, g_ref, w_ref, sc_ref, o_ref, hb_ref, *, bm, gw):
    si = pl.program_id(1)
    HL = POOL_HALO

    @pl.when(si == 0)
    def _():
        hb_ref[0:HL, :] = jnp.zeros((HL, hb_ref.shape[1]), F32)

    @pl.when(si > 0)
    def _():
        hb_ref[0:HL, :] = hb_ref[bm:bm + HL, :]

    x = x_ref[0]
    hb_ref[HL:HL + bm, :] = _rms(x, g_ref[...])

    t = si * bm + lax.broadcasted_iota(jnp.int32, (bm, 1), 0)
    for gi, win in enumerate(POOL_WINDOWS):
        cs = slice(gi * gw, (gi + 1) * gw)
        h = hb_ref[HL:HL + bm, cs]
        acc = h
        for d in range(1, win):
            acc = acc + hb_ref[HL - d:HL - d + bm, cs]
        count = jnp.minimum(t + 1, win).astype(F32)
        y = (acc / count - h).astype(BF16)
        o_ref[0, :, cs] = x[:, cs] + _dot(y, w_ref[gi]) * sc_ref[:, cs]


def pool_mixer(x, g, w_pool, scale, bm=256):
    B, S, D = x.shape
    G, gw, _ = w_pool.shape
    assert S % bm == 0 and G == len(POOL_WINDOWS)
    kern = functools.partial(_pool_kernel, bm=bm, gw=gw)
    return pl.pallas_call(
        kern,
        grid=(B, S // bm),
        in_specs=[
            pl.BlockSpec((1, bm, D), lambda b, s: (b, s, 0)),
            pl.BlockSpec((1, D), lambda b, s: (0, 0)),
            pl.BlockSpec((G, gw, gw), lambda b, s: (0, 0, 0)),
            pl.BlockSpec((1, D), lambda b, s: (0, 0)),
        ],
        out_specs=pl.BlockSpec((1, bm, D), lambda b, s: (b, s, 0)),
        out_shape=jax.ShapeDtypeStruct((B, S, D), F32),
        scratch_shapes=[pltpu.VMEM((POOL_HALO + bm, D), F32)],
        compiler_params=_cparams("parallel", "arbitrary"),
        name="pool_mixer",
    )(x, g.reshape(1, D), w_pool.astype(BF16), scale.reshape(1, D))


def _rel_bucket(rel):
    n = jnp.maximum(rel, 0)
    max_exact = REL_BUCKETS // 2
    nf = jnp.maximum(n, 1).astype(F32)
    large = max_exact + (jnp.log(nf / max_exact) / math.log(REL_MAX_DIST / max_exact)
                         * (REL_BUCKETS - max_exact)).astype(jnp.int32)
    large = jnp.minimum(large, REL_BUCKETS - 1)
    return jnp.where(n < max_exact, n, large)


def _diff_attn_kernel(q_ref, k_ref, v_ref, rb_ref, lam_ref, sg_ref, o_ref,
                      m_ref, l_ref, acc_ref, bias_ref, s_ref, p_ref, a_ref, *, blk, hd, lam_init):
    qi = pl.program_id(2)
    L = 2 * blk
    lanes = m_ref.shape[-1]
    units = m_ref.shape[0]
    vd = 2 * hd

    @pl.when((qi == 0) & (pl.program_id(1) == 0))
    def _():
        for u in range(units):
            for off in range(2):
                r = jnp.broadcast_to(rb_ref[u, off:off + 1, :], (blk, L))
                bias_ref[u, off] = pltpu.roll(r, L - blk + 1, 1, stride=1, stride_axis=0)[:, :blk]

    m_ref[...] = jnp.full(m_ref.shape, -jnp.inf, F32)
    l_ref[...] = jnp.zeros(l_ref.shape, F32)
    acc_ref[...] = jnp.zeros(acc_ref.shape, F32)

    def logits(j, u):
        start = pl.multiple_of(j * blk, blk)
        cols = slice(u * hd, (u + 1) * hd)
        s_ref[u] = _dot_nt(q_ref[0, :, cols], k_ref[0, pl.ds(start, blk), cols])

    def step(j, off, has_next=True):
        start = pl.multiple_of(j * blk, blk)
        for u in range(units):
            for r0 in range(0, blk, DIFF_STRIP):
                rows = slice(r0, r0 + DIFF_STRIP)
                s = s_ref[u, rows, :]
                if off is not None:
                    s = s + bias_ref[u, off, rows, :]
                m_old = m_ref[u, rows, :]
                m_new = jnp.maximum(m_old, jnp.max(s, axis=-1, keepdims=True))
                alpha = jnp.exp2(m_old - m_new)
                p = jnp.exp2(s - jnp.concatenate([m_new] * (blk // lanes), axis=1))
                psum = p[:, :lanes]
                for t in range(1, blk // lanes):
                    psum = psum + p[:, t * lanes:(t + 1) * lanes]
                l_ref[u, rows, :] = alpha * l_ref[u, rows, :] + psum
                p_ref[u, rows, :] = p.astype(BF16)
                a_ref[u, rows, :] = alpha
                m_ref[u, rows, :] = m_new
            if has_next:
                logits(j + 1, u)
            alpha = jnp.concatenate([a_ref[u]] * (vd // lanes), axis=1)
            vb = v_ref[0, pl.ds(start, blk), (u // 2) * vd:(u // 2 + 1) * vd]
            acc_ref[u] = alpha * acc_ref[u] + _dot(p_ref[u], vb)

    def far_body(j, carry):
        step(j, None)
        return carry

    for u in range(units):
        logits(0, u)
    lax.fori_loop(0, jnp.maximum(qi - 1, 0), far_body, 0)

    @pl.when(qi >= 1)
    def _():
        step(qi - 1, 1)

    step(qi, 0, has_next=False)

    lp = lam_ref[...]
    lam = (jnp.exp(jnp.sum(lp[0:1] * lp[1:2], axis=-1, keepdims=True))
           - jnp.exp(jnp.sum(lp[2:3] * lp[3:4], axis=-1, keepdims=True)) + lam_init)
    for h in range(units // 2):
        l0 = jnp.sum(l_ref[2 * h], axis=-1, keepdims=True)
        l1 = jnp.sum(l_ref[2 * h + 1], axis=-1, keepdims=True)
        o = acc_ref[2 * h] / l0 - lam * (acc_ref[2 * h + 1] / l1)
        o_ref[0, :, h * vd:(h + 1) * vd] = (_rms(o, sg_ref[...]) * (1.0 - lam_init)).astype(o_ref.dtype)


def diff_attention(proj, rel_vecs, lam_params, sub_gain, *, d_model, lam_init, blk):
    B, S, _ = proj.shape
    H = DIFF_HEADS
    HS = DIFF_HEADS_PER_STEP
    vd = d_model // H
    hd = vd // 2
    lanes = 128
    units = 2 * HS
    wd = HS * vd
    HG = H // HS
    assert S % blk == 0 and blk % lanes == 0 and H % HS == 0
    kern = functools.partial(_diff_attn_kernel, blk=blk, hd=hd, lam_init=lam_init)
    return pl.pallas_call(
        kern,
        grid=(HG, B, S // blk),
        in_specs=[
            pl.BlockSpec((1, blk, wd), lambda h, b, i: (b, i, h)),
            pl.BlockSpec((1, S, wd), lambda h, b, i: (b, 0, HG + h)),
            pl.BlockSpec((1, S, wd), lambda h, b, i: (b, 0, 2 * HG + h)),
            pl.BlockSpec((units, 2, 2 * blk), lambda h, b, i: (h, 0, 0)),
            pl.BlockSpec((4, hd), lambda h, b, i: (0, 0)),
            pl.BlockSpec((1, vd), lambda h, b, i: (0, 0)),
        ],
        out_specs=pl.BlockSpec((1, blk, wd), lambda h, b, i: (b, i, h)),
        out_shape=jax.ShapeDtypeStruct((B, S, d_model), BF16),
        scratch_shapes=[
            pltpu.VMEM((units, blk, lanes), F32),
            pltpu.VMEM((units, blk, lanes), F32),
            pltpu.VMEM((units, blk, vd), F32),
            pltpu.VMEM((units, 2, blk, blk), F32),
            pltpu.VMEM((units, blk, blk), F32),
            pltpu.VMEM((units, blk, blk), BF16),
            pltpu.VMEM((units, blk, lanes), F32),
        ],
        compiler_params=_cparams("arbitrary", "arbitrary", "arbitrary"),
        name="diff_attention",
    )(proj, proj, proj, rel_vecs, lam_params, sub_gain.reshape(1, vd))


def _rel_bias_vectors(rel_table, blk):
    H2 = rel_table.shape[1]
    far = rel_table[REL_BUCKETS - 1]
    y = jnp.arange(2 * blk)
    vecs = []
    for off in (0, 1):
        rel = off * blk + blk - 1 - y
        v = rel_table[_rel_bucket(rel)].astype(F32) - far
        vecs.append(jnp.where((rel >= 0)[:, None], v * LOG2_E, -jnp.inf).T)
    return jnp.stack(vecs, axis=1)


def diff_attn_mixer(x, g, w_in, w_out, layer, q_gain, k_gain, lam_params, sub_gain, rel_table,
                    layer_idx, blk=512):
    B, S, D = x.shape
    assert blk >= REL_MAX_DIST
    hd = D // DIFF_HEADS // 2
    lam_init = 0.8 - 0.6 * math.exp(-0.3 * layer_idx)
    n_heads2 = D // hd
    head_gain = jnp.concatenate([jnp.tile(q_gain, n_heads2) * (hd ** -0.5 * LOG2_E),
                                 jnp.tile(k_gain, n_heads2), jnp.ones((D,), F32)])
    x2 = x.reshape(B * S, D)
    proj = norm_matmul(x2, g, w_in, layer, head_gain=head_gain, head_norm_cols=2 * D, head_dim=hd)
    o = diff_attention(proj.reshape(B, S, 3 * D), _rel_bias_vectors(rel_table, blk), lam_params,
                       sub_gain, d_model=D, lam_init=lam_init, blk=blk)
    return matmul_residual(o.reshape(B * S, D), w_out, layer, x2).reshape(B, S, D)


def kernel(x, norm_g, gla_w_in, gla_w_a2, gla_b_a, gla_g_norm, gla_w_out, pool_w, pool_scale,
           diff_w_in, diff_q_gain, diff_k_gain, diff_lambda, diff_sub_gain, diff_w_out, rel_bias,
           ffn_w_gu, ffn_w_down):
    B, S, D = x.shape
    depth = norm_g.shape[0]
    gla_w = gla_weights(gla_w_in, gla_w_a2, gla_w_out)
    diff_w_in_b, diff_w_out_b = diff_w_in.astype(BF16), diff_w_out.astype(BF16)
    ffn_w_gu_b, ffn_w_down_b = ffn_w_gu.astype(BF16), ffn_w_down.astype(BF16)
    for i in range(depth):
        kind, slot = i % N_MIXERS, i // N_MIXERS
        if kind == 0:
            x = gla_mixer(x, norm_g[i, 0], gla_w, slot, gla_b_a[slot], gla_g_norm[slot])
        elif kind == 1:
            x = pool_mixer(x, norm_g[i, 0], pool_w[slot], pool_scale[slot])
        else:
            x = diff_attn_mixer(x, norm_g[i, 0], diff_w_in_b, diff_w_out_b, slot, diff_q_gain[slot],
                                diff_k_gain[slot], diff_lambda[slot], diff_sub_gain[slot],
                                rel_bias, i)
        x = ffn(x.reshape(B * S, D), norm_g[i, 1], ffn_w_gu_b, ffn_w_down_b, i).reshape(B, S, D)
    return x
```

```python
import functools
import math

import jax
import jax.numpy as jnp
from jax import lax
from jax.experimental import pallas as pl
from jax.experimental.pallas import tpu as pltpu

F32 = jnp.float32
BF16 = jnp.bfloat16

EPS = 1e-6
N_MIXERS = 3

GLA_HEADS = 4
GLA_RANK = 16
GLA_TAU = 16.0
GLA_CHUNK = 128
GLA_DIAG = 8
LOG2_E = math.log2(math.e)
GLA_RANK_PAD = 128
GLA_HEADS_PER_STEP = 4
GLA_BATCH_PER_STEP = 2
GLA_IN_PAD = 256

POOL_WINDOWS = (2, 4, 8, 16)
POOL_HALO = 16

DIFF_HEADS = 8
DIFF_STRIP = 32
DIFF_HEADS_PER_STEP = 2
REL_BUCKETS = 32
REL_MAX_DIST = 128

VMEM_LIMIT_BYTES = 56 * 1024 * 1024


def _cparams(*sem):
    return pltpu.CompilerParams(dimension_semantics=sem, vmem_limit_bytes=VMEM_LIMIT_BYTES)


def _rms(x, g):
    ms = jnp.mean(x * x, axis=-1, keepdims=True)
    return x * lax.rsqrt(ms + EPS) * g


def _dot(a, b):
    return jnp.dot(a, b, preferred_element_type=F32)


def _dot_nt(a, b):
    return lax.dot_general(a, b, (((1,), (1,)), ((), ())), preferred_element_type=F32)


def _dot_tn(a, b):
    return lax.dot_general(a, b, (((0,), (0,)), ((), ())), preferred_element_type=F32)


def _norm_matmul_kernel(x_ref, g_ref, w_ref, hg_ref, o_ref, h_ref, *, head_norm_blocks, head_dim):
    j = pl.program_id(1)

    def column_block(first, head_norm):
        if first:
            h_ref[...] = _rms(x_ref[...], g_ref[...]).astype(BF16)
        acc = _dot(h_ref[...], w_ref[...])
        if head_norm:
            for c in range(acc.shape[1] // head_dim):
                sl = slice(c * head_dim, (c + 1) * head_dim)
                o_ref[:, sl] = _rms(acc[:, sl], hg_ref[:, sl]).astype(o_ref.dtype)
        else:
            o_ref[...] = acc.astype(o_ref.dtype)

    pl.when(j == 0)(functools.partial(column_block, True, head_norm_blocks > 0))
    if head_norm_blocks > 1:
        pl.when((j > 0) & (j < head_norm_blocks))(functools.partial(column_block, False, True))
    pl.when(j >= max(head_norm_blocks, 1))(functools.partial(column_block, False, False))


def norm_matmul(x, g, w, layer, head_gain=None, head_norm_cols=0, head_dim=128, bm=1024, bn=1024):
    M, D = x.shape
    N = w.shape[2]
    bm = min(bm, M)
    assert M % bm == 0 and N % bn == 0 and head_norm_cols % bn == 0
    if head_gain is None:
        head_gain = jnp.ones((N,), F32)
    kern = functools.partial(_norm_matmul_kernel, head_norm_blocks=head_norm_cols // bn,
                             head_dim=head_dim)
    return pl.pallas_call(
        kern,
        grid=(M // bm, N // bn),
        in_specs=[
            pl.BlockSpec((bm, D), lambda i, j: (i, 0)),
            pl.BlockSpec((1, D), lambda i, j: (0, 0)),
            pl.BlockSpec((None, D, bn), lambda i, j: (layer, 0, j)),
            pl.BlockSpec((1, bn), lambda i, j: (0, j)),
        ],
        out_specs=pl.BlockSpec((bm, bn), lambda i, j: (i, j)),
        out_shape=jax.ShapeDtypeStruct((M, N), BF16),
        scratch_shapes=[pltpu.VMEM((bm, D), BF16)],
        compiler_params=_cparams("parallel", "arbitrary"),
        name="norm_matmul",
    )(x, g.reshape(1, D), w, head_gain.reshape(1, N))


def _matmul_residual_kernel(a_ref, w_ref, x_ref, o_ref):
    o_ref[...] = x_ref[...] + _dot(a_ref[...], w_ref[...])


def matmul_residual(a, w, layer, x, bm=512, bn=2048):
    M, K = a.shape
    N = w.shape[2]
    bm = min(bm, M)
    assert M % bm == 0 and N % bn == 0
    return pl.pallas_call(
        _matmul_residual_kernel,
        grid=(M // bm, N // bn),
        in_specs=[
            pl.BlockSpec((bm, K), lambda i, j: (i, 0)),
            pl.BlockSpec((None, K, bn), lambda i, j: (layer, 0, j)),
            pl.BlockSpec((bm, bn), lambda i, j: (i, j)),
        ],
        out_specs=pl.BlockSpec((bm, bn), lambda i, j: (i, j)),
        out_shape=jax.ShapeDtypeStruct((M, N), F32),
        compiler_params=_cparams("parallel", "arbitrary"),
        name="matmul_residual",
    )(a, w, x)


def _ffn_kernel(x_hbm, g_ref, wg_ref, wu_ref, wd_ref, o_ref, h_ref, xbuf, xsem):
    i, j = pl.program_id(0), pl.program_id(1)
    bm = xbuf.shape[0]

    def x_copy(row_block):
        return pltpu.make_async_copy(x_hbm.at[pl.ds(row_block * bm, bm), :], xbuf, xsem)

    @pl.when((i == 0) & (j == 0))
    def _():
        x_copy(0).start()

    def hidden_chunk(first):
        if first:
            x_copy(i).wait()
            h_ref[...] = _rms(xbuf[...], g_ref[...]).astype(BF16)
        h = h_ref[...]
        gate = _dot(h, wg_ref[...])
        up = _dot(h, wu_ref[...])
        act = (gate * jax.nn.sigmoid(gate) * up).astype(BF16)
        down = _dot(act, wd_ref[...])
        if first:
            o_ref[...] = xbuf[...] + down
        else:
            o_ref[...] += down

    pl.when(j == 0)(functools.partial(hidden_chunk, True))
    pl.when(j > 0)(functools.partial(hidden_chunk, False))

    @pl.when((j == 1) & (i + 1 < pl.num_programs(0)))
    def _():
        x_copy(i + 1).start()


def ffn(x, g, w_gu, w_down, layer, bm=1024, bh=512):
    M, D = x.shape
    FH = w_down.shape[1]
    bm = min(bm, M)
    assert M % bm == 0 and FH % bh == 0
    nh = FH // bh
    assert nh >= 2
    return pl.pallas_call(
        _ffn_kernel,
        grid=(M // bm, nh),
        in_specs=[
            pl.BlockSpec(memory_space=pl.ANY),
            pl.BlockSpec((1, D), lambda i, j: (0, 0)),
            pl.BlockSpec((None, D, bh), lambda i, j: (layer, 0, j)),
            pl.BlockSpec((None, D, bh), lambda i, j: (layer, 0, j + nh)),
            pl.BlockSpec((None, bh, D), lambda i, j: (layer, j, 0)),
        ],
        out_specs=pl.BlockSpec((bm, D), lambda i, j: (i, 0)),
        out_shape=jax.ShapeDtypeStruct((M, D), F32),
        scratch_shapes=[pltpu.VMEM((bm, D), BF16), pltpu.VMEM((bm, D), F32),
                        pltpu.SemaphoreType.DMA(())],
        compiler_params=_cparams("arbitrary", "arbitrary"),
        name="ffn",
    )(x, g.reshape(1, D), w_gu, w_gu, w_down)


def _block_rows(ref, rows, span):
    return jnp.concatenate(
        [jnp.broadcast_to(ref[r:r + 1, :], (span, ref.shape[1])) for r in rows], axis=0)


def _gla_kernel(q_ref, k_ref, v_ref, r_ref, a_ref, wa_ref, ba_ref, gn_ref, o_ref, st_ref,
                b_ref, kf_ref, *, chunk, diag, scale, hk, hv, heads):
    C = chunk

    @pl.when(pl.program_id(2) == 0)
    def _():
        st_ref[...] = jnp.zeros_like(st_ref)

    row = lax.broadcasted_iota(jnp.int32, (C, C), 0)
    col = lax.broadcasted_iota(jnp.int32, (C, C), 1)
    rix = lax.broadcasted_iota(jnp.int32, (C, 1), 0)
    dcol = lax.broadcasted_iota(jnp.int32, (hk, C), 1)
    tri = (row >= col).astype(BF16)

    chains = range(q_ref.shape[0] * heads)
    cols = [(slice((n % heads) * hk, (n % heads + 1) * hk),
             slice((n % heads) * hv, (n % heads + 1) * hv)) for n in chains]
    bbs = [n // heads for n in chains]
    qs, ks_f, bs_ = {}, {}, {}
    for n in chains:
        ks_, _ = cols[n]
        qs[n] = q_ref[bbs[n], :, ks_].astype(F32) * scale
        ks_f[n] = k_ref[bbs[n], :, ks_].astype(F32)
        z = _dot(a_ref[bbs[n]], wa_ref[:, ks_]) + ba_ref[:, ks_]
        log_a = (jnp.minimum(z, 0.0) - jnp.log(1.0 + jnp.exp(-jnp.abs(z)))) * (LOG2_E / GLA_TAU)
        a_hi = log_a.astype(BF16)
        rem = log_a - a_hi.astype(F32)
        a_mid = rem.astype(BF16)
        a_lo = (rem - a_mid.astype(F32)).astype(BF16)
        b = _dot(tri, a_hi) + _dot(tri, a_mid) + _dot(tri, a_lo)
        bs_[n] = b
        b_ref[n] = b
        kf_ref[n] = ks_f[n]

    outs, scores = {}, {}
    for n in chains:
        outs[n] = _dot_nt((qs[n] * jnp.exp2(bs_[n])).astype(BF16), st_ref[n].astype(BF16))
        scores[n] = jnp.zeros((C, C), F32)

    w = C // 2
    while w >= diag:
        blk = 2 * w
        right = (rix % blk) >= w
        keep = ((row // blk) == (col // blk)) & ((row % blk) >= w) & ((col % blk) < w)
        for n in chains:
            beta = _block_rows(b_ref.at[n], range(w, C, blk), blk)
            m = (jnp.where(right, qs[n], ks_f[n]) * jnp.exp2(-jnp.abs(bs_[n] - beta))).astype(BF16)
            scores[n] = scores[n] + jnp.where(keep, _dot_nt(m, m), 0.0)
        w //= 2

    dsum = {n: jnp.zeros((C, C), F32) for n in chains}
    for s in range(diag):
        rows = range(s, C, diag)
        sel = ((dcol % diag) == s).astype(BF16)
        for n in chains:
            bs = _block_rows(b_ref.at[n], rows, diag)
            ks = _block_rows(kf_ref.at[n], rows, diag)
            e = jnp.exp2(jnp.where((rix % diag) >= s, bs_[n] - bs, -jnp.inf))
            p = (qs[n] * ks * e).astype(BF16)
            dsum[n] = dsum[n] + _dot(p, sel)

    for n in chains:
        _, vs_ = cols[n]
        v = v_ref[bbs[n], :, vs_]
        sc = scores[n] + jnp.where((row // diag) == (col // diag), dsum[n], 0.0)
        o = outs[n] + _dot(sc.astype(BF16), v)

        b = bs_[n]
        b_last = b[C - 1:C, :]
        k_dec = (ks_f[n] * jnp.exp2(b_last - b)).astype(BF16)
        st_ref[n] = st_ref[n] * jnp.exp2(b_last) + _dot_tn(v, k_dec)

        r = r_ref[bbs[n], :, vs_].astype(F32)
        o_ref[bbs[n], :, vs_] = (_rms(o, gn_ref[...]) * (r * jax.nn.sigmoid(r))).astype(o_ref.dtype)


def gla_core(proj, w_a2p, layer, b_a, g_norm, *, dk, dv):
    B, S, _ = proj.shape
    H = GLA_HEADS
    hk, hv = dk // H, dv // H
    C = GLA_CHUNK
    HS = GLA_HEADS_PER_STEP
    NB = min(GLA_BATCH_PER_STEP, B)
    assert S % C == 0 and H % HS == 0 and B % NB == 0
    wk, wv = HS * hk, HS * hv
    k_blk0 = dk // wk
    v_blk0 = 2 * dk // wv
    r_blk0 = (2 * dk + dv) // wv
    a_blk = (2 * dk + 2 * dv) // GLA_RANK_PAD
    kern = functools.partial(_gla_kernel, chunk=C, diag=GLA_DIAG, scale=hk ** -0.5,
                             hk=hk, hv=hv, heads=HS)
    return pl.pallas_call(
        kern,
        grid=(B // NB, H // HS, S // C),
        in_specs=[
            pl.BlockSpec((NB, C, wk), lambda b, h, c: (b, c, h)),
            pl.BlockSpec((NB, C, wk), lambda b, h, c: (b, c, k_blk0 + h)),
            pl.BlockSpec((NB, C, wv), lambda b, h, c: (b, c, v_blk0 + h)),
            pl.BlockSpec((NB, C, wv), lambda b, h, c: (b, c, r_blk0 + h)),
            pl.BlockSpec((NB, C, GLA_RANK_PAD), lambda b, h, c: (b, c, a_blk)),
            pl.BlockSpec((None, GLA_RANK_PAD, wk), lambda b, h, c: (layer, 0, h)),
            pl.BlockSpec((1, wk), lambda b, h, c: (0, h)),
            pl.BlockSpec((1, hv), lambda b, h, c: (0, 0)),
        ],
        out_specs=pl.BlockSpec((NB, C, wv), lambda b, h, c: (b, c, h)),
        out_shape=jax.ShapeDtypeStruct((B, S, dv), BF16),
        scratch_shapes=[pltpu.VMEM((NB * HS, hv, hk), F32), pltpu.VMEM((NB * HS, C, hk), F32),
                        pltpu.VMEM((NB * HS, C, hk), F32)],
        compiler_params=_cparams("parallel", "parallel", "arbitrary"),
        name="gla_core",
    )(proj, proj, proj, proj, proj, w_a2p, b_a.reshape(1, dk), g_norm.reshape(1, hv))


def gla_weights(w_in, w_a2, w_out):
    dk, dv = w_a2.shape[2], w_out.shape[1]
    n_pad = 2 * dk + 2 * dv + GLA_IN_PAD
    w_in_p = jnp.pad(w_in, ((0, 0), (0, 0), (0, n_pad - w_in.shape[2]))).astype(BF16)
    w_a2p = jnp.pad(w_a2, ((0, 0), (0, GLA_RANK_PAD - w_a2.shape[1]), (0, 0))).astype(BF16)
    return w_in_p, w_a2p, w_out.astype(BF16)


def gla_mixer(x, g, weights, layer, b_a, g_norm):
    w_in_p, w_a2p, w_out = weights
    B, S, D = x.shape
    dk, dv = w_a2p.shape[2], w_out.shape[1]
    n_pad = w_in_p.shape[2]
    x2 = x.reshape(B * S, D)
    proj = norm_matmul(x2, g, w_in_p, layer, bn=n_pad // 5)
    o = gla_core(proj.reshape(B, S, n_pad), w_a2p, layer, b_a, g_norm, dk=dk, dv=dv)
    return matmul_residual(o.reshape(B * S, dv), w_out, layer, x2).reshape(B, S, D)


def _pool_kernel(x_ref, g_ref, w_ref, sc_ref, o_ref, hb_ref, *, bm, gw):
    si = pl.program_id(1)
    HL = POOL_HALO

    @pl.when(si == 0)
    def _():
        hb_ref[0:HL, :] = jnp.zeros((HL, hb_ref.shape[1]), F32)

    @pl.when(si > 0)
    def _():
        hb_ref[0:HL, :] = hb_ref[bm:bm + HL, :]

    x = x_ref[0]
    hb_ref[HL:HL + bm, :] = _rms(x, g_ref[...])

    t = si * bm + lax.broadcasted_iota(jnp.int32, (bm, 1), 0)
    for gi, win in enumerate(POOL_WINDOWS):
        cs = slice(gi * gw, (gi + 1) * gw)
        h = hb_ref[HL:HL + bm, cs]
        acc = h
        for d in range(1, win):
            acc = acc + hb_ref[HL - d:HL - d + bm, cs]
        count = jnp.minimum(t + 1, win).astype(F32)
        y = (acc / count - h).astype(BF16)
        o_ref[0, :, cs] = x[:, cs] + _dot(y, w_ref[gi]) * sc_ref[:, cs]


def pool_mixer(x, g, w_pool, scale, bm=256):
    B, S, D = x.shape
    G, gw, _ = w_pool.shape
    assert S % bm == 0 and G == len(POOL_WINDOWS)
    kern = functools.partial(_pool_kernel, bm=bm, gw=gw)
    return pl.pallas_call(
        kern,
        grid=(B, S // bm),
        in_specs=[
            pl.BlockSpec((1, bm, D), lambda b, s: (b, s, 0)),
            pl.BlockSpec((1, D), lambda b, s: (0, 0)),
            pl.BlockSpec((G, gw, gw), lambda b, s: (0, 0, 0)),
            pl.BlockSpec((1, D), lambda b, s: (0, 0)),
        ],
        out_specs=pl.BlockSpec((1, bm, D), lambda b, s: (b, s, 0)),
        out_shape=jax.ShapeDtypeStruct((B, S, D), F32),
        scratch_shapes=[pltpu.VMEM((POOL_HALO + bm, D), F32)],
        compiler_params=_cparams("parallel", "arbitrary"),
        name="pool_mixer",
    )(x, g.reshape(1, D), w_pool.astype(BF16), scale.reshape(1, D))


def _rel_bucket(rel):
    n = jnp.maximum(rel, 0)
    max_exact = REL_BUCKETS // 2
    nf = jnp.maximum(n, 1).astype(F32)
    large = max_exact + (jnp.log(nf / max_exact) / math.log(REL_MAX_DIST / max_exact)
                         * (REL_BUCKETS - max_exact)).astype(jnp.int32)
    large = jnp.minimum(large, REL_BUCKETS - 1)
    return jnp.where(n < max_exact, n, large)


def _diff_attn_kernel(q_ref, k_ref, v_ref, rb_ref, lam_ref, sg_ref, o_ref,
                      m_ref, l_ref, acc_ref, bias_ref, s_ref, p_ref, a_ref, *, blk, hd, lam_init):
    qi = pl.program_id(2)
    L = 2 * blk
    lanes = m_ref.shape[-1]
    units = m_ref.shape[0]
    vd = 2 * hd

    @pl.when((qi == 0) & (pl.program_id(1) == 0))
    def _():
        for u in range(units):
            for off in range(2):
                r = jnp.broadcast_to(rb_ref[u, off:off + 1, :], (blk, L))
                bias_ref[u, off] = pltpu.roll(r, L - blk + 1, 1, stride=1, stride_axis=0)[:, :blk]

    m_ref[...] = jnp.full(m_ref.shape, -jnp.inf, F32)
    l_ref[...] = jnp.zeros(l_ref.shape, F32)
    acc_ref[...] = jnp.zeros(acc_ref.shape, F32)

    def logits(j, u):
        start = pl.multiple_of(j * blk, blk)
        cols = slice(u * hd, (u + 1) * hd)
        s_ref[u] = _dot_nt(q_ref[0, :, cols], k_ref[0, pl.ds(start, blk), cols])

    def step(j, off, has_next=True):
        start = pl.multiple_of(j * blk, blk)
        for u in range(units):
            for r0 in range(0, blk, DIFF_STRIP):
                rows = slice(r0, r0 + DIFF_STRIP)
                s = s_ref[u, rows, :]
                if off is not None:
                    s = s + bias_ref[u, off, rows, :]
                m_old = m_ref[u, rows, :]
                m_new = jnp.maximum(m_old, jnp.max(s, axis=-1, keepdims=True))
                alpha = jnp.exp2(m_old - m_new)
                p = jnp.exp2(s - jnp.concatenate([m_new] * (blk // lanes), axis=1))
                psum = p[:, :lanes]
                for t in range(1, blk // lanes):
                    psum = psum + p[:, t * lanes:(t + 1) * lanes]
                l_ref[u, rows, :] = alpha * l_ref[u, rows, :] + psum
                p_ref[u, rows, :] = p.astype(BF16)
                a_ref[u, rows, :] = alpha
                m_ref[u, rows, :] = m_new
            if has_next:
                logits(j + 1, u)
            alpha = jnp.concatenate([a_ref[u]] * (vd // lanes), axis=1)
            vb = v_ref[0, pl.ds(start, blk), (u // 2) * vd:(u // 2 + 1) * vd]
            acc_ref[u] = alpha * acc_ref[u] + _dot(p_ref[u], vb)

    def far_body(j, carry):
        step(j, None)
        return carry

    for u in range(units):
        logits(0, u)
    lax.fori_loop(0, jnp.maximum(qi - 1, 0), far_body, 0)

    @pl.when(qi >= 1)
    def _():
        step(qi - 1, 1)

    step(qi, 0, has_next=False)

    lp = lam_ref[...]
    lam = (jnp.exp(jnp.sum(lp[0:1] * lp[1:2], axis=-1, keepdims=True))
           - jnp.exp(jnp.sum(lp[2:3] * lp[3:4], axis=-1, keepdims=True)) + lam_init)
    for h in range(units // 2):
        l0 = jnp.sum(l_ref[2 * h], axis=-1, keepdims=True)
        l1 = jnp.sum(l_ref[2 * h + 1], axis=-1, keepdims=True)
        o = acc_ref[2 * h] / l0 - lam * (acc_ref[2 * h + 1] / l1)
        o_ref[0, :, h * vd:(h + 1) * vd] = (_rms(o, sg_ref[...]) * (1.0 - lam_init)).astype(o_ref.dtype)


def diff_attention(proj, rel_vecs, lam_params, sub_gain, *, d_model, lam_init, blk):
    B, S, _ = proj.shape
    H = DIFF_HEADS
    HS = DIFF_HEADS_PER_STEP
    vd = d_model // H
    hd = vd // 2
    lanes = 128
    units = 2 * HS
    wd = HS * vd
    HG = H // HS
    assert S % blk == 0 and blk % lanes == 0 and H % HS == 0
    kern = functools.partial(_diff_attn_kernel, blk=blk, hd=hd, lam_init=lam_init)
    return pl.pallas_call(
        kern,
        grid=(HG, B, S // blk),
        in_specs=[
            pl.BlockSpec((1, blk, wd), lambda h, b, i: (b, i, h)),
            pl.BlockSpec((1, S, wd), lambda h, b, i: (b, 0, HG + h)),
            pl.BlockSpec((1, S, wd), lambda h, b, i: (b, 0, 2 * HG + h)),
            pl.BlockSpec((units, 2, 2 * blk), lambda h, b, i: (h, 0, 0)),
            pl.BlockSpec((4, hd), lambda h, b, i: (0, 0)),
            pl.BlockSpec((1, vd), lambda h, b, i: (0, 0)),
        ],
        out_specs=pl.BlockSpec((1, blk, wd), lambda h, b, i: (b, i, h)),
        out_shape=jax.ShapeDtypeStruct((B, S, d_model), BF16),
        scratch_shapes=[
            pltpu.VMEM((units, blk, lanes), F32),
            pltpu.VMEM((units, blk, lanes), F32),
            pltpu.VMEM((units, blk, vd), F32),
            pltpu.VMEM((units, 2, blk, blk), F32),
            pltpu.VMEM((units, blk, blk), F32),
            pltpu.VMEM((units, blk, blk), BF16),
            pltpu.VMEM((units, blk, lanes), F32),
        ],
        compiler_params=_cparams("arbitrary", "arbitrary", "arbitrary"),
        name="diff_attention",
    )(proj, proj, proj, rel_vecs, lam_params, sub_gain.reshape(1, vd))


def _rel_bias_vectors(rel_table, blk):
    H2 = rel_table.shape[1]
    far = rel_table[REL_BUCKETS - 1]
    y = jnp.arange(2 * blk)
    vecs = []
    for off in (0, 1):
        rel = off * blk + blk - 1 - y
        v = rel_table[_rel_bucket(rel)].astype(F32) - far
        vecs.append(jnp.where((rel >= 0)[:, None], v * LOG2_E, -jnp.inf).T)
    return jnp.stack(vecs, axis=1)


def diff_attn_mixer(x, g, w_in, w_out, layer, q_gain, k_gain, lam_params, sub_gain, rel_table,
                    layer_idx, blk=512):
    B, S, D = x.shape
    assert blk >= REL_MAX_DIST
    hd = D // DIFF_HEADS // 2
    lam_init = 0.8 - 0.6 * math.exp(-0.3 * layer_idx)
    n_heads2 = D // hd
    head_gain = jnp.concatenate([jnp.tile(q_gain, n_heads2) * (hd ** -0.5 * LOG2_E),
                                 jnp.tile(k_gain, n_heads2), jnp.ones((D,), F32)])
    x2 = x.reshape(B * S, D)
    proj = norm_matmul(x2, g, w_in, layer, head_gain=head_gain, head_norm_cols=2 * D, head_dim=hd)
    o = diff_attention(proj.reshape(B, S, 3 * D), _rel_bias_vectors(rel_table, blk), lam_params,
                       sub_gain, d_model=D, lam_init=lam_init, blk=blk)
    return matmul_residual(o.reshape(B * S, D), w_out, layer, x2).reshape(B, S, D)


def kernel(x, norm_g, gla_w_in, gla_w_a2, gla_b_a, gla_g_norm, gla_w_out, pool_w, pool_scale,
           diff_w_in, diff_q_gain, diff_k_gain, diff_lambda, diff_sub_gain, diff_w_out, rel_bias,
           ffn_w_gu, ffn_w_down):
    B, S, D = x.shape
    depth = norm_g.shape[0]
    gla_w = gla_weights(gla_w_in, gla_w_a2, gla_w_out)
    diff_w_in_b, diff_w_out_b = diff_w_in.astype(BF16), diff_w_out.astype(BF16)
    ffn_w_gu_b, ffn_w_down_b = ffn_w_gu.astype(BF16), ffn_w_down.astype(BF16)
    for i in range(depth):
        kind, slot = i % N_MIXERS, i // N_MIXERS
        if kind == 0:
            x = gla_mixer(x, norm_g[i, 0], gla_w, slot, gla_b_a[slot], gla_g_norm[slot])
        elif kind == 1:
            x = pool_mixer(x, norm_g[i, 0], pool_w[slot], pool_scale[slot])
        else:
            x = diff_attn_mixer(x, norm_g[i, 0], diff_w_in_b, diff_w_out_b, slot, diff_q_gain[slot],
                                diff_k_gain[slot], diff_lambda[slot], diff_sub_gain[slot],
                                rel_bias, i)
        x = ffn(x.reshape(B * S, D), norm_g[i, 1], ffn_w_gu_b, ffn_w_down_b, i).reshape(B, S, D)
    return x
```

```python
import functools
import math

import jax
import jax.numpy as jnp
from jax import lax
from jax.experimental import pallas as pl
from jax.experimental.pallas import tpu as pltpu

F32 = jnp.float32
BF16 = jnp.bfloat16

EPS = 1e-6
N_MIXERS = 3

GLA_HEADS = 4
GLA_RANK = 16
GLA_TAU = 16.0
GLA_CHUNK = 128
GLA_DIAG = 8
LOG2_E = math.log2(math.e)
GLA_RANK_PAD = 128
GLA_HEADS_PER_STEP = 4
GLA_BATCH_PER_STEP = 2
GLA_IN_PAD = 256

POOL_WINDOWS = (2, 4, 8, 16)
POOL_HALO = 16

DIFF_HEADS = 8
DIFF_STRIP = 32
DIFF_HEADS_PER_STEP = 2
REL_BUCKETS = 32
REL_MAX_DIST = 128

VMEM_LIMIT_BYTES = 56 * 1024 * 1024


def _cparams(*sem):
    return pltpu.CompilerParams(dimension_semantics=sem, vmem_limit_bytes=VMEM_LIMIT_BYTES)


def _rms(x, g):
    ms = jnp.mean(x * x, axis=-1, keepdims=True)
    return x * lax.rsqrt(ms + EPS) * g


def _dot(a, b):
    return jnp.dot(a, b, preferred_element_type=F32)


def _dot_nt(a, b):
    return lax.dot_general(a, b, (((1,), (1,)), ((), ())), preferred_element_type=F32)


def _dot_tn(a, b):
    return lax.dot_general(a, b, (((0,), (0,)), ((), ())), preferred_element_type=F32)


def _norm_matmul_kernel(x_hbm, g_ref, w_ref, hg_ref, o_ref, h_ref, xbuf, xsem, *,
                        head_norm_blocks, head_dim, row_split):
    i, j = pl.program_id(0), pl.program_id(1)
    bm = xbuf.shape[0]

    def x_copy(row_block):
        return pltpu.make_async_copy(x_hbm.at[pl.ds(row_block * bm, bm), :], xbuf, xsem)

    @pl.when((i == 0) & (j == 0))
    def _():
        x_copy(0).start()

    def column_block(first, head_norm):
        if first:
            x_copy(i).wait()
            h_ref[...] = _rms(xbuf[...], g_ref[...]).astype(BF16)
        for r0 in range(0, bm, bm // row_split):
            rows = slice(r0, r0 + bm // row_split)
            acc = _dot(h_ref[rows, :], w_ref[...])
            if head_norm:
                for c in range(acc.shape[1] // head_dim):
                    sl = slice(c * head_dim, (c + 1) * head_dim)
                    o_ref[rows, sl] = _rms(acc[:, sl], hg_ref[:, sl]).astype(o_ref.dtype)
            else:
                o_ref[rows, :] = acc.astype(o_ref.dtype)

    pl.when(j == 0)(functools.partial(column_block, True, head_norm_blocks > 0))
    if head_norm_blocks > 1:
        pl.when((j > 0) & (j < head_norm_blocks))(functools.partial(column_block, False, True))
    pl.when(j >= max(head_norm_blocks, 1))(functools.partial(column_block, False, False))

    @pl.when((j == 1) & (i + 1 < pl.num_programs(0)))
    def _():
        x_copy(i + 1).start()


def norm_matmul(x, g, w, layer, head_gain=None, head_norm_cols=0, head_dim=128, bm=2048, bn=1024):
    M, D = x.shape
    N = w.shape[2]
    bm = min(bm, M)
    assert M % bm == 0 and N % bn == 0 and head_norm_cols % bn == 0
    assert N // bn >= 2
    if head_gain is None:
        head_gain = jnp.ones((N,), F32)
    kern = functools.partial(_norm_matmul_kernel, head_norm_blocks=head_norm_cols // bn,
                             head_dim=head_dim, row_split=2 if bm % 32 == 0 else 1)
    return pl.pallas_call(
        kern,
        grid=(M // bm, N // bn),
        in_specs=[
            pl.BlockSpec(memory_space=pl.ANY),
            pl.BlockSpec((1, D), lambda i, j: (0, 0)),
            pl.BlockSpec((None, D, bn), lambda i, j: (layer, 0, j)),
            pl.BlockSpec((1, bn), lambda i, j: (0, j)),
        ],
        out_specs=pl.BlockSpec((bm, bn), lambda i, j: (i, j)),
        out_shape=jax.ShapeDtypeStruct((M, N), BF16),
        scratch_shapes=[pltpu.VMEM((bm, D), BF16), pltpu.VMEM((bm, D), F32),
                        pltpu.SemaphoreType.DMA(())],
        compiler_params=_cparams("arbitrary", "arbitrary"),
        name="norm_matmul",
    )(x, g.reshape(1, D), w, head_gain.reshape(1, N))


def _matmul_residual_kernel(a_ref, w_ref, x_ref, o_ref):
    o_ref[...] = x_ref[...] + _dot(a_ref[...], w_ref[...])


def matmul_residual(a, w, layer, x, bm=512, bn=2048):
    M, K = a.shape
    N = w.shape[2]
    bm = min(bm, M)
    assert M % bm == 0 and N % bn == 0
    return pl.pallas_call(
        _matmul_residual_kernel,
        grid=(M // bm, N // bn),
        in_specs=[
            pl.BlockSpec((bm, K), lambda i, j: (i, 0)),
            pl.BlockSpec((None, K, bn), lambda i, j: (layer, 0, j)),
            pl.BlockSpec((bm, bn), lambda i, j: (i, j)),
        ],
        out_specs=pl.BlockSpec((bm, bn), lambda i, j: (i, j)),
        out_shape=jax.ShapeDtypeStruct((M, N), F32),
        compiler_params=_cparams("parallel", "arbitrary"),
        name="matmul_residual",
    )(a, w, x)


def _ffn_kernel(x_hbm, g_ref, wg_ref, wu_ref, wd_ref, o_ref, h_ref, xbuf, xsem):
    i, j = pl.program_id(0), pl.program_id(1)
    bm = xbuf.shape[0]

    def x_copy(row_block):
        return pltpu.make_async_copy(x_hbm.at[pl.ds(row_block * bm, bm), :], xbuf, xsem)

    @pl.when((i == 0) & (j == 0))
    def _():
        x_copy(0).start()

    def hidden_chunk(first):
        if first:
            x_copy(i).wait()
            h_ref[...] = _rms(xbuf[...], g_ref[...]).astype(BF16)
        h = h_ref[...]
        gate = _dot(h, wg_ref[...])
        up = _dot(h, wu_ref[...])
        act = (gate * jax.nn.sigmoid(gate) * up).astype(BF16)
        down = _dot(act, wd_ref[...])
        if first:
            o_ref[...] = xbuf[...] + down
        else:
            o_ref[...] += down

    pl.when(j == 0)(functools.partial(hidden_chunk, True))
    pl.when(j > 0)(functools.partial(hidden_chunk, False))

    @pl.when((j == 1) & (i + 1 < pl.num_programs(0)))
    def _():
        x_copy(i + 1).start()


def ffn(x, g, w_gu, w_down, layer, bm=1024, bh=512):
    M, D = x.shape
    FH = w_down.shape[1]
    bm = min(bm, M)
    assert M % bm == 0 and FH % bh == 0
    nh = FH // bh
    assert nh >= 2
    return pl.pallas_call(
        _ffn_kernel,
        grid=(M // bm, nh),
        in_specs=[
            pl.BlockSpec(memory_space=pl.ANY),
            pl.BlockSpec((1, D), lambda i, j: (0, 0)),
            pl.BlockSpec((None, D, bh), lambda i, j: (layer, 0, j)),
            pl.BlockSpec((None, D, bh), lambda i, j: (layer, 0, j + nh)),
            pl.BlockSpec((None, bh, D), lambda i, j: (layer, j, 0)),
        ],
        out_specs=pl.BlockSpec((bm, D), lambda i, j: (i, 0)),
        out_shape=jax.ShapeDtypeStruct((M, D), F32),
        scratch_shapes=[pltpu.VMEM((bm, D), BF16), pltpu.VMEM((bm, D), F32),
                        pltpu.SemaphoreType.DMA(())],
        compiler_params=_cparams("arbitrary", "arbitrary"),
        name="ffn",
    )(x, g.reshape(1, D), w_gu, w_gu, w_down)


def _block_rows(ref, rows, span):
    return jnp.concatenate(
        [jnp.broadcast_to(ref[r:r + 1, :], (span, ref.shape[1])) for r in rows], axis=0)


def _gla_kernel(q_ref, k_ref, v_ref, r_ref, a_ref, wa_ref, ba_ref, gn_ref, o_ref, st_ref,
                b_ref, kf_ref, *, chunk, diag, scale, hk, hv, heads):
    C = chunk

    @pl.when(pl.program_id(2) == 0)
    def _():
        st_ref[...] = jnp.zeros_like(st_ref)

    row = lax.broadcasted_iota(jnp.int32, (C, C), 0)
    col = lax.broadcasted_iota(jnp.int32, (C, C), 1)
    rix = lax.broadcasted_iota(jnp.int32, (C, 1), 0)
    dcol = lax.broadcasted_iota(jnp.int32, (hk, C), 1)
    tri = (row >= col).astype(BF16)

    chains = range(q_ref.shape[0] * heads)
    cols = [(slice((n % heads) * hk, (n % heads + 1) * hk),
             slice((n % heads) * hv, (n % heads + 1) * hv)) for n in chains]
    bbs = [n // heads for n in chains]
    qs, ks_f, bs_ = {}, {}, {}
    for n in chains:
        ks_, _ = cols[n]
        qs[n] = q_ref[bbs[n], :, ks_].astype(F32) * scale
        ks_f[n] = k_ref[bbs[n], :, ks_].astype(F32)
        z = _dot(a_ref[bbs[n]], wa_ref[:, ks_]) + ba_ref[:, ks_]
        log_a = (jnp.minimum(z, 0.0) - jnp.log(1.0 + jnp.exp(-jnp.abs(z)))) * (LOG2_E / GLA_TAU)
        a_hi = log_a.astype(BF16)
        rem = log_a - a_hi.astype(F32)
        a_mid = rem.astype(BF16)
        a_lo = (rem - a_mid.astype(F32)).astype(BF16)
        b = _dot(tri, a_hi) + _dot(tri, a_mid) + _dot(tri, a_lo)
        bs_[n] = b
        b_ref[n] = b
        kf_ref[n] = ks_f[n]

    outs, scores = {}, {}
    for n in chains:
        outs[n] = _dot_nt((qs[n] * jnp.exp2(bs_[n])).astype(BF16), st_ref[n].astype(BF16))
        scores[n] = jnp.zeros((C, C), F32)

    w = C // 2
    while w >= diag:
        blk = 2 * w
        right = (rix % blk) >= w
        keep = ((row // blk) == (col // blk)) & ((row % blk) >= w) & ((col % blk) < w)
        for n in chains:
            beta = _block_rows(b_ref.at[n], range(w, C, blk), blk)
            m = (jnp.where(right, qs[n], ks_f[n]) * jnp.exp2(-jnp.abs(bs_[n] - beta))).astype(BF16)
            scores[n] = scores[n] + jnp.where(keep, _dot_nt(m, m), 0.0)
        w //= 2

    dsum = {n: jnp.zeros((C, C), F32) for n in chains}
    for s in range(diag):
        rows = range(s, C, diag)
        sel = ((dcol % diag) == s).astype(BF16)
        for n in chains:
            bs = _block_rows(b_ref.at[n], rows, diag)
            ks = _block_rows(kf_ref.at[n], rows, diag)
            e = jnp.exp2(jnp.where((rix % diag) >= s, bs_[n] - bs, -jnp.inf))
            p = (qs[n] * ks * e).astype(BF16)
            dsum[n] = dsum[n] + _dot(p, sel)

    for n in chains:
        _, vs_ = cols[n]
        v = v_ref[bbs[n], :, vs_]
        sc = scores[n] + jnp.where((row // diag) == (col // diag), dsum[n], 0.0)
        o = outs[n] + _dot(sc.astype(BF16), v)

        b = bs_[n]
        b_last = b[C - 1:C, :]
        k_dec = (ks_f[n] * jnp.exp2(b_last - b)).astype(BF16)
        st_ref[n] = st_ref[n] * jnp.exp2(b_last) + _dot_tn(v, k_dec)

        r = r_ref[bbs[n], :, vs_].astype(F32)
        o_ref[bbs[n], :, vs_] = (_rms(o, gn_ref[...]) * (r * jax.nn.sigmoid(r))).astype(o_ref.dtype)


def gla_core(proj, w_a2p, layer, b_a, g_norm, *, dk, dv):
    B, S, _ = proj.shape
    H = GLA_HEADS
    hk, hv = dk // H, dv // H
    C = GLA_CHUNK
    HS = GLA_HEADS_PER_STEP
    NB = min(GLA_BATCH_PER_STEP, B)
    assert S % C == 0 and H % HS == 0 and B % NB == 0
    wk, wv = HS * hk, HS * hv
    k_blk0 = dk // wk
    v_blk0 = 2 * dk // wv
    r_blk0 = (2 * dk + dv) // wv
    a_blk = (2 * dk + 2 * dv) // GLA_RANK_PAD
    kern = functools.partial(_gla_kernel, chunk=C, diag=GLA_DIAG, scale=hk ** -0.5,
                             hk=hk, hv=hv, heads=HS)
    return pl.pallas_call(
        kern,
        grid=(B // NB, H // HS, S // C),
        in_specs=[
            pl.BlockSpec((NB, C, wk), lambda b, h, c: (b, c, h)),
            pl.BlockSpec((NB, C, wk), lambda b, h, c: (b, c, k_blk0 + h)),
            pl.BlockSpec((NB, C, wv), lambda b, h, c: (b, c, v_blk0 + h)),
            pl.BlockSpec((NB, C, wv), lambda b, h, c: (b, c, r_blk0 + h)),
            pl.BlockSpec((NB, C, GLA_RANK_PAD), lambda b, h, c: (b, c, a_blk)),
            pl.BlockSpec((None, GLA_RANK_PAD, wk), lambda b, h, c: (layer, 0, h)),
            pl.BlockSpec((1, wk), lambda b, h, c: (0, h)),
            pl.BlockSpec((1, hv), lambda b, h, c: (0, 0)),
        ],
        out_specs=pl.BlockSpec((NB, C, wv), lambda b, h, c: (b, c, h)),
        out_shape=jax.ShapeDtypeStruct((B, S, dv), BF16),
        scratch_shapes=[pltpu.VMEM((NB * HS, hv, hk), F32), pltpu.VMEM((NB * HS, C, hk), F32),
                        pltpu.VMEM((NB * HS, C, hk), F32)],
        compiler_params=_cparams("parallel", "parallel", "arbitrary"),
        name="gla_core",
    )(proj, proj, proj, proj, proj, w_a2p, b_a.reshape(1, dk), g_norm.reshape(1, hv))


def gla_weights(w_in, w_a2, w_out):
    dk, dv = w_a2.shape[2], w_out.shape[1]
    n_pad = 2 * dk + 2 * dv + GLA_IN_PAD
    w_in_p = jnp.pad(w_in, ((0, 0), (0, 0), (0, n_pad - w_in.shape[2]))).astype(BF16)
    w_a2p = jnp.pad(w_a2, ((0, 0), (0, GLA_RANK_PAD - w_a2.shape[1]), (0, 0))).astype(BF16)
    return w_in_p, w_a2p, w_out.astype(BF16)


def gla_mixer(x, g, weights, layer, b_a, g_norm):
    w_in_p, w_a2p, w_out = weights
    B, S, D = x.shape
    dk, dv = w_a2p.shape[2], w_out.shape[1]
    n_pad = w_in_p.shape[2]
    x2 = x.reshape(B * S, D)
    proj = norm_matmul(x2, g, w_in_p, layer, bn=n_pad // 5)
    o = gla_core(proj.reshape(B, S, n_pad), w_a2p, layer, b_a, g_norm, dk=dk, dv=dv)
    return matmul_residual(o.reshape(B * S, dv), w_out, layer, x2).reshape(B, S, D)


def _pool_kernel(x_ref, g_ref, w_ref, sc_ref, o_ref, hb_ref, *, bm, gw):
    si = pl.program_id(1)
    HL = POOL_HALO

    @pl.when(si == 0)
    def _():
        hb_ref[0:HL, :] = jnp.zeros((HL, hb_ref.shape[1]), F32)

    @pl.when(si > 0)
    def _():
        hb_ref[0:HL, :] = hb_ref[bm:bm + HL, :]

    x = x_ref[0]
    hb_ref[HL:HL + bm, :] = _rms(x, g_ref[...])

    t = si * bm + lax.broadcasted_iota(jnp.int32, (bm, 1), 0)
    for gi, win in enumerate(POOL_WINDOWS):
        cs = slice(gi * gw, (gi + 1) * gw)
        ext = hb_ref[:, cs]
        d = 1
        while d < win:
            ext = ext + pltpu.roll(ext, d, 0)
            d *= 2
        h = hb_ref[HL:HL + bm, cs]
        acc = ext[HL:, :]
        count = jnp.minimum(t + 1, win).astype(F32)
        y = (acc / count - h).astype(BF16)
        o_ref[0, :, cs] = x[:, cs] + _dot(y, w_ref[gi]) * sc_ref[:, cs]


def pool_mixer(x, g, w_pool, scale, bm=256):
    B, S, D = x.shape
    G, gw, _ = w_pool.shape
    assert S % bm == 0 and G == len(POOL_WINDOWS)
    assert all(w & (w - 1) == 0 and w <= POOL_HALO for w in POOL_WINDOWS)
    kern = functools.partial(_pool_kernel, bm=bm, gw=gw)
    return pl.pallas_call(
        kern,
        grid=(B, S // bm),
        in_specs=[
            pl.BlockSpec((1, bm, D), lambda b, s: (b, s, 0)),
            pl.BlockSpec((1, D), lambda b, s: (0, 0)),
            pl.BlockSpec((G, gw, gw), lambda b, s: (0, 0, 0)),
            pl.BlockSpec((1, D), lambda b, s: (0, 0)),
        ],
        out_specs=pl.BlockSpec((1, bm, D), lambda b, s: (b, s, 0)),
        out_shape=jax.ShapeDtypeStruct((B, S, D), F32),
        scratch_shapes=[pltpu.VMEM((POOL_HALO + bm, D), F32)],
        compiler_params=_cparams("parallel", "arbitrary"),
        name="pool_mixer",
    )(x, g.reshape(1, D), w_pool.astype(BF16), scale.reshape(1, D))


def _rel_bucket(rel):
    n = jnp.maximum(rel, 0)
    max_exact = REL_BUCKETS // 2
    nf = jnp.maximum(n, 1).astype(F32)
    large = max_exact + (jnp.log(nf / max_exact) / math.log(REL_MAX_DIST / max_exact)
                         * (REL_BUCKETS - max_exact)).astype(jnp.int32)
    large = jnp.minimum(large, REL_BUCKETS - 1)
    return jnp.where(n < max_exact, n, large)


def _diff_attn_kernel(q_ref, k_ref, v_ref, rb_ref, lam_ref, sg_ref, o_ref,
                      m_ref, l_ref, acc_ref, bias_ref, s_ref, p_ref, a_ref, *, blk, hd, lam_init):
    qi = pl.program_id(2)
    L = 2 * blk
    lanes = m_ref.shape[-1]
    units = m_ref.shape[0]
    vd = 2 * hd

    @pl.when((qi == 0) & (pl.program_id(1) == 0))
    def _():
        for u in range(units):
            for off in range(2):
                r = jnp.broadcast_to(rb_ref[u, off:off + 1, :], (blk, L))
                bias_ref[u, off] = pltpu.roll(r, L - blk + 1, 1, stride=1, stride_axis=0)[:, :blk]

    m_ref[...] = jnp.full(m_ref.shape, -jnp.inf, F32)
    l_ref[...] = jnp.zeros(l_ref.shape, F32)
    acc_ref[...] = jnp.zeros(acc_ref.shape, F32)

    def logits(j, u):
        start = pl.multiple_of(j * blk, blk)
        cols = slice(u * hd, (u + 1) * hd)
        s_ref[u] = _dot_nt(q_ref[0, :, cols], k_ref[0, pl.ds(start, blk), cols])

    def step(j, off, has_next=True):
        start = pl.multiple_of(j * blk, blk)
        for u in range(units):
            for r0 in range(0, blk, DIFF_STRIP):
                rows = slice(r0, r0 + DIFF_STRIP)
                s = s_ref[u, rows, :]
                if off is not None:
                    s = s + bias_ref[u, off, rows, :]
                m_old = m_ref[u, rows, :]
                m_new = jnp.maximum(m_old, jnp.max(s, axis=-1, keepdims=True))
                alpha = jnp.exp2(m_old - m_new)
                p = jnp.exp2(s - jnp.concatenate([m_new] * (blk // lanes), axis=1))
                psum = p[:, :lanes]
                for t in range(1, blk // lanes):
                    psum = psum + p[:, t * lanes:(t + 1) * lanes]
                l_ref[u, rows, :] = alpha * l_ref[u, rows, :] + psum
                p_ref[u, rows, :] = p.astype(BF16)
                a_ref[u, rows, :] = alpha
                m_ref[u, rows, :] = m_new
            if has_next:
                logits(j + 1, u)
            alpha = jnp.concatenate([a_ref[u]] * (vd // lanes), axis=1)
            vb = v_ref[0, pl.ds(start, blk), (u // 2) * vd:(u // 2 + 1) * vd]
            acc_ref[u] = alpha * acc_ref[u] + _dot(p_ref[u], vb)

    def far_body(j, carry):
        step(j, None)
        return carry

    for u in range(units):
        logits(0, u)
    lax.fori_loop(0, jnp.maximum(qi - 1, 0), far_body, 0)

    @pl.when(qi >= 1)
    def _():
        step(qi - 1, 1)

    step(qi, 0, has_next=False)

    lp = lam_ref[...]
    lam = (jnp.exp(jnp.sum(lp[0:1] * lp[1:2], axis=-1, keepdims=True))
           - jnp.exp(jnp.sum(lp[2:3] * lp[3:4], axis=-1, keepdims=True)) + lam_init)
    for h in range(units // 2):
        l0 = jnp.sum(l_ref[2 * h], axis=-1, keepdims=True)
        l1 = jnp.sum(l_ref[2 * h + 1], axis=-1, keepdims=True)
        o = acc_ref[2 * h] / l0 - lam * (acc_ref[2 * h + 1] / l1)
        o_ref[0, :, h * vd:(h + 1) * vd] = (_rms(o, sg_ref[...]) * (1.0 - lam_init)).astype(o_ref.dtype)


def diff_attention(proj, rel_vecs, lam_params, sub_gain, *, d_model, lam_init, blk):
    B, S, _ = proj.shape
    H = DIFF_HEADS
    HS = DIFF_HEADS_PER_STEP
    vd = d_model // H
    hd = vd // 2
    lanes = 128
    units = 2 * HS
    wd = HS * vd
    HG = H // HS
    assert S % blk == 0 and blk % lanes == 0 and H % HS == 0
    kern = functools.partial(_diff_attn_kernel, blk=blk, hd=hd, lam_init=lam_init)
    return pl.pallas_call(
        kern,
        grid=(HG, B, S // blk),
        in_specs=[
            pl.BlockSpec((1, blk, wd), lambda h, b, i: (b, i, h)),
            pl.BlockSpec((1, S, wd), lambda h, b, i: (b, 0, HG + h)),
            pl.BlockSpec((1, S, wd), lambda h, b, i: (b, 0, 2 * HG + h)),
            pl.BlockSpec((units, 2, 2 * blk), lambda h, b, i: (h, 0, 0)),
            pl.BlockSpec((4, hd), lambda h, b, i: (0, 0)),
            pl.BlockSpec((1, vd), lambda h, b, i: (0, 0)),
        ],
        out_specs=pl.BlockSpec((1, blk, wd), lambda h, b, i: (b, i, h)),
        out_shape=jax.ShapeDtypeStruct((B, S, d_model), BF16),
        scratch_shapes=[
            pltpu.VMEM((units, blk, lanes), F32),
            pltpu.VMEM((units, blk, lanes), F32),
            pltpu.VMEM((units, blk, vd), F32),
            pltpu.VMEM((units, 2, blk, blk), F32),
            pltpu.VMEM((units, blk, blk), F32),
            pltpu.VMEM((units, blk, blk), BF16),
            pltpu.VMEM((units, blk, lanes), F32),
        ],
        compiler_params=_cparams("arbitrary", "arbitrary", "arbitrary"),
        name="diff_attention",
    )(proj, proj, proj, rel_vecs, lam_params, sub_gain.reshape(1, vd))


def _rel_bias_vectors(rel_table, blk):
    H2 = rel_table.shape[1]
    far = rel_table[REL_BUCKETS - 1]
    y = jnp.arange(2 * blk)
    vecs = []
    for off in (0, 1):
        rel = off * blk + blk - 1 - y
        v = rel_table[_rel_bucket(rel)].astype(F32) - far
        vecs.append(jnp.where((rel >= 0)[:, None], v * LOG2_E, -jnp.inf).T)
    return jnp.stack(vecs, axis=1)


def diff_attn_mixer(x, g, w_in, w_out, layer, q_gain, k_gain, lam_params, sub_gain, rel_table,
                    layer_idx, blk=512):
    B, S, D = x.shape
    assert blk >= REL_MAX_DIST
    hd = D // DIFF_HEADS // 2
    lam_init = 0.8 - 0.6 * math.exp(-0.3 * layer_idx)
    n_heads2 = D // hd
    head_gain = jnp.concatenate([jnp.tile(q_gain, n_heads2) * (hd ** -0.5 * LOG2_E),
                                 jnp.tile(k_gain, n_heads2), jnp.ones((D,), F32)])
    x2 = x.reshape(B * S, D)
    proj = norm_matmul(x2, g, w_in, layer, head_gain=head_gain, head_norm_cols=2 * D, head_dim=hd)
    o = diff_attention(proj.reshape(B, S, 3 * D), _rel_bias_vectors(rel_table, blk), lam_params,
                       sub_gain, d_model=D, lam_init=lam_init, blk=blk)
    return matmul_residual(o.reshape(B * S, D), w_out, layer, x2).reshape(B, S, D)


def kernel(x, norm_g, gla_w_in, gla_w_a2, gla_b_a, gla_g_norm, gla_w_out, pool_w, pool_scale,
           diff_w_in, diff_q_gain, diff_k_gain, diff_lambda, diff_sub_gain, diff_w_out, rel_bias,
           ffn_w_gu, ffn_w_down):
    B, S, D = x.shape
    depth = norm_g.shape[0]
    gla_w = gla_weights(gla_w_in, gla_w_a2, gla_w_out)
    diff_w_in_b, diff_w_out_b = diff_w_in.astype(BF16), diff_w_out.astype(BF16)
    ffn_w_gu_b, ffn_w_down_b = ffn_w_gu.astype(BF16), ffn_w_down.astype(BF16)
    for i in range(depth):
        kind, slot = i % N_MIXERS, i // N_MIXERS
        if kind == 0:
            x = gla_mixer(x, norm_g[i, 0], gla_w, slot, gla_b_a[slot], gla_g_norm[slot])
        elif kind == 1:
            x = pool_mixer(x, norm_g[i, 0], pool_w[slot], pool_scale[slot])
        else:
            x = diff_attn_mixer(x, norm_g[i, 0], diff_w_in_b, diff_w_out_b, slot, diff_q_gain[slot],
                                diff_k_gain[slot], diff_lambda[slot], diff_sub_gain[slot],
                                rel_bias, i)
        x = ffn(x.reshape(B * S, D), norm_g[i, 1], ffn_w_gu_b, ffn_w_down_b, i).reshape(B, S, D)
    return x
```

```python
import functools
import math

import jax
import jax.numpy as jnp
from jax import lax
from jax.experimental import pallas as pl
from jax.experimental.pallas import tpu as pltpu

F32 = jnp.float32
BF16 = jnp.bfloat16

EPS = 1e-6
N_MIXERS = 3

GLA_HEADS = 4
GLA_RANK = 16
GLA_TAU = 16.0
GLA_CHUNK = 128
GLA_DIAG = 8
LOG2_E = math.log2(math.e)
GLA_RANK_PAD = 128
GLA_HEADS_PER_STEP = 4
GLA_BATCH_PER_STEP = 4
GLA_IN_PAD = 256

POOL_WINDOWS = (2, 4, 8, 16)
POOL_HALO = 16

DIFF_HEADS = 8
DIFF_STRIP = 32
DIFF_HEADS_PER_STEP = 2
REL_BUCKETS = 32
REL_MAX_DIST = 128

VMEM_LIMIT_BYTES = 56 * 1024 * 1024


def _cparams(*sem):
    return pltpu.CompilerParams(dimension_semantics=sem, vmem_limit_bytes=VMEM_LIMIT_BYTES)


def _rms(x, g):
    ms = jnp.mean(x * x, axis=-1, keepdims=True)
    return x * lax.rsqrt(ms + EPS) * g


def _dot(a, b):
    return jnp.dot(a, b, preferred_element_type=F32)


def _dot_nt(a, b):
    return lax.dot_general(a, b, (((1,), (1,)), ((), ())), preferred_element_type=F32)


def _dot_tn(a, b):
    return lax.dot_general(a, b, (((0,), (0,)), ((), ())), preferred_element_type=F32)


def _norm_matmul_kernel(x_hbm, g_ref, w_ref, hg_ref, o_ref, h_ref, xbuf, xsem, *,
                        head_norm_blocks, head_dim, row_split):
    i, j = pl.program_id(0), pl.program_id(1)
    bm = xbuf.shape[0]

    def x_copy(row_block):
        return pltpu.make_async_copy(x_hbm.at[pl.ds(row_block * bm, bm), :], xbuf, xsem)

    @pl.when((i == 0) & (j == 0))
    def _():
        x_copy(0).start()

    def column_block(first, head_norm):
        if first:
            x_copy(i).wait()
            h_ref[...] = _rms(xbuf[...], g_ref[...]).astype(BF16)
        for r0 in range(0, bm, bm // row_split):
            rows = slice(r0, r0 + bm // row_split)
            acc = _dot(h_ref[rows, :], w_ref[...])
            if head_norm:
                for c in range(acc.shape[1] // head_dim):
                    sl = slice(c * head_dim, (c + 1) * head_dim)
                    o_ref[rows, sl] = _rms(acc[:, sl], hg_ref[:, sl]).astype(o_ref.dtype)
            else:
                o_ref[rows, :] = acc.astype(o_ref.dtype)

    pl.when(j == 0)(functools.partial(column_block, True, head_norm_blocks > 0))
    if head_norm_blocks > 1:
        pl.when((j > 0) & (j < head_norm_blocks))(functools.partial(column_block, False, True))
    pl.when(j >= max(head_norm_blocks, 1))(functools.partial(column_block, False, False))

    @pl.when((j == 1) & (i + 1 < pl.num_programs(0)))
    def _():
        x_copy(i + 1).start()


def norm_matmul(x, g, w, layer, head_gain=None, head_norm_cols=0, head_dim=128, bm=2048, bn=1024):
    M, D = x.shape
    N = w.shape[2]
    bm = min(bm, M)
    assert M % bm == 0 and N % bn == 0 and head_norm_cols % bn == 0
    assert N // bn >= 2
    if head_gain is None:
        head_gain = jnp.ones((N,), F32)
    kern = functools.partial(_norm_matmul_kernel, head_norm_blocks=head_norm_cols // bn,
                             head_dim=head_dim, row_split=2 if bm % 32 == 0 else 1)
    return pl.pallas_call(
        kern,
        grid=(M // bm, N // bn),
        in_specs=[
            pl.BlockSpec(memory_space=pl.ANY),
            pl.BlockSpec((1, D), lambda i, j: (0, 0)),
            pl.BlockSpec((None, D, bn), lambda i, j: (layer, 0, j)),
            pl.BlockSpec((1, bn), lambda i, j: (0, j)),
        ],
        out_specs=pl.BlockSpec((bm, bn), lambda i, j: (i, j)),
        out_shape=jax.ShapeDtypeStruct((M, N), BF16),
        scratch_shapes=[pltpu.VMEM((bm, D), BF16), pltpu.VMEM((bm, D), F32),
                        pltpu.SemaphoreType.DMA(())],
        compiler_params=_cparams("arbitrary", "arbitrary"),
        name="norm_matmul",
    )(x, g.reshape(1, D), w, head_gain.reshape(1, N))


def _matmul_residual_kernel(a_ref, w_ref, x_ref, o_ref):
    o_ref[...] = x_ref[...] + _dot(a_ref[...], w_ref[...])


def matmul_residual(a, w, layer, x, bm=512, bn=2048):
    M, K = a.shape
    N = w.shape[2]
    bm = min(bm, M)
    assert M % bm == 0 and N % bn == 0
    return pl.pallas_call(
        _matmul_residual_kernel,
        grid=(M // bm, N // bn),
        in_specs=[
            pl.BlockSpec((bm, K), lambda i, j: (i, 0)),
            pl.BlockSpec((None, K, bn), lambda i, j: (layer, 0, j)),
            pl.BlockSpec((bm, bn), lambda i, j: (i, j)),
        ],
        out_specs=pl.BlockSpec((bm, bn), lambda i, j: (i, j)),
        out_shape=jax.ShapeDtypeStruct((M, N), F32),
        compiler_params=_cparams("parallel", "arbitrary"),
        name="matmul_residual",
    )(a, w, x)


def _ffn_kernel(x_hbm, g_ref, wg_ref, wu_ref, wd_ref, o_ref, h_ref, xbuf, xsem):
    i, j = pl.program_id(0), pl.program_id(1)
    bm = xbuf.shape[0]

    def x_copy(row_block):
        return pltpu.make_async_copy(x_hbm.at[pl.ds(row_block * bm, bm), :], xbuf, xsem)

    @pl.when((i == 0) & (j == 0))
    def _():
        x_copy(0).start()

    def hidden_chunk(first):
        if first:
            x_copy(i).wait()
            h_ref[...] = _rms(xbuf[...], g_ref[...]).astype(BF16)
        h = h_ref[...]
        gate = _dot(h, wg_ref[...])
        up = _dot(h, wu_ref[...])
        act = (gate * jax.nn.sigmoid(gate) * up).astype(BF16)
        down = _dot(act, wd_ref[...])
        if first:
            o_ref[...] = xbuf[...] + down
        else:
            o_ref[...] += down

    pl.when(j == 0)(functools.partial(hidden_chunk, True))
    pl.when(j > 0)(functools.partial(hidden_chunk, False))

    @pl.when((j == 1) & (i + 1 < pl.num_programs(0)))
    def _():
        x_copy(i + 1).start()


def ffn(x, g, w_gu, w_down, layer, bm=1024, bh=512):
    M, D = x.shape
    FH = w_down.shape[1]
    bm = min(bm, M)
    assert M % bm == 0 and FH % bh == 0
    nh = FH // bh
    assert nh >= 2
    return pl.pallas_call(
        _ffn_kernel,
        grid=(M // bm, nh),
        in_specs=[
            pl.BlockSpec(memory_space=pl.ANY),
            pl.BlockSpec((1, D), lambda i, j: (0, 0)),
            pl.BlockSpec((None, D, bh), lambda i, j: (layer, 0, j)),
            pl.BlockSpec((None, D, bh), lambda i, j: (layer, 0, j + nh)),
            pl.BlockSpec((None, bh, D), lambda i, j: (layer, j, 0)),
        ],
        out_specs=pl.BlockSpec((bm, D), lambda i, j: (i, 0)),
        out_shape=jax.ShapeDtypeStruct((M, D), F32),
        scratch_shapes=[pltpu.VMEM((bm, D), BF16), pltpu.VMEM((bm, D), F32),
                        pltpu.SemaphoreType.DMA(())],
        compiler_params=_cparams("arbitrary", "arbitrary"),
        name="ffn",
    )(x, g.reshape(1, D), w_gu, w_gu, w_down)


def _block_rows(ref, rows, span):
    return jnp.concatenate(
        [jnp.broadcast_to(ref[r:r + 1, :], (span, ref.shape[1])) for r in rows], axis=0)


def _gla_kernel(q_ref, k_ref, v_ref, r_ref, a_ref, wa_ref, ba_ref, gn_ref, o_ref, st_ref,
                b_ref, kf_ref, *, chunk, diag, scale, hk, hv, heads):
    C = chunk

    @pl.when(pl.program_id(2) == 0)
    def _():
        st_ref[...] = jnp.zeros_like(st_ref)

    row = lax.broadcasted_iota(jnp.int32, (C, C), 0)
    col = lax.broadcasted_iota(jnp.int32, (C, C), 1)
    rix = lax.broadcasted_iota(jnp.int32, (C, 1), 0)
    dcol = lax.broadcasted_iota(jnp.int32, (hk, C), 1)
    tri = (row >= col).astype(BF16)

    chains = range(q_ref.shape[0] * heads)
    cols = [(slice((n % heads) * hk, (n % heads + 1) * hk),
             slice((n % heads) * hv, (n % heads + 1) * hv)) for n in chains]
    bbs = [n // heads for n in chains]
    qs, ks_f, bs_ = {}, {}, {}
    for n in chains:
        ks_, _ = cols[n]
        qs[n] = q_ref[bbs[n], :, ks_].astype(F32) * scale
        ks_f[n] = k_ref[bbs[n], :, ks_].astype(F32)
        z = _dot(a_ref[bbs[n]], wa_ref[:, ks_]) + ba_ref[:, ks_]
        log_a = (jnp.minimum(z, 0.0) - jnp.log(1.0 + jnp.exp(-jnp.abs(z)))) * (LOG2_E / GLA_TAU)
        a_hi = log_a.astype(BF16)
        rem = log_a - a_hi.astype(F32)
        a_mid = rem.astype(BF16)
        a_lo = (rem - a_mid.astype(F32)).astype(BF16)
        b = _dot(tri, a_hi) + _dot(tri, a_mid) + _dot(tri, a_lo)
        bs_[n] = b
        b_ref[n] = b
        kf_ref[n] = ks_f[n]

    outs, scores = {}, {}
    for n in chains:
        outs[n] = _dot_nt((qs[n] * jnp.exp2(bs_[n])).astype(BF16), st_ref[n].astype(BF16))
        scores[n] = jnp.zeros((C, C), F32)

    w = C // 2
    while w >= diag:
        blk = 2 * w
        right = (rix % blk) >= w
        keep = ((row // blk) == (col // blk)) & ((row % blk) >= w) & ((col % blk) < w)
        for n in chains:
            beta = _block_rows(b_ref.at[n], range(w, C, blk), blk)
            m = (jnp.where(right, qs[n], ks_f[n]) * jnp.exp2(-jnp.abs(bs_[n] - beta))).astype(BF16)
            scores[n] = scores[n] + jnp.where(keep, _dot_nt(m, m), 0.0)
        w //= 2

    dsum = {n: jnp.zeros((C, C), F32) for n in chains}
    for s in range(diag):
        rows = range(s, C, diag)
        sel = ((dcol % diag) == s).astype(BF16)
        for n in chains:
            bs = _block_rows(b_ref.at[n], rows, diag)
            ks = _block_rows(kf_ref.at[n], rows, diag)
            e = jnp.exp2(jnp.where((rix % diag) >= s, bs_[n] - bs, -jnp.inf))
            p = (qs[n] * ks * e).astype(BF16)
            dsum[n] = dsum[n] + _dot(p, sel)

    for n in chains:
        _, vs_ = cols[n]
        v = v_ref[bbs[n], :, vs_]
        sc = scores[n] + jnp.where((row // diag) == (col // diag), dsum[n], 0.0)
        o = outs[n] + _dot(sc.astype(BF16), v)

        b = bs_[n]
        b_last = b[C - 1:C, :]
        k_dec = (ks_f[n] * jnp.exp2(b_last - b)).astype(BF16)
        st_ref[n] = st_ref[n] * jnp.exp2(b_last) + _dot_tn(v, k_dec)

        r = r_ref[bbs[n], :, vs_].astype(F32)
        o_ref[bbs[n], :, vs_] = (_rms(o, gn_ref[...]) * (r * jax.nn.sigmoid(r))).astype(o_ref.dtype)


def gla_core(proj, w_a2p, layer, b_a, g_norm, *, dk, dv):
    B, S, _ = proj.shape
    H = GLA_HEADS
    hk, hv = dk // H, dv // H
    C = GLA_CHUNK
    HS = GLA_HEADS_PER_STEP
    NB = min(GLA_BATCH_PER_STEP, B)
    assert S % C == 0 and H % HS == 0 and B % NB == 0
    wk, wv = HS * hk, HS * hv
    k_blk0 = dk // wk
    v_blk0 = 2 * dk // wv
    r_blk0 = (2 * dk + dv) // wv
    a_blk = (2 * dk + 2 * dv) // GLA_RANK_PAD
    kern = functools.partial(_gla_kernel, chunk=C, diag=GLA_DIAG, scale=hk ** -0.5,
                             hk=hk, hv=hv, heads=HS)
    return pl.pallas_call(
        kern,
        grid=(B // NB, H // HS, S // C),
        in_specs=[
            pl.BlockSpec((NB, C, wk), lambda b, h, c: (b, c, h)),
            pl.BlockSpec((NB, C, wk), lambda b, h, c: (b, c, k_blk0 + h)),
            pl.BlockSpec((NB, C, wv), lambda b, h, c: (b, c, v_blk0 + h)),
            pl.BlockSpec((NB, C, wv), lambda b, h, c: (b, c, r_blk0 + h)),
            pl.BlockSpec((NB, C, GLA_RANK_PAD), lambda b, h, c: (b, c, a_blk)),
            pl.BlockSpec((None, GLA_RANK_PAD, wk), lambda b, h, c: (layer, 0, h)),
            pl.BlockSpec((1, wk), lambda b, h, c: (0, h)),
            pl.BlockSpec((1, hv), lambda b, h, c: (0, 0)),
        ],
        out_specs=pl.BlockSpec((NB, C, wv), lambda b, h, c: (b, c, h)),
        out_shape=jax.ShapeDtypeStruct((B, S, dv), BF16),
        scratch_shapes=[pltpu.VMEM((NB * HS, hv, hk), F32), pltpu.VMEM((NB * HS, C, hk), F32),
                        pltpu.VMEM((NB * HS, C, hk), F32)],
        compiler_params=_cparams("parallel", "parallel", "arbitrary"),
        name="gla_core",
    )(proj, proj, proj, proj, proj, w_a2p, b_a.reshape(1, dk), g_norm.reshape(1, hv))


def gla_weights(w_in, w_a2, w_out):
    dk, dv = w_a2.shape[2], w_out.shape[1]
    n_pad = 2 * dk + 2 * dv + GLA_IN_PAD
    w_in_p = jnp.pad(w_in.astype(BF16), ((0, 0), (0, 0), (0, n_pad - w_in.shape[2])))
    w_a2p = jnp.pad(w_a2, ((0, 0), (0, GLA_RANK_PAD - w_a2.shape[1]), (0, 0))).astype(BF16)
    return w_in_p, w_a2p, w_out.astype(BF16)


def gla_mixer(x, g, weights, layer, b_a, g_norm):
    w_in_p, w_a2p, w_out = weights
    B, S, D = x.shape
    dk, dv = w_a2p.shape[2], w_out.shape[1]
    n_pad = w_in_p.shape[2]
    x2 = x.reshape(B * S, D)
    proj = norm_matmul(x2, g, w_in_p, layer, bn=n_pad // 5)
    o = gla_core(proj.reshape(B, S, n_pad), w_a2p, layer, b_a, g_norm, dk=dk, dv=dv)
    return matmul_residual(o.reshape(B * S, dv), w_out, layer, x2).reshape(B, S, D)


def _pool_kernel(x_ref, g_ref, w_ref, sc_ref, o_ref, hb_ref, *, bm, gw):
    si = pl.program_id(1)
    HL = POOL_HALO

    @pl.when(si == 0)
    def _():
        hb_ref[0:HL, :] = jnp.zeros((HL, hb_ref.shape[1]), F32)

    @pl.when(si > 0)
    def _():
        hb_ref[0:HL, :] = hb_ref[bm:bm + HL, :]

    x = x_ref[0]
    hb_ref[HL:HL + bm, :] = _rms(x, g_ref[...])

    t = si * bm + lax.broadcasted_iota(jnp.int32, (bm, 1), 0)
    for gi, win in enumerate(POOL_WINDOWS):
        cs = slice(gi * gw, (gi + 1) * gw)
        ext = hb_ref[:, cs]
        d = 1
        while d < win:
            ext = ext + pltpu.roll(ext, d, 0)
            d *= 2
        h = hb_ref[HL:HL + bm, cs]
        acc = ext[HL:, :]
        count = jnp.minimum(t + 1, win).astype(F32)
        y = (acc / count - h).astype(BF16)
        o_ref[0, :, cs] = x[:, cs] + _dot(y, w_ref[gi]) * sc_ref[:, cs]


def pool_mixer(x, g, w_pool, scale, bm=512):
    B, S, D = x.shape
    G, gw, _ = w_pool.shape
    assert S % bm == 0 and G == len(POOL_WINDOWS)
    assert all(w & (w - 1) == 0 and w <= POOL_HALO for w in POOL_WINDOWS)
    kern = functools.partial(_pool_kernel, bm=bm, gw=gw)
    return pl.pallas_call(
        kern,
        grid=(B, S // bm),
        in_specs=[
            pl.BlockSpec((1, bm, D), lambda b, s: (b, s, 0)),
            pl.BlockSpec((1, D), lambda b, s: (0, 0)),
            pl.BlockSpec((G, gw, gw), lambda b, s: (0, 0, 0)),
            pl.BlockSpec((1, D), lambda b, s: (0, 0)),
        ],
        out_specs=pl.BlockSpec((1, bm, D), lambda b, s: (b, s, 0)),
        out_shape=jax.ShapeDtypeStruct((B, S, D), F32),
        scratch_shapes=[pltpu.VMEM((POOL_HALO + bm, D), F32)],
        compiler_params=_cparams("parallel", "arbitrary"),
        name="pool_mixer",
    )(x, g.reshape(1, D), w_pool.astype(BF16), scale.reshape(1, D))


def _rel_bucket(rel):
    n = jnp.maximum(rel, 0)
    max_exact = REL_BUCKETS // 2
    nf = jnp.maximum(n, 1).astype(F32)
    large = max_exact + (jnp.log(nf / max_exact) / math.log(REL_MAX_DIST / max_exact)
                         * (REL_BUCKETS - max_exact)).astype(jnp.int32)
    large = jnp.minimum(large, REL_BUCKETS - 1)
    return jnp.where(n < max_exact, n, large)


def _diff_attn_kernel(q_ref, k_ref, v_ref, rb_ref, lam_ref, sg_ref, o_ref,
                      m_ref, l_ref, acc_ref, bias_ref, s_ref, p_ref, a_ref, *, blk, hd, lam_init):
    qi = pl.program_id(2)
    L = 2 * blk
    lanes = m_ref.shape[-1]
    units = m_ref.shape[0]
    vd = 2 * hd

    @pl.when((qi == 0) & (pl.program_id(1) == 0))
    def _():
        for u in range(units):
            for off in range(2):
                r = jnp.broadcast_to(rb_ref[u, off:off + 1, :], (blk, L))
                bias_ref[u, off] = pltpu.roll(r, L - blk + 1, 1, stride=1, stride_axis=0)[:, :blk]

    m_ref[...] = jnp.full(m_ref.shape, -jnp.inf, F32)
    l_ref[...] = jnp.zeros(l_ref.shape, F32)
    acc_ref[...] = jnp.zeros(acc_ref.shape, F32)

    def logits(j, u):
        start = pl.multiple_of(j * blk, blk)
        cols = slice(u * hd, (u + 1) * hd)
        s_ref[u] = _dot_nt(q_ref[0, :, cols], k_ref[0, pl.ds(start, blk), cols])

    def step(j, off, has_next=True):
        start = pl.multiple_of(j * blk, blk)
        for u in range(units):
            for r0 in range(0, blk, DIFF_STRIP):
                rows = slice(r0, r0 + DIFF_STRIP)
                s = s_ref[u, rows, :]
                if off is not None:
                    s = s + bias_ref[u, off, rows, :]
                m_old = m_ref[u, rows, :]
                m_new = jnp.maximum(m_old, jnp.max(s, axis=-1, keepdims=True))
                alpha = jnp.exp2(m_old - m_new)
                p = jnp.exp2(s - jnp.concatenate([m_new] * (blk // lanes), axis=1))
                psum = p[:, :lanes]
                for t in range(1, blk // lanes):
                    psum = psum + p[:, t * lanes:(t + 1) * lanes]
                l_ref[u, rows, :] = alpha * l_ref[u, rows, :] + psum
                p_ref[u, rows, :] = p.astype(BF16)
                a_ref[u, rows, :] = alpha
                m_ref[u, rows, :] = m_new
            if has_next:
                logits(j + 1, u)
            alpha = jnp.concatenate([a_ref[u]] * (vd // lanes), axis=1)
            vb = v_ref[0, pl.ds(start, blk), (u // 2) * vd:(u // 2 + 1) * vd]
            acc_ref[u] = alpha * acc_ref[u] + _dot(p_ref[u], vb)

    def far_body(j, carry):
        step(j, None)
        return carry

    for u in range(units):
        logits(0, u)
    lax.fori_loop(0, jnp.maximum(qi - 1, 0), far_body, 0)

    @pl.when(qi >= 1)
    def _():
        step(qi - 1, 1)

    step(qi, 0, has_next=False)

    lp = lam_ref[...]
    lam = (jnp.exp(jnp.sum(lp[0:1] * lp[1:2], axis=-1, keepdims=True))
           - jnp.exp(jnp.sum(lp[2:3] * lp[3:4], axis=-1, keepdims=True)) + lam_init)
    for h in range(units // 2):
        l0 = jnp.sum(l_ref[2 * h], axis=-1, keepdims=True)
        l1 = jnp.sum(l_ref[2 * h + 1], axis=-1, keepdims=True)
        o = acc_ref[2 * h] / l0 - lam * (acc_ref[2 * h + 1] / l1)
        o_ref[0, :, h * vd:(h + 1) * vd] = (_rms(o, sg_ref[...]) * (1.0 - lam_init)).astype(o_ref.dtype)


def diff_attention(proj, rel_vecs, lam_params, sub_gain, *, d_model, lam_init, blk):
    B, S, _ = proj.shape
    H = DIFF_HEADS
    HS = DIFF_HEADS_PER_STEP
    vd = d_model // H
    hd = vd // 2
    lanes = 128
    units = 2 * HS
    wd = HS * vd
    HG = H // HS
    assert S % blk == 0 and blk % lanes == 0 and H % HS == 0
    kern = functools.partial(_diff_attn_kernel, blk=blk, hd=hd, lam_init=lam_init)
    return pl.pallas_call(
        kern,
        grid=(HG, B, S // blk),
        in_specs=[
            pl.BlockSpec((1, blk, wd), lambda h, b, i: (b, i, h)),
            pl.BlockSpec((1, S, wd), lambda h, b, i: (b, 0, HG + h)),
            pl.BlockSpec((1, S, wd), lambda h, b, i: (b, 0, 2 * HG + h)),
            pl.BlockSpec((units, 2, 2 * blk), lambda h, b, i: (h, 0, 0)),
            pl.BlockSpec((4, hd), lambda h, b, i: (0, 0)),
            pl.BlockSpec((1, vd), lambda h, b, i: (0, 0)),
        ],
        out_specs=pl.BlockSpec((1, blk, wd), lambda h, b, i: (b, i, h)),
        out_shape=jax.ShapeDtypeStruct((B, S, d_model), BF16),
        scratch_shapes=[
            pltpu.VMEM((units, blk, lanes), F32),
            pltpu.VMEM((units, blk, lanes), F32),
            pltpu.VMEM((units, blk, vd), F32),
            pltpu.VMEM((units, 2, blk, blk), F32),
            pltpu.VMEM((units, blk, blk), F32),
            pltpu.VMEM((units, blk, blk), BF16),
            pltpu.VMEM((units, blk, lanes), F32),
        ],
        compiler_params=_cparams("arbitrary", "arbitrary", "arbitrary"),
        name="diff_attention",
    )(proj, proj, proj, rel_vecs, lam_params, sub_gain.reshape(1, vd))


def _rel_bias_vectors(rel_table, blk):
    H2 = rel_table.shape[1]
    far = rel_table[REL_BUCKETS - 1]
    y = jnp.arange(2 * blk)
    vecs = []
    for off in (0, 1):
        rel = off * blk + blk - 1 - y
        v = rel_table[_rel_bucket(rel)].astype(F32) - far
        vecs.append(jnp.where((rel >= 0)[:, None], v * LOG2_E, -jnp.inf).T)
    return jnp.stack(vecs, axis=1)


def diff_attn_mixer(x, g, w_in, w_out, layer, q_gain, k_gain, lam_params, sub_gain, rel_table,
                    layer_idx, blk=512):
    B, S, D = x.shape
    assert blk >= REL_MAX_DIST
    hd = D // DIFF_HEADS // 2
    lam_init = 0.8 - 0.6 * math.exp(-0.3 * layer_idx)
    n_heads2 = D // hd
    head_gain = jnp.concatenate([jnp.tile(q_gain, n_heads2) * (hd ** -0.5 * LOG2_E),
                                 jnp.tile(k_gain, n_heads2), jnp.ones((D,), F32)])
    x2 = x.reshape(B * S, D)
    proj = norm_matmul(x2, g, w_in, layer, head_gain=head_gain, head_norm_cols=2 * D, head_dim=hd)
    o = diff_attention(proj.reshape(B, S, 3 * D), _rel_bias_vectors(rel_table, blk), lam_params,
                       sub_gain, d_model=D, lam_init=lam_init, blk=blk)
    return matmul_residual(o.reshape(B * S, D), w_out, layer, x2).reshape(B, S, D)


def kernel(x, norm_g, gla_w_in, gla_w_a2, gla_b_a, gla_g_norm, gla_w_out, pool_w, pool_scale,
           diff_w_in, diff_q_gain, diff_k_gain, diff_lambda, diff_sub_gain, diff_w_out, rel_bias,
           ffn_w_gu, ffn_w_down):
    B, S, D = x.shape
    depth = norm_g.shape[0]
    gla_w = gla_weights(gla_w_in, gla_w_a2, gla_w_out)
    diff_w_in_b, diff_w_out_b = diff_w_in.astype(BF16), diff_w_out.astype(BF16)
    ffn_w_gu_b, ffn_w_down_b = ffn_w_gu.astype(BF16), ffn_w_down.astype(BF16)
    for i in range(depth):
        kind, slot = i % N_MIXERS, i // N_MIXERS
        if kind == 0:
            x = gla_mixer(x, norm_g[i, 0], gla_w, slot, gla_b_a[slot], gla_g_norm[slot])
        elif kind == 1:
            x = pool_mixer(x, norm_g[i, 0], pool_w[slot], pool_scale[slot])
        else:
            x = diff_attn_mixer(x, norm_g[i, 0], diff_w_in_b, diff_w_out_b, slot, diff_q_gain[slot],
                                diff_k_gain[slot], diff_lambda[slot], diff_sub_gain[slot],
                                rel_bias, i)
        x = ffn(x.reshape(B * S, D), norm_g[i, 1], ffn_w_gu_b, ffn_w_down_b, i).reshape(B, S, D)
    return x
```

```python
import functools
import math

import jax
import jax.numpy as jnp
from jax import lax
from jax.experimental import pallas as pl
from jax.experimental.pallas import tpu as pltpu

F32 = jnp.float32
BF16 = jnp.bfloat16

EPS = 1e-6
N_MIXERS = 3

GLA_HEADS = 4
GLA_RANK = 16
GLA_TAU = 16.0
GLA_CHUNK = 128
GLA_DIAG = 8
LOG2_E = math.log2(math.e)
GLA_RANK_PAD = 128
GLA_HEADS_PER_STEP = 4
GLA_BATCH_PER_STEP = 4
GLA_IN_PAD = 256

POOL_WINDOWS = (2, 4, 8, 16)
POOL_HALO = 16

DIFF_HEADS = 8
DIFF_STRIP = 32
DIFF_HEADS_PER_STEP = 2
REL_BUCKETS = 32
REL_MAX_DIST = 128

VMEM_LIMIT_BYTES = 56 * 1024 * 1024


def _cparams(*sem):
    return pltpu.CompilerParams(dimension_semantics=sem, vmem_limit_bytes=VMEM_LIMIT_BYTES)


def _rms(x, g):
    ms = jnp.mean(x * x, axis=-1, keepdims=True)
    return x * lax.rsqrt(ms + EPS) * g


def _dot(a, b):
    return jnp.dot(a, b, preferred_element_type=F32)


def _dot_nt(a, b):
    return lax.dot_general(a, b, (((1,), (1,)), ((), ())), preferred_element_type=F32)


def _dot_tn(a, b):
    return lax.dot_general(a, b, (((0,), (0,)), ((), ())), preferred_element_type=F32)


def _norm_matmul_kernel(x_hbm, g_ref, w_ref, hg_ref, o_ref, h_ref, xbuf, xsem, *,
                        head_norm_blocks, head_dim, row_split):
    i, j = pl.program_id(0), pl.program_id(1)
    bm = xbuf.shape[0]

    def x_copy(row_block):
        return pltpu.make_async_copy(x_hbm.at[pl.ds(row_block * bm, bm), :], xbuf, xsem)

    @pl.when((i == 0) & (j == 0))
    def _():
        x_copy(0).start()

    def column_block(first, head_norm):
        if first:
            x_copy(i).wait()
            h_ref[...] = _rms(xbuf[...], g_ref[...]).astype(BF16)
        for r0 in range(0, bm, bm // row_split):
            rows = slice(r0, r0 + bm // row_split)
            acc = _dot(h_ref[rows, :], w_ref[...])
            if head_norm:
                for c in range(acc.shape[1] // head_dim):
                    sl = slice(c * head_dim, (c + 1) * head_dim)
                    o_ref[rows, sl] = _rms(acc[:, sl], hg_ref[:, sl]).astype(o_ref.dtype)
            else:
                o_ref[rows, :] = acc.astype(o_ref.dtype)

    pl.when(j == 0)(functools.partial(column_block, True, head_norm_blocks > 0))
    if head_norm_blocks > 1:
        pl.when((j > 0) & (j < head_norm_blocks))(functools.partial(column_block, False, True))
    pl.when(j >= max(head_norm_blocks, 1))(functools.partial(column_block, False, False))

    @pl.when((j == 1) & (i + 1 < pl.num_programs(0)))
    def _():
        x_copy(i + 1).start()


def norm_matmul(x, g, w, layer, head_gain=None, head_norm_cols=0, head_dim=128, bm=2048, bn=1024):
    M, D = x.shape
    N = w.shape[2]
    bm = min(bm, M)
    assert M % bm == 0 and N % bn == 0 and head_norm_cols % bn == 0
    assert N // bn >= 2
    if head_gain is None:
        head_gain = jnp.ones((N,), F32)
    kern = functools.partial(_norm_matmul_kernel, head_norm_blocks=head_norm_cols // bn,
                             head_dim=head_dim, row_split=2 if bm % 32 == 0 else 1)
    return pl.pallas_call(
        kern,
        grid=(M // bm, N // bn),
        in_specs=[
            pl.BlockSpec(memory_space=pl.ANY),
            pl.BlockSpec((1, D), lambda i, j: (0, 0)),
            pl.BlockSpec((None, D, bn), lambda i, j: (layer, 0, j)),
            pl.BlockSpec((1, bn), lambda i, j: (0, j)),
        ],
        out_specs=pl.BlockSpec((bm, bn), lambda i, j: (i, j)),
        out_shape=jax.ShapeDtypeStruct((M, N), BF16),
        scratch_shapes=[pltpu.VMEM((bm, D), BF16), pltpu.VMEM((bm, D), F32),
                        pltpu.SemaphoreType.DMA(())],
        compiler_params=_cparams("arbitrary", "arbitrary"),
        name="norm_matmul",
    )(x, g.reshape(1, D), w, head_gain.reshape(1, N))


def _matmul_residual_kernel(a_ref, w_ref, x_ref, o_ref):
    o_ref[...] = x_ref[...] + _dot(a_ref[...], w_ref[...])


def matmul_residual(a, w, layer, x, bm=512, bn=2048):
    M, K = a.shape
    N = w.shape[2]
    bm = min(bm, M)
    assert M % bm == 0 and N % bn == 0
    return pl.pallas_call(
        _matmul_residual_kernel,
        grid=(M // bm, N // bn),
        in_specs=[
            pl.BlockSpec((bm, K), lambda i, j: (i, 0)),
            pl.BlockSpec((None, K, bn), lambda i, j: (layer, 0, j)),
            pl.BlockSpec((bm, bn), lambda i, j: (i, j)),
        ],
        out_specs=pl.BlockSpec((bm, bn), lambda i, j: (i, j)),
        out_shape=jax.ShapeDtypeStruct((M, N), F32),
        compiler_params=_cparams("parallel", "arbitrary"),
        name="matmul_residual",
    )(a, w, x)


def _ffn_kernel(x_hbm, g_ref, wg_ref, wu_ref, wd_ref, *rest, cast_next):
    if cast_next:
        ngu_ref, ndn_ref, o_ref, ogu_ref, odn_ref, h_ref, xbuf, xsem = rest
    else:
        o_ref, h_ref, xbuf, xsem = rest
    i, j = pl.program_id(0), pl.program_id(1)
    bm = xbuf.shape[0]

    def x_copy(row_block):
        return pltpu.make_async_copy(x_hbm.at[pl.ds(row_block * bm, bm), :], xbuf, xsem)

    @pl.when((i == 0) & (j == 0))
    def _():
        x_copy(0).start()

    def hidden_chunk(first):
        if first:
            x_copy(i).wait()
            h_ref[...] = _rms(xbuf[...], g_ref[...]).astype(BF16)
        h = h_ref[...]
        gate = _dot(h, wg_ref[...])
        up = _dot(h, wu_ref[...])
        act = (gate * jax.nn.sigmoid(gate) * up).astype(BF16)
        down = _dot(act, wd_ref[...])
        if first:
            o_ref[...] = xbuf[...] + down
        else:
            o_ref[...] += down
        if cast_next:
            ogu_ref[...] = ngu_ref[...].astype(BF16)
            odn_ref[...] = ndn_ref[...].astype(BF16)

    pl.when(j == 0)(functools.partial(hidden_chunk, True))
    pl.when(j > 0)(functools.partial(hidden_chunk, False))

    @pl.when((j == 1) & (i + 1 < pl.num_programs(0)))
    def _():
        x_copy(i + 1).start()


def ffn(x, g, w_gu, w_down, layer, next_f32=None, bm=1024, bh=512):
    M, D = x.shape
    FH = w_down.shape[1]
    bm = min(bm, M)
    assert M % bm == 0 and FH % bh == 0
    ni, nh = M // bm, FH // bh
    assert nh >= 2
    in_specs = [
        pl.BlockSpec(memory_space=pl.ANY),
        pl.BlockSpec((1, D), lambda i, j: (0, 0)),
        pl.BlockSpec((None, D, bh), lambda i, j: (layer, 0, j)),
        pl.BlockSpec((None, D, bh), lambda i, j: (layer, 0, j + nh)),
        pl.BlockSpec((None, bh, D), lambda i, j: (layer, j, 0)),
    ]
    out_specs = pl.BlockSpec((bm, D), lambda i, j: (i, 0))
    out_shape = jax.ShapeDtypeStruct((M, D), F32)
    args = (x, g.reshape(1, D), w_gu, w_gu, w_down)
    if next_f32 is not None:
        gu32, dn32, nxt = next_f32
        gu_blk = (D // ni, 2 * FH // nh)
        dn_rows = FH // (ni * nh)
        assert D % ni == 0 and (2 * FH) % nh == 0 and FH % (ni * nh) == 0
        assert gu_blk[0] % 16 == 0 and gu_blk[1] % 128 == 0 and dn_rows % 16 == 0
        in_specs += [
            pl.BlockSpec((None,) + gu_blk, lambda i, j: (nxt, i, j)),
            pl.BlockSpec((None, dn_rows, D), lambda i, j: (nxt, i * nh + j, 0)),
        ]
        out_specs = (out_specs,
                     pl.BlockSpec((None,) + gu_blk, lambda i, j: (0, i, j)),
                     pl.BlockSpec((None, dn_rows, D), lambda i, j: (0, i * nh + j, 0)))
        out_shape = (out_shape, jax.ShapeDtypeStruct((1, D, 2 * FH), BF16),
                     jax.ShapeDtypeStruct((1, FH, D), BF16))
        args = args + (gu32, dn32)
    return pl.pallas_call(
        functools.partial(_ffn_kernel, cast_next=next_f32 is not None),
        grid=(ni, nh),
        in_specs=in_specs,
        out_specs=out_specs,
        out_shape=out_shape,
        scratch_shapes=[pltpu.VMEM((bm, D), BF16), pltpu.VMEM((bm, D), F32),
                        pltpu.SemaphoreType.DMA(())],
        compiler_params=_cparams("arbitrary", "arbitrary"),
        name="ffn",
    )(*args)


def _block_rows(ref, rows, span):
    return jnp.concatenate(
        [jnp.broadcast_to(ref[r:r + 1, :], (span, ref.shape[1])) for r in rows], axis=0)


def _gla_kernel(q_ref, k_ref, v_ref, r_ref, a_ref, wa_ref, ba_ref, gn_ref, o_ref, st_ref,
                b_ref, kf_ref, *, chunk, diag, scale, hk, hv, heads):
    C = chunk

    @pl.when(pl.program_id(2) == 0)
    def _():
        st_ref[...] = jnp.zeros_like(st_ref)

    row = lax.broadcasted_iota(jnp.int32, (C, C), 0)
    col = lax.broadcasted_iota(jnp.int32, (C, C), 1)
    rix = lax.broadcasted_iota(jnp.int32, (C, 1), 0)
    dcol = lax.broadcasted_iota(jnp.int32, (hk, C), 1)
    tri = (row >= col).astype(BF16)

    chains = range(q_ref.shape[0] * heads)
    cols = [(slice((n % heads) * hk, (n % heads + 1) * hk),
             slice((n % heads) * hv, (n % heads + 1) * hv)) for n in chains]
    bbs = [n // heads for n in chains]
    qs, ks_f, bs_ = {}, {}, {}
    for n in chains:
        ks_, _ = cols[n]
        qs[n] = q_ref[bbs[n], :, ks_].astype(F32) * scale
        ks_f[n] = k_ref[bbs[n], :, ks_].astype(F32)
        z = _dot(a_ref[bbs[n]], wa_ref[:, ks_]) + ba_ref[:, ks_]
        log_a = (jnp.minimum(z, 0.0) - jnp.log(1.0 + jnp.exp(-jnp.abs(z)))) * (LOG2_E / GLA_TAU)
        a_hi = log_a.astype(BF16)
        rem = log_a - a_hi.astype(F32)
        a_mid = rem.astype(BF16)
        a_lo = (rem - a_mid.astype(F32)).astype(BF16)
        b = _dot(tri, a_hi) + _dot(tri, a_mid) + _dot(tri, a_lo)
        bs_[n] = b
        b_ref[n] = b
        kf_ref[n] = ks_f[n]

    outs, scores = {}, {}
    for n in chains:
        outs[n] = _dot_nt((qs[n] * jnp.exp2(bs_[n])).astype(BF16), st_ref[n].astype(BF16))
        scores[n] = jnp.zeros((C, C), F32)

    w = C // 2
    while w >= diag:
        blk = 2 * w
        right = (rix % blk) >= w
        keep = ((row // blk) == (col // blk)) & ((row % blk) >= w) & ((col % blk) < w)
        for n in chains:
            beta = _block_rows(b_ref.at[n], range(w, C, blk), blk)
            m = (jnp.where(right, qs[n], ks_f[n]) * jnp.exp2(-jnp.abs(bs_[n] - beta))).astype(BF16)
            scores[n] = scores[n] + jnp.where(keep, _dot_nt(m, m), 0.0)
        w //= 2

    dsum = {n: jnp.zeros((C, C), F32) for n in chains}
    for s in range(diag):
        rows = range(s, C, diag)
        sel = ((dcol % diag) == s).astype(BF16)
        for n in chains:
            bs = _block_rows(b_ref.at[n], rows, diag)
            ks = _block_rows(kf_ref.at[n], rows, diag)
            e = jnp.exp2(jnp.where((rix % diag) >= s, bs_[n] - bs, -jnp.inf))
            p = (qs[n] * ks * e).astype(BF16)
            dsum[n] = dsum[n] + _dot(p, sel)

    for n in chains:
        _, vs_ = cols[n]
        v = v_ref[bbs[n], :, vs_]
        sc = scores[n] + jnp.where((row // diag) == (col // diag), dsum[n], 0.0)
        o = outs[n] + _dot(sc.astype(BF16), v)

        b = bs_[n]
        b_last = b[C - 1:C, :]
        k_dec = (ks_f[n] * jnp.exp2(b_last - b)).astype(BF16)
        st_ref[n] = st_ref[n] * jnp.exp2(b_last) + _dot_tn(v, k_dec)

        r = r_ref[bbs[n], :, vs_].astype(F32)
        o_ref[bbs[n], :, vs_] = (_rms(o, gn_ref[...]) * (r * jax.nn.sigmoid(r))).astype(o_ref.dtype)


def gla_core(proj, w_a2p, layer, b_a, g_norm, *, dk, dv):
    B, S, _ = proj.shape
    H = GLA_HEADS
    hk, hv = dk // H, dv // H
    C = GLA_CHUNK
    HS = GLA_HEADS_PER_STEP
    NB = min(GLA_BATCH_PER_STEP, B)
    assert S % C == 0 and H % HS == 0 and B % NB == 0
    wk, wv = HS * hk, HS * hv
    k_blk0 = dk // wk
    v_blk0 = 2 * dk // wv
    r_blk0 = (2 * dk + dv) // wv
    a_blk = (2 * dk + 2 * dv) // GLA_RANK_PAD
    kern = functools.partial(_gla_kernel, chunk=C, diag=GLA_DIAG, scale=hk ** -0.5,
                             hk=hk, hv=hv, heads=HS)
    return pl.pallas_call(
        kern,
        grid=(B // NB, H // HS, S // C),
        in_specs=[
            pl.BlockSpec((NB, C, wk), lambda b, h, c: (b, c, h)),
            pl.BlockSpec((NB, C, wk), lambda b, h, c: (b, c, k_blk0 + h)),
            pl.BlockSpec((NB, C, wv), lambda b, h, c: (b, c, v_blk0 + h)),
            pl.BlockSpec((NB, C, wv), lambda b, h, c: (b, c, r_blk0 + h)),
            pl.BlockSpec((NB, C, GLA_RANK_PAD), lambda b, h, c: (b, c, a_blk)),
            pl.BlockSpec((None, GLA_RANK_PAD, wk), lambda b, h, c: (layer, 0, h)),
            pl.BlockSpec((1, wk), lambda b, h, c: (0, h)),
            pl.BlockSpec((1, hv), lambda b, h, c: (0, 0)),
        ],
        out_specs=pl.BlockSpec((NB, C, wv), lambda b, h, c: (b, c, h)),
        out_shape=jax.ShapeDtypeStruct((B, S, dv), BF16),
        scratch_shapes=[pltpu.VMEM((NB * HS, hv, hk), F32), pltpu.VMEM((NB * HS, C, hk), F32),
                        pltpu.VMEM((NB * HS, C, hk), F32)],
        compiler_params=_cparams("parallel", "parallel", "arbitrary"),
        name="gla_core",
    )(proj, proj, proj, proj, proj, w_a2p, b_a.reshape(1, dk), g_norm.reshape(1, hv))


def gla_weights(w_in, w_a2, w_out):
    dk, dv = w_a2.shape[2], w_out.shape[1]
    n_pad = 2 * dk + 2 * dv + GLA_IN_PAD
    w_in_p = jnp.pad(w_in.astype(BF16), ((0, 0), (0, 0), (0, n_pad - w_in.shape[2])))
    w_a2p = jnp.pad(w_a2, ((0, 0), (0, GLA_RANK_PAD - w_a2.shape[1]), (0, 0))).astype(BF16)
    return w_in_p, w_a2p, w_out.astype(BF16)


def gla_mixer(x, g, weights, layer, b_a, g_norm):
    w_in_p, w_a2p, w_out = weights
    B, S, D = x.shape
    dk, dv = w_a2p.shape[2], w_out.shape[1]
    n_pad = w_in_p.shape[2]
    x2 = x.reshape(B * S, D)
    proj = norm_matmul(x2, g, w_in_p, layer, bn=n_pad // 5)
    o = gla_core(proj.reshape(B, S, n_pad), w_a2p, layer, b_a, g_norm, dk=dk, dv=dv)
    return matmul_residual(o.reshape(B * S, dv), w_out, layer, x2).reshape(B, S, D)


def _pool_kernel(x_ref, g_ref, w_ref, sc_ref, o_ref, hb_ref, *, bm, gw):
    si = pl.program_id(1)
    HL = POOL_HALO

    @pl.when(si == 0)
    def _():
        hb_ref[0:HL, :] = jnp.zeros((HL, hb_ref.shape[1]), F32)

    @pl.when(si > 0)
    def _():
        hb_ref[0:HL, :] = hb_ref[bm:bm + HL, :]

    x = x_ref[0]
    hb_ref[HL:HL + bm, :] = _rms(x, g_ref[...])

    t = si * bm + lax.broadcasted_iota(jnp.int32, (bm, 1), 0)
    for gi, win in enumerate(POOL_WINDOWS):
        cs = slice(gi * gw, (gi + 1) * gw)
        ext = hb_ref[:, cs]
        d = 1
        while d < win:
            ext = ext + pltpu.roll(ext, d, 0)
            d *= 2
        h = hb_ref[HL:HL + bm, cs]
        acc = ext[HL:, :]
        count = jnp.minimum(t + 1, win).astype(F32)
        y = (acc / count - h).astype(BF16)
        o_ref[0, :, cs] = x[:, cs] + _dot(y, w_ref[gi]) * sc_ref[:, cs]


def pool_mixer(x, g, w_pool, scale, bm=512):
    B, S, D = x.shape
    G, gw, _ = w_pool.shape
    assert S % bm == 0 and G == len(POOL_WINDOWS)
    assert all(w & (w - 1) == 0 and w <= POOL_HALO for w in POOL_WINDOWS)
    kern = functools.partial(_pool_kernel, bm=bm, gw=gw)
    return pl.pallas_call(
        kern,
        grid=(B, S // bm),
        in_specs=[
            pl.BlockSpec((1, bm, D), lambda b, s: (b, s, 0)),
            pl.BlockSpec((1, D), lambda b, s: (0, 0)),
            pl.BlockSpec((G, gw, gw), lambda b, s: (0, 0, 0)),
            pl.BlockSpec((1, D), lambda b, s: (0, 0)),
        ],
        out_specs=pl.BlockSpec((1, bm, D), lambda b, s: (b, s, 0)),
        out_shape=jax.ShapeDtypeStruct((B, S, D), F32),
        scratch_shapes=[pltpu.VMEM((POOL_HALO + bm, D), F32)],
        compiler_params=_cparams("parallel", "arbitrary"),
        name="pool_mixer",
    )(x, g.reshape(1, D), w_pool.astype(BF16), scale.reshape(1, D))


def _rel_bucket(rel):
    n = jnp.maximum(rel, 0)
    max_exact = REL_BUCKETS // 2
    nf = jnp.maximum(n, 1).astype(F32)
    large = max_exact + (jnp.log(nf / max_exact) / math.log(REL_MAX_DIST / max_exact)
                         * (REL_BUCKETS - max_exact)).astype(jnp.int32)
    large = jnp.minimum(large, REL_BUCKETS - 1)
    return jnp.where(n < max_exact, n, large)


def _diff_attn_kernel(q_ref, k_ref, v_ref, rb_ref, lam_ref, sg_ref, o_ref,
                      m_ref, l_ref, acc_ref, bias_ref, s_ref, p_ref, a_ref, *, blk, hd, lam_init):
    qi = pl.program_id(2)
    L = 2 * blk
    lanes = m_ref.shape[-1]
    units = m_ref.shape[0]
    vd = 2 * hd

    @pl.when((qi == 0) & (pl.program_id(1) == 0))
    def _():
        for u in range(units):
            for off in range(2):
                r = jnp.broadcast_to(rb_ref[u, off:off + 1, :], (blk, L))
                bias_ref[u, off] = pltpu.roll(r, L - blk + 1, 1, stride=1, stride_axis=0)[:, :blk]

    m_ref[...] = jnp.full(m_ref.shape, -jnp.inf, F32)
    l_ref[...] = jnp.zeros(l_ref.shape, F32)
    acc_ref[...] = jnp.zeros(acc_ref.shape, F32)

    def logits(j, u):
        start = pl.multiple_of(j * blk, blk)
        cols = slice(u * hd, (u + 1) * hd)
        s_ref[u] = _dot_nt(q_ref[0, :, cols], k_ref[0, pl.ds(start, blk), cols])

    def step(j, off, has_next=True):
        start = pl.multiple_of(j * blk, blk)
        for u in range(units):
            for r0 in range(0, blk, DIFF_STRIP):
                rows = slice(r0, r0 + DIFF_STRIP)
                s = s_ref[u, rows, :]
                if off is not None:
                    s = s + bias_ref[u, off, rows, :]
                m_old = m_ref[u, rows, :]
                m_new = jnp.maximum(m_old, jnp.max(s, axis=-1, keepdims=True))
                alpha = jnp.exp2(m_old - m_new)
                p = jnp.exp2(s - jnp.concatenate([m_new] * (blk // lanes), axis=1))
                psum = p[:, :lanes]
                for t in range(1, blk // lanes):
                    psum = psum + p[:, t * lanes:(t + 1) * lanes]
                l_ref[u, rows, :] = alpha * l_ref[u, rows, :] + psum
                p_ref[u, rows, :] = p.astype(BF16)
                a_ref[u, rows, :] = alpha
                m_ref[u, rows, :] = m_new
            if has_next:
                logits(j + 1, u)
            alpha = jnp.concatenate([a_ref[u]] * (vd // lanes), axis=1)
            vb = v_ref[0, pl.ds(start, blk), (u // 2) * vd:(u // 2 + 1) * vd]
            acc_ref[u] = alpha * acc_ref[u] + _dot(p_ref[u], vb)

    def far_body(j, carry):
        step(j, None)
        return carry

    for u in range(units):
        logits(0, u)
    lax.fori_loop(0, jnp.maximum(qi - 1, 0), far_body, 0)

    @pl.when(qi >= 1)
    def _():
        step(qi - 1, 1)

    step(qi, 0, has_next=False)

    lp = lam_ref[...]
    lam = (jnp.exp(jnp.sum(lp[0:1] * lp[1:2], axis=-1, keepdims=True))
           - jnp.exp(jnp.sum(lp[2:3] * lp[3:4], axis=-1, keepdims=True)) + lam_init)
    for h in range(units // 2):
        l0 = jnp.sum(l_ref[2 * h], axis=-1, keepdims=True)
        l1 = jnp.sum(l_ref[2 * h + 1], axis=-1, keepdims=True)
        o = acc_ref[2 * h] / l0 - lam * (acc_ref[2 * h + 1] / l1)
        o_ref[0, :, h * vd:(h + 1) * vd] = (_rms(o, sg_ref[...]) * (1.0 - lam_init)).astype(o_ref.dtype)


def diff_attention(proj, rel_vecs, lam_params, sub_gain, *, d_model, lam_init, blk):
    B, S, _ = proj.shape
    H = DIFF_HEADS
    HS = DIFF_HEADS_PER_STEP
    vd = d_model // H
    hd = vd // 2
    lanes = 128
    units = 2 * HS
    wd = HS * vd
    HG = H // HS
    assert S % blk == 0 and blk % lanes == 0 and H % HS == 0
    kern = functools.partial(_diff_attn_kernel, blk=blk, hd=hd, lam_init=lam_init)
    return pl.pallas_call(
        kern,
        grid=(HG, B, S // blk),
        in_specs=[
            pl.BlockSpec((1, blk, wd), lambda h, b, i: (b, i, h)),
            pl.BlockSpec((1, S, wd), lambda h, b, i: (b, 0, HG + h)),
            pl.BlockSpec((1, S, wd), lambda h, b, i: (b, 0, 2 * HG + h)),
            pl.BlockSpec((units, 2, 2 * blk), lambda h, b, i: (h, 0, 0)),
            pl.BlockSpec((4, hd), lambda h, b, i: (0, 0)),
            pl.BlockSpec((1, vd), lambda h, b, i: (0, 0)),
        ],
        out_specs=pl.BlockSpec((1, blk, wd), lambda h, b, i: (b, i, h)),
        out_shape=jax.ShapeDtypeStruct((B, S, d_model), BF16),
        scratch_shapes=[
            pltpu.VMEM((units, blk, lanes), F32),
            pltpu.VMEM((units, blk, lanes), F32),
            pltpu.VMEM((units, blk, vd), F32),
            pltpu.VMEM((units, 2, blk, blk), F32),
            pltpu.VMEM((units, blk, blk), F32),
            pltpu.VMEM((units, blk, blk), BF16),
            pltpu.VMEM((units, blk, lanes), F32),
        ],
        compiler_params=_cparams("arbitrary", "arbitrary", "arbitrary"),
        name="diff_attention",
    )(proj, proj, proj, rel_vecs, lam_params, sub_gain.reshape(1, vd))


def _rel_bias_vectors(rel_table, blk):
    H2 = rel_table.shape[1]
    far = rel_table[REL_BUCKETS - 1]
    y = jnp.arange(2 * blk)
    vecs = []
    for off in (0, 1):
        rel = off * blk + blk - 1 - y
        v = rel_table[_rel_bucket(rel)].astype(F32) - far
        vecs.append(jnp.where((rel >= 0)[:, None], v * LOG2_E, -jnp.inf).T)
    return jnp.stack(vecs, axis=1)


def diff_attn_mixer(x, g, w_in, w_out, layer, q_gain, k_gain, lam_params, sub_gain, rel_table,
                    layer_idx, blk=512):
    B, S, D = x.shape
    assert blk >= REL_MAX_DIST
    hd = D // DIFF_HEADS // 2
    lam_init = 0.8 - 0.6 * math.exp(-0.3 * layer_idx)
    n_heads2 = D // hd
    head_gain = jnp.concatenate([jnp.tile(q_gain, n_heads2) * (hd ** -0.5 * LOG2_E),
                                 jnp.tile(k_gain, n_heads2), jnp.ones((D,), F32)])
    x2 = x.reshape(B * S, D)
    proj = norm_matmul(x2, g, w_in, layer, head_gain=head_gain, head_norm_cols=2 * D, head_dim=hd)
    o = diff_attention(proj.reshape(B, S, 3 * D), _rel_bias_vectors(rel_table, blk), lam_params,
                       sub_gain, d_model=D, lam_init=lam_init, blk=blk)
    return matmul_residual(o.reshape(B * S, D), w_out, layer, x2).reshape(B, S, D)


def kernel(x, norm_g, gla_w_in, gla_w_a2, gla_b_a, gla_g_norm, gla_w_out, pool_w, pool_scale,
           diff_w_in, diff_q_gain, diff_k_gain, diff_lambda, diff_sub_gain, diff_w_out, rel_bias,
           ffn_w_gu, ffn_w_down):
    B, S, D = x.shape
    depth = norm_g.shape[0]
    gla_w = gla_weights(gla_w_in, gla_w_a2, gla_w_out)
    diff_w_in_b, diff_w_out_b = diff_w_in.astype(BF16), diff_w_out.astype(BF16)
    ffn_w = (ffn_w_gu[:1].astype(BF16), ffn_w_down[:1].astype(BF16))
    for i in range(depth):
        kind, slot = i % N_MIXERS, i // N_MIXERS
        if kind == 0:
            x = gla_mixer(x, norm_g[i, 0], gla_w, slot, gla_b_a[slot], gla_g_norm[slot])
        elif kind == 1:
            x = pool_mixer(x, norm_g[i, 0], pool_w[slot], pool_scale[slot])
        else:
            x = diff_attn_mixer(x, norm_g[i, 0], diff_w_in_b, diff_w_out_b, slot, diff_q_gain[slot],
                                diff_k_gain[slot], diff_lambda[slot], diff_sub_gain[slot],
                                rel_bias, i)
        x2 = x.reshape(B * S, D)
        if i + 1 < depth:
            x2, *ffn_w_next = ffn(x2, norm_g[i, 1], ffn_w[0], ffn_w[1], 0,
                                  next_f32=(ffn_w_gu, ffn_w_down, i + 1))
            ffn_w = tuple(ffn_w_next)
        else:
            x2 = ffn(x2, norm_g[i, 1], ffn_w[0], ffn_w[1], 0)
        x = x2.reshape(B, S, D)
    return x
```

```python
import functools
import math

import jax
import jax.numpy as jnp
from jax import lax
from jax.experimental import pallas as pl
from jax.experimental.pallas import tpu as pltpu

F32 = jnp.float32
BF16 = jnp.bfloat16

EPS = 1e-6
N_MIXERS = 3

GLA_HEADS = 4
GLA_RANK = 16
GLA_TAU = 16.0
GLA_CHUNK = 128
GLA_DIAG = 8
LOG2_E = math.log2(math.e)
GLA_RANK_PAD = 128
GLA_HEADS_PER_STEP = 4
GLA_BATCH_PER_STEP = 4
GLA_IN_PAD = 256

POOL_WINDOWS = (2, 4, 8, 16)
POOL_HALO = 16

DIFF_HEADS = 8
DIFF_STRIP = 32
DIFF_HEADS_PER_STEP = 2
REL_BUCKETS = 32
REL_MAX_DIST = 128

VMEM_LIMIT_BYTES = 56 * 1024 * 1024


def _cparams(*sem):
    return pltpu.CompilerParams(dimension_semantics=sem, vmem_limit_bytes=VMEM_LIMIT_BYTES)


def _rms(x, g):
    ms = jnp.mean(x * x, axis=-1, keepdims=True)
    return x * lax.rsqrt(ms + EPS) * g


def _dot(a, b):
    return jnp.dot(a, b, preferred_element_type=F32)


def _dot_nt(a, b):
    return lax.dot_general(a, b, (((1,), (1,)), ((), ())), preferred_element_type=F32)


def _dot_tn(a, b):
    return lax.dot_general(a, b, (((0,), (0,)), ((), ())), preferred_element_type=F32)


def _norm_matmul_kernel(x_hbm, g_ref, w_ref, hg_ref, o_ref, h_ref, xbuf, xsem, *,
                        head_norm_blocks, head_dim, row_split):
    i, j = pl.program_id(0), pl.program_id(1)
    bm = xbuf.shape[0]

    def x_copy(row_block):
        return pltpu.make_async_copy(x_hbm.at[pl.ds(row_block * bm, bm), :], xbuf, xsem)

    @pl.when((i == 0) & (j == 0))
    def _():
        x_copy(0).start()

    def column_block(first, head_norm):
        if first:
            x_copy(i).wait()
            h_ref[...] = _rms(xbuf[...], g_ref[...]).astype(BF16)
        for r0 in range(0, bm, bm // row_split):
            rows = slice(r0, r0 + bm // row_split)
            acc = _dot(h_ref[rows, :], w_ref[...])
            if head_norm:
                for c in range(acc.shape[1] // head_dim):
                    sl = slice(c * head_dim, (c + 1) * head_dim)
                    o_ref[rows, sl] = _rms(acc[:, sl], hg_ref[:, sl]).astype(o_ref.dtype)
            else:
                o_ref[rows, :] = acc.astype(o_ref.dtype)

    pl.when(j == 0)(functools.partial(column_block, True, head_norm_blocks > 0))
    if head_norm_blocks > 1:
        pl.when((j > 0) & (j < head_norm_blocks))(functools.partial(column_block, False, True))
    pl.when(j >= max(head_norm_blocks, 1))(functools.partial(column_block, False, False))

    @pl.when((j == 1) & (i + 1 < pl.num_programs(0)))
    def _():
        x_copy(i + 1).start()


def norm_matmul(x, g, w, layer, head_gain=None, head_norm_cols=0, head_dim=128, bm=2048, bn=1024):
    M, D = x.shape
    N = w.shape[2]
    bm = min(bm, M)
    assert M % bm == 0 and N % bn == 0 and head_norm_cols % bn == 0
    assert N // bn >= 2
    if head_gain is None:
        head_gain = jnp.ones((N,), F32)
    kern = functools.partial(_norm_matmul_kernel, head_norm_blocks=head_norm_cols // bn,
                             head_dim=head_dim, row_split=2 if bm % 32 == 0 else 1)
    return pl.pallas_call(
        kern,
        grid=(M // bm, N // bn),
        in_specs=[
            pl.BlockSpec(memory_space=pl.ANY),
            pl.BlockSpec((1, D), lambda i, j: (0, 0)),
            pl.BlockSpec((None, D, bn), lambda i, j: (layer, 0, j)),
            pl.BlockSpec((1, bn), lambda i, j: (0, j)),
        ],
        out_specs=pl.BlockSpec((bm, bn), lambda i, j: (i, j)),
        out_shape=jax.ShapeDtypeStruct((M, N), BF16),
        scratch_shapes=[pltpu.VMEM((bm, D), BF16), pltpu.VMEM((bm, D), F32),
                        pltpu.SemaphoreType.DMA(())],
        compiler_params=_cparams("arbitrary", "arbitrary"),
        name="norm_matmul",
    )(x, g.reshape(1, D), w, head_gain.reshape(1, N))


def _matmul_residual_kernel(a_ref, w_ref, x_ref, o_ref):
    o_ref[...] = x_ref[...] + _dot(a_ref[...], w_ref[...])


def matmul_residual(a, w, layer, x, bm=512, bn=2048):
    M, K = a.shape
    N = w.shape[2]
    bm = min(bm, M)
    assert M % bm == 0 and N % bn == 0
    return pl.pallas_call(
        _matmul_residual_kernel,
        grid=(M // bm, N // bn),
        in_specs=[
            pl.BlockSpec((bm, K), lambda i, j: (i, 0)),
            pl.BlockSpec((None, K, bn), lambda i, j: (layer, 0, j)),
            pl.BlockSpec((bm, bn), lambda i, j: (i, j)),
        ],
        out_specs=pl.BlockSpec((bm, bn), lambda i, j: (i, j)),
        out_shape=jax.ShapeDtypeStruct((M, N), F32),
        compiler_params=_cparams("parallel", "arbitrary"),
        name="matmul_residual",
    )(a, w, x)


def _ffn_kernel(x_hbm, g_ref, wg_ref, wu_ref, wd_ref, *rest, cast_next):
    if cast_next:
        ngu_ref, ndn_ref, o_ref, ogu_ref, odn_ref, h_ref, xbuf, xsem = rest
    else:
        o_ref, h_ref, xbuf, xsem = rest
    i, j = pl.program_id(0), pl.program_id(1)
    bm = xbuf.shape[0]

    def x_copy(row_block):
        return pltpu.make_async_copy(x_hbm.at[pl.ds(row_block * bm, bm), :], xbuf, xsem)

    @pl.when((i == 0) & (j == 0))
    def _():
        x_copy(0).start()

    def hidden_chunk(first):
        if first:
            x_copy(i).wait()
            h_ref[...] = _rms(xbuf[...], g_ref[...]).astype(BF16)
        h = h_ref[...]
        gate = _dot(h, wg_ref[...])
        up = _dot(h, wu_ref[...])
        act = (gate * jax.nn.sigmoid(gate) * up).astype(BF16)
        down = _dot(act, wd_ref[...])
        if first:
            o_ref[...] = xbuf[...] + down
        else:
            o_ref[...] += down
        if cast_next:
            ogu_ref[...] = ngu_ref[...].astype(BF16)
            odn_ref[...] = ndn_ref[...].astype(BF16)

    pl.when(j == 0)(functools.partial(hidden_chunk, True))
    pl.when(j > 0)(functools.partial(hidden_chunk, False))

    @pl.when((j == 1) & (i + 1 < pl.num_programs(0)))
    def _():
        x_copy(i + 1).start()


def ffn(x, g, w_gu, w_down, layer, next_f32=None, bm=1024, bh=512):
    M, D = x.shape
    FH = w_down.shape[1]
    bm = min(bm, M)
    assert M % bm == 0 and FH % bh == 0
    ni, nh = M // bm, FH // bh
    assert nh >= 2
    in_specs = [
        pl.BlockSpec(memory_space=pl.ANY),
        pl.BlockSpec((1, D), lambda i, j: (0, 0)),
        pl.BlockSpec((None, D, bh), lambda i, j: (layer, 0, j)),
        pl.BlockSpec((None, D, bh), lambda i, j: (layer, 0, j + nh)),
        pl.BlockSpec((None, bh, D), lambda i, j: (layer, j, 0)),
    ]
    out_specs = pl.BlockSpec((bm, D), lambda i, j: (i, 0))
    out_shape = jax.ShapeDtypeStruct((M, D), F32)
    args = (x, g.reshape(1, D), w_gu, w_gu, w_down)
    if next_f32 is not None:
        gu32, dn32, nxt = next_f32
        gu_blk = (D // ni, 2 * FH // nh)
        dn_rows = FH // (ni * nh)
        assert D % ni == 0 and (2 * FH) % nh == 0 and FH % (ni * nh) == 0
        assert gu_blk[0] % 16 == 0 and gu_blk[1] % 128 == 0 and dn_rows % 16 == 0
        in_specs += [
            pl.BlockSpec((None,) + gu_blk, lambda i, j: (nxt, i, j)),
            pl.BlockSpec((None, dn_rows, D), lambda i, j: (nxt, i * nh + j, 0)),
        ]
        out_specs = (out_specs,
                     pl.BlockSpec((None,) + gu_blk, lambda i, j: (0, i, j)),
                     pl.BlockSpec((None, dn_rows, D), lambda i, j: (0, i * nh + j, 0)))
        out_shape = (out_shape, jax.ShapeDtypeStruct((1, D, 2 * FH), BF16),
                     jax.ShapeDtypeStruct((1, FH, D), BF16))
        args = args + (gu32, dn32)
    return pl.pallas_call(
        functools.partial(_ffn_kernel, cast_next=next_f32 is not None),
        grid=(ni, nh),
        in_specs=in_specs,
        out_specs=out_specs,
        out_shape=out_shape,
        scratch_shapes=[pltpu.VMEM((bm, D), BF16), pltpu.VMEM((bm, D), F32),
                        pltpu.SemaphoreType.DMA(())],
        compiler_params=_cparams("arbitrary", "arbitrary"),
        name="ffn",
    )(*args)


def _block_rows(ref, rows, span):
    return jnp.concatenate(
        [jnp.broadcast_to(ref[r:r + 1, :], (span, ref.shape[1])) for r in rows], axis=0)


def _gla_kernel(q_ref, k_ref, v_ref, r_ref, a_ref, wa_ref, ba_ref, gn_ref, *rest,
                chunk, diag, scale, hk, hv, heads, n_cast):
    cast_in, (o_ref,), cast_out = rest[:n_cast], rest[n_cast:n_cast + 1], rest[n_cast + 1:2 * n_cast + 1]
    st_ref, b_ref, kf_ref = rest[2 * n_cast + 1:]
    C = chunk

    @pl.when(pl.program_id(2) == 0)
    def _():
        st_ref[...] = jnp.zeros_like(st_ref)

    row = lax.broadcasted_iota(jnp.int32, (C, C), 0)
    col = lax.broadcasted_iota(jnp.int32, (C, C), 1)
    rix = lax.broadcasted_iota(jnp.int32, (C, 1), 0)
    dcol = lax.broadcasted_iota(jnp.int32, (hk, C), 1)
    tri = (row >= col).astype(BF16)

    for src, dst in zip(cast_in, cast_out):
        dst[...] = src[...].astype(BF16)

    chains = range(q_ref.shape[0] * heads)
    cols = [(slice((n % heads) * hk, (n % heads + 1) * hk),
             slice((n % heads) * hv, (n % heads + 1) * hv)) for n in chains]
    bbs = [n // heads for n in chains]
    qs, ks_f, bs_ = {}, {}, {}
    for n in chains:
        ks_, _ = cols[n]
        qs[n] = q_ref[bbs[n], :, ks_].astype(F32) * scale
        ks_f[n] = k_ref[bbs[n], :, ks_].astype(F32)
        z = _dot(a_ref[bbs[n]], wa_ref[:, ks_]) + ba_ref[:, ks_]
        log_a = (jnp.minimum(z, 0.0) - jnp.log(1.0 + jnp.exp(-jnp.abs(z)))) * (LOG2_E / GLA_TAU)
        a_hi = log_a.astype(BF16)
        rem = log_a - a_hi.astype(F32)
        a_mid = rem.astype(BF16)
        a_lo = (rem - a_mid.astype(F32)).astype(BF16)
        b = _dot(tri, a_hi) + _dot(tri, a_mid) + _dot(tri, a_lo)
        bs_[n] = b
        b_ref[n] = b
        kf_ref[n] = ks_f[n]

    outs, scores = {}, {}
    for n in chains:
        outs[n] = _dot_nt((qs[n] * jnp.exp2(bs_[n])).astype(BF16), st_ref[n].astype(BF16))
        scores[n] = jnp.zeros((C, C), F32)

    w = C // 2
    while w >= diag:
        blk = 2 * w
        right = (rix % blk) >= w
        keep = ((row // blk) == (col // blk)) & ((row % blk) >= w) & ((col % blk) < w)
        for n in chains:
            beta = _block_rows(b_ref.at[n], range(w, C, blk), blk)
            m = (jnp.where(right, qs[n], ks_f[n]) * jnp.exp2(-jnp.abs(bs_[n] - beta))).astype(BF16)
            scores[n] = scores[n] + jnp.where(keep, _dot_nt(m, m), 0.0)
        w //= 2

    dsum = {n: jnp.zeros((C, C), F32) for n in chains}
    for s in range(diag):
        rows = range(s, C, diag)
        sel = ((dcol % diag) == s).astype(BF16)
        for n in chains:
            bs = _block_rows(b_ref.at[n], rows, diag)
            ks = _block_rows(kf_ref.at[n], rows, diag)
            e = jnp.exp2(jnp.where((rix % diag) >= s, bs_[n] - bs, -jnp.inf))
            p = (qs[n] * ks * e).astype(BF16)
            dsum[n] = dsum[n] + _dot(p, sel)

    for n in chains:
        _, vs_ = cols[n]
        v = v_ref[bbs[n], :, vs_]
        sc = scores[n] + jnp.where((row // diag) == (col // diag), dsum[n], 0.0)
        o = outs[n] + _dot(sc.astype(BF16), v)

        b = bs_[n]
        b_last = b[C - 1:C, :]
        k_dec = (ks_f[n] * jnp.exp2(b_last - b)).astype(BF16)
        st_ref[n] = st_ref[n] * jnp.exp2(b_last) + _dot_tn(v, k_dec)

        r = r_ref[bbs[n], :, vs_].astype(F32)
        o_ref[bbs[n], :, vs_] = (_rms(o, gn_ref[...]) * (r * jax.nn.sigmoid(r))).astype(o_ref.dtype)


def gla_core(proj, w_a2p, layer, b_a, g_norm, *, dk, dv, cast=()):
    B, S, _ = proj.shape
    H = GLA_HEADS
    hk, hv = dk // H, dv // H
    C = GLA_CHUNK
    HS = GLA_HEADS_PER_STEP
    NB = min(GLA_BATCH_PER_STEP, B)
    assert S % C == 0 and H % HS == 0 and B % NB == 0
    wk, wv = HS * hk, HS * hv
    k_blk0 = dk // wk
    v_blk0 = 2 * dk // wv
    r_blk0 = (2 * dk + dv) // wv
    a_blk = (2 * dk + 2 * dv) // GLA_RANK_PAD
    kern = functools.partial(_gla_kernel, chunk=C, diag=GLA_DIAG, scale=hk ** -0.5,
                             hk=hk, hv=hv, heads=HS, n_cast=len(cast))
    grid = (B // NB, H // HS, S // C)
    n_steps = grid[0] * grid[1] * grid[2]
    cast_in_specs, cast_out_specs, cast_shapes = [], [], []
    for arr, lyr in cast:
        _, R, N = arr.shape
        band = R // n_steps
        assert R % n_steps == 0 and band % 16 == 0
        cast_in_specs.append(pl.BlockSpec(
            (None, band, N), lambda b, h, c, lyr=lyr: (lyr, (b * grid[1] + h) * grid[2] + c, 0)))
        cast_out_specs.append(pl.BlockSpec(
            (None, band, N), lambda b, h, c: (0, (b * grid[1] + h) * grid[2] + c, 0)))
        cast_shapes.append(jax.ShapeDtypeStruct((1, R, N), BF16))
    outs = pl.pallas_call(
        kern,
        grid=grid,
        in_specs=[
            pl.BlockSpec((NB, C, wk), lambda b, h, c: (b, c, h)),
            pl.BlockSpec((NB, C, wk), lambda b, h, c: (b, c, k_blk0 + h)),
            pl.BlockSpec((NB, C, wv), lambda b, h, c: (b, c, v_blk0 + h)),
            pl.BlockSpec((NB, C, wv), lambda b, h, c: (b, c, r_blk0 + h)),
            pl.BlockSpec((NB, C, GLA_RANK_PAD), lambda b, h, c: (b, c, a_blk)),
            pl.BlockSpec((None, GLA_RANK_PAD, wk), lambda b, h, c: (layer, 0, h)),
            pl.BlockSpec((1, wk), lambda b, h, c: (0, h)),
            pl.BlockSpec((1, hv), lambda b, h, c: (0, 0)),
        ] + cast_in_specs,
        out_specs=[pl.BlockSpec((NB, C, wv), lambda b, h, c: (b, c, h))] + cast_out_specs,
        out_shape=[jax.ShapeDtypeStruct((B, S, dv), BF16)] + cast_shapes,
        scratch_shapes=[pltpu.VMEM((NB * HS, hv, hk), F32), pltpu.VMEM((NB * HS, C, hk), F32),
                        pltpu.VMEM((NB * HS, C, hk), F32)],
        compiler_params=_cparams("arbitrary", "arbitrary", "arbitrary"),
        name="gla_core",
    )(proj, proj, proj, proj, proj, w_a2p, b_a.reshape(1, dk), g_norm.reshape(1, hv),
      *[arr for arr, _ in cast])
    return outs[0], tuple(outs[1:])


def gla_weights(w_in, w_a2, w_out):
    dk, dv = w_a2.shape[2], w_out.shape[1]
    n_pad = 2 * dk + 2 * dv + GLA_IN_PAD
    w_in_p = jnp.pad(w_in.astype(BF16), ((0, 0), (0, 0), (0, n_pad - w_in.shape[2])))
    w_a2p = jnp.pad(w_a2, ((0, 0), (0, GLA_RANK_PAD - w_a2.shape[1]), (0, 0))).astype(BF16)
    return w_in_p, w_a2p, w_out.astype(BF16)


def gla_mixer(x, g, weights, layer, b_a, g_norm, cast=()):
    w_in_p, w_a2p, w_out = weights
    B, S, D = x.shape
    dk, dv = w_a2p.shape[2], w_out.shape[1]
    n_pad = w_in_p.shape[2]
    x2 = x.reshape(B * S, D)
    proj = norm_matmul(x2, g, w_in_p, layer, bn=n_pad // 5)
    o, cast_out = gla_core(proj.reshape(B, S, n_pad), w_a2p, layer, b_a, g_norm, dk=dk, dv=dv,
                           cast=cast)
    return matmul_residual(o.reshape(B * S, dv), w_out, layer, x2).reshape(B, S, D), cast_out


def _pool_kernel(x_ref, g_ref, w_ref, sc_ref, o_ref, hb_ref, *, bm, gw):
    si = pl.program_id(1)
    HL = POOL_HALO

    @pl.when(si == 0)
    def _():
        hb_ref[0:HL, :] = jnp.zeros((HL, hb_ref.shape[1]), F32)

    @pl.when(si > 0)
    def _():
        hb_ref[0:HL, :] = hb_ref[bm:bm + HL, :]

    x = x_ref[0]
    hb_ref[HL:HL + bm, :] = _rms(x, g_ref[...])

    t = si * bm + lax.broadcasted_iota(jnp.int32, (bm, 1), 0)
    for gi, win in enumerate(POOL_WINDOWS):
        cs = slice(gi * gw, (gi + 1) * gw)
        ext = hb_ref[:, cs]
        d = 1
        while d < win:
            ext = ext + pltpu.roll(ext, d, 0)
            d *= 2
        h = hb_ref[HL:HL + bm, cs]
        acc = ext[HL:, :]
        count = jnp.minimum(t + 1, win).astype(F32)
        y = (acc / count - h).astype(BF16)
        o_ref[0, :, cs] = x[:, cs] + _dot(y, w_ref[gi]) * sc_ref[:, cs]


def pool_mixer(x, g, w_pool, scale, bm=512):
    B, S, D = x.shape
    G, gw, _ = w_pool.shape
    assert S % bm == 0 and G == len(POOL_WINDOWS)
    assert all(w & (w - 1) == 0 and w <= POOL_HALO for w in POOL_WINDOWS)
    kern = functools.partial(_pool_kernel, bm=bm, gw=gw)
    return pl.pallas_call(
        kern,
        grid=(B, S // bm),
        in_specs=[
            pl.BlockSpec((1, bm, D), lambda b, s: (b, s, 0)),
            pl.BlockSpec((1, D), lambda b, s: (0, 0)),
            pl.BlockSpec((G, gw, gw), lambda b, s: (0, 0, 0)),
            pl.BlockSpec((1, D), lambda b, s: (0, 0)),
        ],
        out_specs=pl.BlockSpec((1, bm, D), lambda b, s: (b, s, 0)),
        out_shape=jax.ShapeDtypeStruct((B, S, D), F32),
        scratch_shapes=[pltpu.VMEM((POOL_HALO + bm, D), F32)],
        compiler_params=_cparams("parallel", "arbitrary"),
        name="pool_mixer",
    )(x, g.reshape(1, D), w_pool.astype(BF16), scale.reshape(1, D))


def _rel_bucket(rel):
    n = jnp.maximum(rel, 0)
    max_exact = REL_BUCKETS // 2
    nf = jnp.maximum(n, 1).astype(F32)
    large = max_exact + (jnp.log(nf / max_exact) / math.log(REL_MAX_DIST / max_exact)
                         * (REL_BUCKETS - max_exact)).astype(jnp.int32)
    large = jnp.minimum(large, REL_BUCKETS - 1)
    return jnp.where(n < max_exact, n, large)


def _diff_attn_kernel(q_ref, k_ref, v_ref, rb_ref, lam_ref, sg_ref, o_ref,
                      m_ref, l_ref, acc_ref, bias_ref, s_ref, p_ref, a_ref, *, blk, hd, lam_init):
    qi = pl.program_id(2)
    L = 2 * blk
    lanes = m_ref.shape[-1]
    units = m_ref.shape[0]
    vd = 2 * hd

    @pl.when((qi == 0) & (pl.program_id(1) == 0))
    def _():
        for u in range(units):
            for off in range(2):
                r = jnp.broadcast_to(rb_ref[u, off:off + 1, :], (blk, L))
                bias_ref[u, off] = pltpu.roll(r, L - blk + 1, 1, stride=1, stride_axis=0)[:, :blk]

    m_ref[...] = jnp.full(m_ref.shape, -jnp.inf, F32)
    l_ref[...] = jnp.zeros(l_ref.shape, F32)
    acc_ref[...] = jnp.zeros(acc_ref.shape, F32)

    def logits(j, u):
        start = pl.multiple_of(j * blk, blk)
        cols = slice(u * hd, (u + 1) * hd)
        s_ref[u] = _dot_nt(q_ref[0, :, cols], k_ref[0, pl.ds(start, blk), cols])

    def step(j, off, has_next=True):
        start = pl.multiple_of(j * blk, blk)
        for u in range(units):
            for r0 in range(0, blk, DIFF_STRIP):
                rows = slice(r0, r0 + DIFF_STRIP)
                s = s_ref[u, rows, :]
                if off is not None:
                    s = s + bias_ref[u, off, rows, :]
                m_old = m_ref[u, rows, :]
                m_new = jnp.maximum(m_old, jnp.max(s, axis=-1, keepdims=True))
                alpha = jnp.exp2(m_old - m_new)
                p = jnp.exp2(s - jnp.concatenate([m_new] * (blk // lanes), axis=1))
                psum = p[:, :lanes]
                for t in range(1, blk // lanes):
                    psum = psum + p[:, t * lanes:(t + 1) * lanes]
                l_ref[u, rows, :] = alpha * l_ref[u, rows, :] + psum
                p_ref[u, rows, :] = p.astype(BF16)
                a_ref[u, rows, :] = alpha
                m_ref[u, rows, :] = m_new
            if has_next:
                logits(j + 1, u)
            alpha = jnp.concatenate([a_ref[u]] * (vd // lanes), axis=1)
            vb = v_ref[0, pl.ds(start, blk), (u // 2) * vd:(u // 2 + 1) * vd]
            acc_ref[u] = alpha * acc_ref[u] + _dot(p_ref[u], vb)

    def far_body(j, carry):
        step(j, None)
        return carry

    for u in range(units):
        logits(0, u)
    lax.fori_loop(0, jnp.maximum(qi - 1, 0), far_body, 0)

    @pl.when(qi >= 1)
    def _():
        step(qi - 1, 1)

    step(qi, 0, has_next=False)

    lp = lam_ref[...]
    lam = (jnp.exp(jnp.sum(lp[0:1] * lp[1:2], axis=-1, keepdims=True))
           - jnp.exp(jnp.sum(lp[2:3] * lp[3:4], axis=-1, keepdims=True)) + lam_init)
    for h in range(units // 2):
        l0 = jnp.sum(l_ref[2 * h], axis=-1, keepdims=True)
        l1 = jnp.sum(l_ref[2 * h + 1], axis=-1, keepdims=True)
        o = acc_ref[2 * h] / l0 - lam * (acc_ref[2 * h + 1] / l1)
        o_ref[0, :, h * vd:(h + 1) * vd] = (_rms(o, sg_ref[...]) * (1.0 - lam_init)).astype(o_ref.dtype)


def diff_attention(proj, rel_vecs, lam_params, sub_gain, *, d_model, lam_init, blk):
    B, S, _ = proj.shape
    H = DIFF_HEADS
    HS = DIFF_HEADS_PER_STEP
    vd = d_model // H
    hd = vd // 2
    lanes = 128
    units = 2 * HS
    wd = HS * vd
    HG = H // HS
    assert S % blk == 0 and blk % lanes == 0 and H % HS == 0
    kern = functools.partial(_diff_attn_kernel, blk=blk, hd=hd, lam_init=lam_init)
    return pl.pallas_call(
        kern,
        grid=(HG, B, S // blk),
        in_specs=[
            pl.BlockSpec((1, blk, wd), lambda h, b, i: (b, i, h)),
            pl.BlockSpec((1, S, wd), lambda h, b, i: (b, 0, HG + h)),
            pl.BlockSpec((1, S, wd), lambda h, b, i: (b, 0, 2 * HG + h)),
            pl.BlockSpec((units, 2, 2 * blk), lambda h, b, i: (h, 0, 0)),
            pl.BlockSpec((4, hd), lambda h, b, i: (0, 0)),
            pl.BlockSpec((1, vd), lambda h, b, i: (0, 0)),
        ],
        out_specs=pl.BlockSpec((1, blk, wd), lambda h, b, i: (b, i, h)),
        out_shape=jax.ShapeDtypeStruct((B, S, d_model), BF16),
        scratch_shapes=[
            pltpu.VMEM((units, blk, lanes), F32),
            pltpu.VMEM((units, blk, lanes), F32),
            pltpu.VMEM((units, blk, vd), F32),
            pltpu.VMEM((units, 2, blk, blk), F32),
            pltpu.VMEM((units, blk, blk), F32),
            pltpu.VMEM((units, blk, blk), BF16),
            pltpu.VMEM((units, blk, lanes), F32),
        ],
        compiler_params=_cparams("arbitrary", "arbitrary", "arbitrary"),
        name="diff_attention",
    )(proj, proj, proj, rel_vecs, lam_params, sub_gain.reshape(1, vd))


def _rel_bias_vectors(rel_table, blk):
    H2 = rel_table.shape[1]
    far = rel_table[REL_BUCKETS - 1]
    y = jnp.arange(2 * blk)
    vecs = []
    for off in (0, 1):
        rel = off * blk + blk - 1 - y
        v = rel_table[_rel_bucket(rel)].astype(F32) - far
        vecs.append(jnp.where((rel >= 0)[:, None], v * LOG2_E, -jnp.inf).T)
    return jnp.stack(vecs, axis=1)


def diff_attn_mixer(x, g, w_in, w_out, layer, q_gain, k_gain, lam_params, sub_gain, rel_table,
                    layer_idx, blk=512):
    B, S, D = x.shape
    assert blk >= REL_MAX_DIST
    hd = D // DIFF_HEADS // 2
    lam_init = 0.8 - 0.6 * math.exp(-0.3 * layer_idx)
    n_heads2 = D // hd
    head_gain = jnp.concatenate([jnp.tile(q_gain, n_heads2) * (hd ** -0.5 * LOG2_E),
                                 jnp.tile(k_gain, n_heads2), jnp.ones((D,), F32)])
    x2 = x.reshape(B * S, D)
    proj = norm_matmul(x2, g, w_in, layer, head_gain=head_gain, head_norm_cols=2 * D, head_dim=hd)
    o = diff_attention(proj.reshape(B, S, 3 * D), _rel_bias_vectors(rel_table, blk), lam_params,
                       sub_gain, d_model=D, lam_init=lam_init, blk=blk)
    return matmul_residual(o.reshape(B * S, D), w_out, layer, x2).reshape(B, S, D)


def kernel(x, norm_g, gla_w_in, gla_w_a2, gla_b_a, gla_g_norm, gla_w_out, pool_w, pool_scale,
           diff_w_in, diff_q_gain, diff_k_gain, diff_lambda, diff_sub_gain, diff_w_out, rel_bias,
           ffn_w_gu, ffn_w_down):
    B, S, D = x.shape
    depth = norm_g.shape[0]
    gla_w = gla_weights(gla_w_in, gla_w_a2, gla_w_out)
    n_diff = diff_w_in.shape[0]
    riders = [(ffn_w_gu, 0), (ffn_w_down, 0)]
    riders += [(diff_w_in, s) for s in range(n_diff)] + [(diff_w_out, s) for s in range(n_diff)]
    for i in range(depth):
        kind, slot = i % N_MIXERS, i // N_MIXERS
        if kind == 0:
            x, cast_out = gla_mixer(x, norm_g[i, 0], gla_w, slot, gla_b_a[slot], gla_g_norm[slot],
                                    cast=tuple(riders) if i == 0 else ())
            if i == 0:
                ffn_w = cast_out[:2]
                diff_w_in_b, diff_w_out_b = cast_out[2:2 + n_diff], cast_out[2 + n_diff:]
        elif kind == 1:
            x = pool_mixer(x, norm_g[i, 0], pool_w[slot], pool_scale[slot])
        else:
            x = diff_attn_mixer(x, norm_g[i, 0], diff_w_in_b[slot], diff_w_out_b[slot], 0,
                                diff_q_gain[slot], diff_k_gain[slot], diff_lambda[slot],
                                diff_sub_gain[slot], rel_bias, i)
        x2 = x.reshape(B * S, D)
        if i + 1 < depth:
            x2, *ffn_w_next = ffn(x2, norm_g[i, 1], ffn_w[0], ffn_w[1], 0,
                                  next_f32=(ffn_w_gu, ffn_w_down, i + 1))
            ffn_w = tuple(ffn_w_next)
        else:
            x2 = ffn(x2, norm_g[i, 1], ffn_w[0], ffn_w[1], 0)
        x = x2.reshape(B, S, D)
    return x
```

```python
import functools
import math

import jax
import jax.numpy as jnp
from jax import lax
from jax.experimental import pallas as pl
from jax.experimental.pallas import tpu as pltpu

F32 = jnp.float32
BF16 = jnp.bfloat16

EPS = 1e-6
N_MIXERS = 3

GLA_HEADS = 4
GLA_RANK = 16
GLA_TAU = 16.0
GLA_CHUNK = 128
GLA_DIAG = 8
LOG2_E = math.log2(math.e)
GLA_RANK_PAD = 128
GLA_HEADS_PER_STEP = 4
GLA_BATCH_PER_STEP = 4
GLA_IN_PAD = 256

POOL_WINDOWS = (2, 4, 8, 16)
POOL_HALO = 16

DIFF_HEADS = 8
DIFF_STRIP = 32
DIFF_HEADS_PER_STEP = 2
REL_BUCKETS = 32
REL_MAX_DIST = 128

VMEM_LIMIT_BYTES = 58 * 1024 * 1024


def _cparams(*sem):
    return pltpu.CompilerParams(dimension_semantics=sem, vmem_limit_bytes=VMEM_LIMIT_BYTES)


def _rms(x, g):
    ms = jnp.mean(x * x, axis=-1, keepdims=True)
    return x * lax.rsqrt(ms + EPS) * g


def _dot(a, b):
    return jnp.dot(a, b, preferred_element_type=F32)


def _dot_nt(a, b):
    return lax.dot_general(a, b, (((1,), (1,)), ((), ())), preferred_element_type=F32)


def _dot_tn(a, b):
    return lax.dot_general(a, b, (((0,), (0,)), ((), ())), preferred_element_type=F32)


def _norm_matmul_kernel(x_hbm, g_ref, w_ref, hg_ref, o_ref, h_ref, xbuf, xsem, *,
                        head_norm_blocks, head_dim, row_split):
    i, j = pl.program_id(0), pl.program_id(1)
    bm = xbuf.shape[0]

    def x_copy(row_block):
        return pltpu.make_async_copy(x_hbm.at[pl.ds(row_block * bm, bm), :], xbuf, xsem)

    @pl.when((i == 0) & (j == 0))
    def _():
        x_copy(0).start()

    def column_block(first, head_norm):
        if first:
            x_copy(i).wait()
            h_ref[...] = _rms(xbuf[...], g_ref[...]).astype(BF16)
        for r0 in range(0, bm, bm // row_split):
            rows = slice(r0, r0 + bm // row_split)
            acc = _dot(h_ref[rows, :], w_ref[...])
            if head_norm:
                for c in range(acc.shape[1] // head_dim):
                    sl = slice(c * head_dim, (c + 1) * head_dim)
                    o_ref[rows, sl] = _rms(acc[:, sl], hg_ref[:, sl]).astype(o_ref.dtype)
            else:
                o_ref[rows, :] = acc.astype(o_ref.dtype)

    pl.when(j == 0)(functools.partial(column_block, True, head_norm_blocks > 0))
    if head_norm_blocks > 1:
        pl.when((j > 0) & (j < head_norm_blocks))(functools.partial(column_block, False, True))
    pl.when(j >= max(head_norm_blocks, 1))(functools.partial(column_block, False, False))

    @pl.when((j == 1) & (i + 1 < pl.num_programs(0)))
    def _():
        x_copy(i + 1).start()


def norm_matmul(x, g, w, layer, head_gain=None, head_norm_cols=0, head_dim=128, bm=2048, bn=1024):
    M, D = x.shape
    N = w.shape[2]
    bm = min(bm, M)
    assert M % bm == 0 and N % bn == 0 and head_norm_cols % bn == 0
    assert N // bn >= 2
    if head_gain is None:
        head_gain = jnp.ones((N,), F32)
    kern = functools.partial(_norm_matmul_kernel, head_norm_blocks=head_norm_cols // bn,
                             head_dim=head_dim, row_split=2 if bm % 32 == 0 else 1)
    return pl.pallas_call(
        kern,
        grid=(M // bm, N // bn),
        in_specs=[
            pl.BlockSpec(memory_space=pl.ANY),
            pl.BlockSpec((1, D), lambda i, j: (0, 0)),
            pl.BlockSpec((None, D, bn), lambda i, j: (layer, 0, j)),
            pl.BlockSpec((1, bn), lambda i, j: (0, j)),
        ],
        out_specs=pl.BlockSpec((bm, bn), lambda i, j: (i, j)),
        out_shape=jax.ShapeDtypeStruct((M, N), BF16),
        scratch_shapes=[pltpu.VMEM((bm, D), BF16), pltpu.VMEM((bm, D), F32),
                        pltpu.SemaphoreType.DMA(())],
        compiler_params=_cparams("arbitrary", "arbitrary"),
        name="norm_matmul",
    )(x, g.reshape(1, D), w, head_gain.reshape(1, N))


def _matmul_residual_kernel(a_ref, w_ref, x_ref, o_ref):
    o_ref[...] = x_ref[...] + _dot(a_ref[...], w_ref[...])


def matmul_residual(a, w, layer, x, bm=512, bn=2048):
    M, K = a.shape
    N = w.shape[2]
    bm = min(bm, M)
    assert M % bm == 0 and N % bn == 0
    return pl.pallas_call(
        _matmul_residual_kernel,
        grid=(M // bm, N // bn),
        in_specs=[
            pl.BlockSpec((bm, K), lambda i, j: (i, 0)),
            pl.BlockSpec((None, K, bn), lambda i, j: (layer, 0, j)),
            pl.BlockSpec((bm, bn), lambda i, j: (i, j)),
        ],
        out_specs=pl.BlockSpec((bm, bn), lambda i, j: (i, j)),
        out_shape=jax.ShapeDtypeStruct((M, N), F32),
        compiler_params=_cparams("parallel", "arbitrary"),
        name="matmul_residual",
    )(a, w, x)


def _ffn_kernel(x_hbm, g_ref, wg_ref, wu_ref, wd_ref, *rest, cast_next):
    if cast_next:
        ngu_ref, ndn_ref, o_ref, ogu_ref, odn_ref, h_ref, xbuf, xsem = rest
    else:
        o_ref, h_ref, xbuf, xsem = rest
    i, j = pl.program_id(0), pl.program_id(1)
    bm = xbuf.shape[0]

    def x_copy(row_block):
        return pltpu.make_async_copy(x_hbm.at[pl.ds(row_block * bm, bm), :], xbuf, xsem)

    @pl.when((i == 0) & (j == 0))
    def _():
        x_copy(0).start()

    def hidden_chunk(first):
        if first:
            x_copy(i).wait()
            h_ref[...] = _rms(xbuf[...], g_ref[...]).astype(BF16)
        h = h_ref[...]
        gate = _dot(h, wg_ref[...])
        up = _dot(h, wu_ref[...])
        act = (gate * jax.nn.sigmoid(gate) * up).astype(BF16)
        down = _dot(act, wd_ref[...])
        if first:
            o_ref[...] = xbuf[...] + down
        else:
            o_ref[...] += down
        if cast_next:
            ogu_ref[...] = ngu_ref[...].astype(BF16)
            odn_ref[...] = ndn_ref[...].astype(BF16)

    pl.when(j == 0)(functools.partial(hidden_chunk, True))
    pl.when(j > 0)(functools.partial(hidden_chunk, False))

    @pl.when((j == 1) & (i + 1 < pl.num_programs(0)))
    def _():
        x_copy(i + 1).start()


def ffn(x, g, w_gu, w_down, layer, next_f32=None, bm=1024, bh=512):
    M, D = x.shape
    FH = w_down.shape[1]
    bm = min(bm, M)
    assert M % bm == 0 and FH % bh == 0
    ni, nh = M // bm, FH // bh
    assert nh >= 2
    in_specs = [
        pl.BlockSpec(memory_space=pl.ANY),
        pl.BlockSpec((1, D), lambda i, j: (0, 0)),
        pl.BlockSpec((None, D, bh), lambda i, j: (layer, 0, j)),
        pl.BlockSpec((None, D, bh), lambda i, j: (layer, 0, j + nh)),
        pl.BlockSpec((None, bh, D), lambda i, j: (layer, j, 0)),
    ]
    out_specs = pl.BlockSpec((bm, D), lambda i, j: (i, 0))
    out_shape = jax.ShapeDtypeStruct((M, D), F32)
    args = (x, g.reshape(1, D), w_gu, w_gu, w_down)
    if next_f32 is not None:
        gu32, dn32, nxt = next_f32
        gu_blk = (D // ni, 2 * FH // nh)
        dn_rows = FH // (ni * nh)
        assert D % ni == 0 and (2 * FH) % nh == 0 and FH % (ni * nh) == 0
        assert gu_blk[0] % 16 == 0 and gu_blk[1] % 128 == 0 and dn_rows % 16 == 0
        in_specs += [
            pl.BlockSpec((None,) + gu_blk, lambda i, j: (nxt, i, j)),
            pl.BlockSpec((None, dn_rows, D), lambda i, j: (nxt, i * nh + j, 0)),
        ]
        out_specs = (out_specs,
                     pl.BlockSpec((None,) + gu_blk, lambda i, j: (0, i, j)),
                     pl.BlockSpec((None, dn_rows, D), lambda i, j: (0, i * nh + j, 0)))
        out_shape = (out_shape, jax.ShapeDtypeStruct((1, D, 2 * FH), BF16),
                     jax.ShapeDtypeStruct((1, FH, D), BF16))
        args = args + (gu32, dn32)
    return pl.pallas_call(
        functools.partial(_ffn_kernel, cast_next=next_f32 is not None),
        grid=(ni, nh),
        in_specs=in_specs,
        out_specs=out_specs,
        out_shape=out_shape,
        scratch_shapes=[pltpu.VMEM((bm, D), BF16), pltpu.VMEM((bm, D), F32),
                        pltpu.SemaphoreType.DMA(())],
        compiler_params=_cparams("arbitrary", "arbitrary"),
        name="ffn",
    )(*args)


def _block_rows(ref, rows, span):
    return jnp.concatenate(
        [jnp.broadcast_to(ref[r:r + 1, :], (span, ref.shape[1])) for r in rows], axis=0)


def _gla_kernel(q_ref, k_ref, v_ref, r_ref, a_ref, wa_ref, ba_ref, gn_ref, *rest,
                chunk, diag, scale, hk, hv, heads, n_cast):
    cast_in, (o_ref,), cast_out = rest[:n_cast], rest[n_cast:n_cast + 1], rest[n_cast + 1:2 * n_cast + 1]
    st_ref, b_ref, kf_ref = rest[2 * n_cast + 1:]
    C = chunk

    @pl.when(pl.program_id(2) == 0)
    def _():
        st_ref[...] = jnp.zeros_like(st_ref)

    row = lax.broadcasted_iota(jnp.int32, (C, C), 0)
    col = lax.broadcasted_iota(jnp.int32, (C, C), 1)
    rix = lax.broadcasted_iota(jnp.int32, (C, 1), 0)
    dcol = lax.broadcasted_iota(jnp.int32, (hk, C), 1)
    tri = (row >= col).astype(BF16)

    for src, dst in zip(cast_in, cast_out):
        dst[...] = src[...].astype(BF16)

    chains = range(q_ref.shape[0] * heads)
    cols = [(slice((n % heads) * hk, (n % heads + 1) * hk),
             slice((n % heads) * hv, (n % heads + 1) * hv)) for n in chains]
    bbs = [n // heads for n in chains]
    qs, ks_f, bs_ = {}, {}, {}
    for n in chains:
        ks_, _ = cols[n]
        qs[n] = q_ref[bbs[n], :, ks_].astype(F32) * scale
        ks_f[n] = k_ref[bbs[n], :, ks_].astype(F32)
        z = _dot(a_ref[bbs[n]], wa_ref[:, ks_]) + ba_ref[:, ks_]
        log_a = (jnp.minimum(z, 0.0) - jnp.log(1.0 + jnp.exp(-jnp.abs(z)))) * (LOG2_E / GLA_TAU)
        a_hi = log_a.astype(BF16)
        rem = log_a - a_hi.astype(F32)
        a_mid = rem.astype(BF16)
        a_lo = (rem - a_mid.astype(F32)).astype(BF16)
        b = _dot(tri, a_hi) + _dot(tri, a_mid) + _dot(tri, a_lo)
        bs_[n] = b
        b_ref[n] = b
        kf_ref[n] = ks_f[n]

    outs, scores = {}, {}
    for n in chains:
        outs[n] = _dot_nt((qs[n] * jnp.exp2(bs_[n])).astype(BF16), st_ref[n].astype(BF16))
        scores[n] = jnp.zeros((C, C), F32)

    w = C // 2
    while w >= diag:
        blk = 2 * w
        right = (rix % blk) >= w
        keep = ((row // blk) == (col // blk)) & ((row % blk) >= w) & ((col % blk) < w)
        for n in chains:
            beta = _block_rows(b_ref.at[n], range(w, C, blk), blk)
            m = (jnp.where(right, qs[n], ks_f[n]) * jnp.exp2(-jnp.abs(bs_[n] - beta))).astype(BF16)
            scores[n] = scores[n] + jnp.where(keep, _dot_nt(m, m), 0.0)
        w //= 2

    dsum = {n: jnp.zeros((C, C), F32) for n in chains}
    for s in range(diag):
        rows = range(s, C, diag)
        sel = ((dcol % diag) == s).astype(BF16)
        for n in chains:
            bs = _block_rows(b_ref.at[n], rows, diag)
            ks = _block_rows(kf_ref.at[n], rows, diag)
            e = jnp.exp2(jnp.where((rix % diag) >= s, bs_[n] - bs, -jnp.inf))
            p = (qs[n] * ks * e).astype(BF16)
            dsum[n] = dsum[n] + _dot(p, sel)

    for n in chains:
        _, vs_ = cols[n]
        v = v_ref[bbs[n], :, vs_]
        sc = scores[n] + jnp.where((row // diag) == (col // diag), dsum[n], 0.0)
        o = outs[n] + _dot(sc.astype(BF16), v)

        b = bs_[n]
        b_last = b[C - 1:C, :]
        k_dec = (ks_f[n] * jnp.exp2(b_last - b)).astype(BF16)
        st_ref[n] = st_ref[n] * jnp.exp2(b_last) + _dot_tn(v, k_dec)

        r = r_ref[bbs[n], :, vs_].astype(F32)
        o_ref[bbs[n], :, vs_] = (_rms(o, gn_ref[...]) * (r * jax.nn.sigmoid(r))).astype(o_ref.dtype)


def gla_core(proj, w_a2p, layer, b_a, g_norm, *, dk, dv, cast=()):
    B, S, _ = proj.shape
    H = GLA_HEADS
    hk, hv = dk // H, dv // H
    C = GLA_CHUNK
    HS = GLA_HEADS_PER_STEP
    NB = min(GLA_BATCH_PER_STEP, B)
    assert S % C == 0 and H % HS == 0 and B % NB == 0
    wk, wv = HS * hk, HS * hv
    k_blk0 = dk // wk
    v_blk0 = 2 * dk // wv
    r_blk0 = (2 * dk + dv) // wv
    a_blk = (2 * dk + 2 * dv) // GLA_RANK_PAD
    kern = functools.partial(_gla_kernel, chunk=C, diag=GLA_DIAG, scale=hk ** -0.5,
                             hk=hk, hv=hv, heads=HS, n_cast=len(cast))
    grid = (B // NB, H // HS, S // C)
    n_steps = grid[0] * grid[1] * grid[2]
    cast_in_specs, cast_out_specs, cast_shapes = [], [], []
    for arr, lyr in cast:
        _, R, N = arr.shape
        band = R // n_steps
        assert R % n_steps == 0 and band % 16 == 0
        cast_in_specs.append(pl.BlockSpec(
            (None, band, N), lambda b, h, c, lyr=lyr: (lyr, (b * grid[1] + h) * grid[2] + c, 0)))
        cast_out_specs.append(pl.BlockSpec(
            (None, band, N), lambda b, h, c: (0, (b * grid[1] + h) * grid[2] + c, 0)))
        cast_shapes.append(jax.ShapeDtypeStruct((1, R, N), BF16))
    outs = pl.pallas_call(
        kern,
        grid=grid,
        in_specs=[
            pl.BlockSpec((NB, C, wk), lambda b, h, c: (b, c, h)),
            pl.BlockSpec((NB, C, wk), lambda b, h, c: (b, c, k_blk0 + h)),
            pl.BlockSpec((NB, C, wv), lambda b, h, c: (b, c, v_blk0 + h)),
            pl.BlockSpec((NB, C, wv), lambda b, h, c: (b, c, r_blk0 + h)),
            pl.BlockSpec((NB, C, GLA_RANK_PAD), lambda b, h, c: (b, c, a_blk)),
            pl.BlockSpec((None, GLA_RANK_PAD, wk), lambda b, h, c: (layer, 0, h)),
            pl.BlockSpec((1, wk), lambda b, h, c: (0, h)),
            pl.BlockSpec((1, hv), lambda b, h, c: (0, 0)),
        ] + cast_in_specs,
        out_specs=[pl.BlockSpec((NB, C, wv), lambda b, h, c: (b, c, h))] + cast_out_specs,
        out_shape=[jax.ShapeDtypeStruct((B, S, dv), BF16)] + cast_shapes,
        scratch_shapes=[pltpu.VMEM((NB * HS, hv, hk), F32), pltpu.VMEM((NB * HS, C, hk), F32),
                        pltpu.VMEM((NB * HS, C, hk), F32)],
        compiler_params=_cparams("arbitrary", "arbitrary", "arbitrary"),
        name="gla_core",
    )(proj, proj, proj, proj, proj, w_a2p, b_a.reshape(1, dk), g_norm.reshape(1, hv),
      *[arr for arr, _ in cast])
    return outs[0], tuple(outs[1:])


def gla_weights(w_in, w_a2, dv):
    n_pad = 2 * w_a2.shape[2] + 2 * dv + GLA_IN_PAD
    w_in_p = jnp.pad(w_in.astype(BF16), ((0, 0), (0, 0), (0, n_pad - w_in.shape[2])))
    w_a2p = jnp.pad(w_a2, ((0, 0), (0, GLA_RANK_PAD - w_a2.shape[1]), (0, 0))).astype(BF16)
    return w_in_p, w_a2p


def gla_mixer(x, g, weights, layer, b_a, g_norm, w_out=None, cast=(), w_out_cast=None):
    w_in_p, w_a2p = weights
    B, S, D = x.shape
    dk = w_a2p.shape[2]
    n_pad = w_in_p.shape[2]
    dv = (n_pad - GLA_IN_PAD - 2 * dk) // 2
    x2 = x.reshape(B * S, D)
    proj = norm_matmul(x2, g, w_in_p, layer, bn=n_pad // 5)
    o, cast_out = gla_core(proj.reshape(B, S, n_pad), w_a2p, layer, b_a, g_norm, dk=dk, dv=dv,
                           cast=cast)
    if w_out_cast is not None:
        w_out = cast_out[w_out_cast]
    return matmul_residual(o.reshape(B * S, dv), w_out, 0, x2).reshape(B, S, D), cast_out


def _pool_kernel(x_ref, g_ref, w_ref, sc_ref, o_ref, hb_ref, *, bm, gw):
    si = pl.program_id(1)
    HL = POOL_HALO

    @pl.when(si == 0)
    def _():
        hb_ref[0:HL, :] = jnp.zeros((HL, hb_ref.shape[1]), F32)

    @pl.when(si > 0)
    def _():
        hb_ref[0:HL, :] = hb_ref[bm:bm + HL, :]

    x = x_ref[0]
    hb_ref[HL:HL + bm, :] = _rms(x, g_ref[...])

    t = si * bm + lax.broadcasted_iota(jnp.int32, (bm, 1), 0)
    for gi, win in enumerate(POOL_WINDOWS):
        cs = slice(gi * gw, (gi + 1) * gw)
        ext = hb_ref[:, cs]
        d = 1
        while d < win:
            ext = ext + pltpu.roll(ext, d, 0)
            d *= 2
        h = hb_ref[HL:HL + bm, cs]
        acc = ext[HL:, :]
        count = jnp.minimum(t + 1, win).astype(F32)
        y = (acc / count - h).astype(BF16)
        o_ref[0, :, cs] = x[:, cs] + _dot(y, w_ref[gi]) * sc_ref[:, cs]


def pool_mixer(x, g, w_pool, scale, bm=512):
    B, S, D = x.shape
    G, gw, _ = w_pool.shape
    assert S % bm == 0 and G == len(POOL_WINDOWS)
    assert all(w & (w - 1) == 0 and w <= POOL_HALO for w in POOL_WINDOWS)
    kern = functools.partial(_pool_kernel, bm=bm, gw=gw)
    return pl.pallas_call(
        kern,
        grid=(B, S // bm),
        in_specs=[
            pl.BlockSpec((1, bm, D), lambda b, s: (b, s, 0)),
            pl.BlockSpec((1, D), lambda b, s: (0, 0)),
            pl.BlockSpec((G, gw, gw), lambda b, s: (0, 0, 0)),
            pl.BlockSpec((1, D), lambda b, s: (0, 0)),
        ],
        out_specs=pl.BlockSpec((1, bm, D), lambda b, s: (b, s, 0)),
        out_shape=jax.ShapeDtypeStruct((B, S, D), F32),
        scratch_shapes=[pltpu.VMEM((POOL_HALO + bm, D), F32)],
        compiler_params=_cparams("parallel", "arbitrary"),
        name="pool_mixer",
    )(x, g.reshape(1, D), w_pool.astype(BF16), scale.reshape(1, D))


def _rel_bucket(rel):
    n = jnp.maximum(rel, 0)
    max_exact = REL_BUCKETS // 2
    nf = jnp.maximum(n, 1).astype(F32)
    large = max_exact + (jnp.log(nf / max_exact) / math.log(REL_MAX_DIST / max_exact)
                         * (REL_BUCKETS - max_exact)).astype(jnp.int32)
    large = jnp.minimum(large, REL_BUCKETS - 1)
    return jnp.where(n < max_exact, n, large)


def _diff_attn_kernel(q_ref, k_ref, v_ref, rb_ref, lam_ref, sg_ref, o_ref,
                      m_ref, l_ref, acc_ref, bias_ref, s_ref, p_ref, a_ref, *, blk, hd, lam_init):
    qi = pl.program_id(2)
    L = 2 * blk
    lanes = m_ref.shape[-1]
    units = m_ref.shape[0]
    vd = 2 * hd

    @pl.when((qi == 0) & (pl.program_id(1) == 0))
    def _():
        for u in range(units):
            for off in range(2):
                r = jnp.broadcast_to(rb_ref[u, off:off + 1, :], (blk, L))
                bias_ref[u, off] = pltpu.roll(r, L - blk + 1, 1, stride=1, stride_axis=0)[:, :blk]

    m_ref[...] = jnp.full(m_ref.shape, -jnp.inf, F32)
    l_ref[...] = jnp.zeros(l_ref.shape, F32)
    acc_ref[...] = jnp.zeros(acc_ref.shape, F32)

    def logits(j, u):
        start = pl.multiple_of(j * blk, blk)
        cols = slice(u * hd, (u + 1) * hd)
        s_ref[u] = _dot_nt(q_ref[0, :, cols], k_ref[0, pl.ds(start, blk), cols])

    def step(j, off, has_next=True):
        start = pl.multiple_of(j * blk, blk)
        for u in range(units):
            for r0 in range(0, blk, DIFF_STRIP):
                rows = slice(r0, r0 + DIFF_STRIP)
                s = s_ref[u, rows, :]
                if off is not None:
                    s = s + bias_ref[u, off, rows, :]
                m_old = m_ref[u, rows, :]
                m_new = jnp.maximum(m_old, jnp.max(s, axis=-1, keepdims=True))
                alpha = jnp.exp2(m_old - m_new)
                p = jnp.exp2(s - jnp.concatenate([m_new] * (blk // lanes), axis=1))
                psum = p[:, :lanes]
                for t in range(1, blk // lanes):
                    psum = psum + p[:, t * lanes:(t + 1) * lanes]
                l_ref[u, rows, :] = alpha * l_ref[u, rows, :] + psum
                p_ref[u, rows, :] = p.astype(BF16)
                a_ref[u, rows, :] = alpha
                m_ref[u, rows, :] = m_new
            if has_next:
                logits(j + 1, u)
            alpha = jnp.concatenate([a_ref[u]] * (vd // lanes), axis=1)
            vb = v_ref[0, pl.ds(start, blk), (u // 2) * vd:(u // 2 + 1) * vd]
            acc_ref[u] = alpha * acc_ref[u] + _dot(p_ref[u], vb)

    def far_body(j, carry):
        step(j, None)
        return carry

    for u in range(units):
        logits(0, u)
    lax.fori_loop(0, jnp.maximum(qi - 1, 0), far_body, 0)

    @pl.when(qi >= 1)
    def _():
        step(qi - 1, 1)

    step(qi, 0, has_next=False)

    lp = lam_ref[...]
    lam = (jnp.exp(jnp.sum(lp[0:1] * lp[1:2], axis=-1, keepdims=True))
           - jnp.exp(jnp.sum(lp[2:3] * lp[3:4], axis=-1, keepdims=True)) + lam_init)
    for h in range(units // 2):
        l0 = jnp.sum(l_ref[2 * h], axis=-1, keepdims=True)
        l1 = jnp.sum(l_ref[2 * h + 1], axis=-1, keepdims=True)
        o = acc_ref[2 * h] / l0 - lam * (acc_ref[2 * h + 1] / l1)
        o_ref[0, :, h * vd:(h + 1) * vd] = (_rms(o, sg_ref[...]) * (1.0 - lam_init)).astype(o_ref.dtype)


def diff_attention(proj, rel_vecs, lam_params, sub_gain, *, d_model, lam_init, blk):
    B, S, _ = proj.shape
    H = DIFF_HEADS
    HS = DIFF_HEADS_PER_STEP
    vd = d_model // H
    hd = vd // 2
    lanes = 128
    units = 2 * HS
    wd = HS * vd
    HG = H // HS
    assert S % blk == 0 and blk % lanes == 0 and H % HS == 0
    kern = functools.partial(_diff_attn_kernel, blk=blk, hd=hd, lam_init=lam_init)
    return pl.pallas_call(
        kern,
        grid=(HG, B, S // blk),
        in_specs=[
            pl.BlockSpec((1, blk, wd), lambda h, b, i: (b, i, h)),
            pl.BlockSpec((1, S, wd), lambda h, b, i: (b, 0, HG + h)),
            pl.BlockSpec((1, S, wd), lambda h, b, i: (b, 0, 2 * HG + h)),
            pl.BlockSpec((units, 2, 2 * blk), lambda h, b, i: (h, 0, 0)),
            pl.BlockSpec((4, hd), lambda h, b, i: (0, 0)),
            pl.BlockSpec((1, vd), lambda h, b, i: (0, 0)),
        ],
        out_specs=pl.BlockSpec((1, blk, wd), lambda h, b, i: (b, i, h)),
        out_shape=jax.ShapeDtypeStruct((B, S, d_model), BF16),
        scratch_shapes=[
            pltpu.VMEM((units, blk, lanes), F32),
            pltpu.VMEM((units, blk, lanes), F32),
            pltpu.VMEM((units, blk, vd), F32),
            pltpu.VMEM((units, 2, blk, blk), F32),
            pltpu.VMEM((units, blk, blk), F32),
            pltpu.VMEM((units, blk, blk), BF16),
            pltpu.VMEM((units, blk, lanes), F32),
        ],
        compiler_params=_cparams("arbitrary", "arbitrary", "arbitrary"),
        name="diff_attention",
    )(proj, proj, proj, rel_vecs, lam_params, sub_gain.reshape(1, vd))


def _rel_bias_vectors(rel_table, blk):
    H2 = rel_table.shape[1]
    far = rel_table[REL_BUCKETS - 1]
    y = jnp.arange(2 * blk)
    vecs = []
    for off in (0, 1):
        rel = off * blk + blk - 1 - y
        v = rel_table[_rel_bucket(rel)].astype(F32) - far
        vecs.append(jnp.where((rel >= 0)[:, None], v * LOG2_E, -jnp.inf).T)
    return jnp.stack(vecs, axis=1)


def diff_attn_mixer(x, g, w_in, w_out, layer, q_gain, k_gain, lam_params, sub_gain, rel_table,
                    layer_idx, blk=512):
    B, S, D = x.shape
    assert blk >= REL_MAX_DIST
    hd = D // DIFF_HEADS // 2
    lam_init = 0.8 - 0.6 * math.exp(-0.3 * layer_idx)
    n_heads2 = D // hd
    head_gain = jnp.concatenate([jnp.tile(q_gain, n_heads2) * (hd ** -0.5 * LOG2_E),
                                 jnp.tile(k_gain, n_heads2), jnp.ones((D,), F32)])
    x2 = x.reshape(B * S, D)
    proj = norm_matmul(x2, g, w_in, layer, head_gain=head_gain, head_norm_cols=2 * D, head_dim=hd)
    o = diff_attention(proj.reshape(B, S, 3 * D), _rel_bias_vectors(rel_table, blk), lam_params,
                       sub_gain, d_model=D, lam_init=lam_init, blk=blk)
    return matmul_residual(o.reshape(B * S, D), w_out, layer, x2).reshape(B, S, D)


def kernel(x, norm_g, gla_w_in, gla_w_a2, gla_b_a, gla_g_norm, gla_w_out, pool_w, pool_scale,
           diff_w_in, diff_q_gain, diff_k_gain, diff_lambda, diff_sub_gain, diff_w_out, rel_bias,
           ffn_w_gu, ffn_w_down):
    B, S, D = x.shape
    depth = norm_g.shape[0]
    n_gla, n_diff = gla_w_in.shape[0], diff_w_in.shape[0]
    gla_w = gla_weights(gla_w_in, gla_w_a2, gla_w_out.shape[1])
    riders = [(ffn_w_gu, 0), (ffn_w_down, 0)]
    riders += [(diff_w_in, s) for s in range(n_diff)] + [(diff_w_out, s) for s in range(n_diff)]
    riders += [(gla_w_out, s) for s in range(n_gla)]
    for i in range(depth):
        kind, slot = i % N_MIXERS, i // N_MIXERS
        if kind == 0 and i == 0:
            x, cast_out = gla_mixer(x, norm_g[i, 0], gla_w, slot, gla_b_a[slot], gla_g_norm[slot],
                                    cast=tuple(riders), w_out_cast=2 + 2 * n_diff + slot)
            ffn_w = cast_out[:2]
            diff_w_in_b, diff_w_out_b = cast_out[2:2 + n_diff], cast_out[2 + n_diff:2 + 2 * n_diff]
            gla_out_b = cast_out[2 + 2 * n_diff:]
        elif kind == 0:
            x, _ = gla_mixer(x, norm_g[i, 0], gla_w, slot, gla_b_a[slot], gla_g_norm[slot],
                             w_out=gla_out_b[slot])
        elif kind == 1:
            x = pool_mixer(x, norm_g[i, 0], pool_w[slot], pool_scale[slot])
        else:
            x = diff_attn_mixer(x, norm_g[i, 0], diff_w_in_b[slot], diff_w_out_b[slot], 0,
                                diff_q_gain[slot], diff_k_gain[slot], diff_lambda[slot],
                                diff_sub_gain[slot], rel_bias, i)
        x2 = x.reshape(B * S, D)
        if i + 1 < depth:
            x2, *ffn_w_next = ffn(x2, norm_g[i, 1], ffn_w[0], ffn_w[1], 0,
                                  next_f32=(ffn_w_gu, ffn_w_down, i + 1))
            ffn_w = tuple(ffn_w_next)
        else:
            x2 = ffn(x2, norm_g[i, 1], ffn_w[0], ffn_w[1], 0)
        x = x2.reshape(B, S, D)
    return x
```

```python
import functools
import math

import jax
import jax.numpy as jnp
from jax import lax
from jax.experimental import pallas as pl
from jax.experimental.pallas import tpu as pltpu

F32 = jnp.float32
BF16 = jnp.bfloat16

EPS = 1e-6
N_MIXERS = 3
LOG2_E = math.log2(math.e)

GLA_HEADS = 4
GLA_TAU = 16.0
GLA_CHUNK = 128
GLA_DIAG = 8
GLA_RANK_PAD = 128
GLA_HEADS_PER_STEP = 4
GLA_BATCH_PER_STEP = 4
GLA_IN_PAD = 256
GLA_IN_COL_STEPS = 5

POOL_WINDOWS = (2, 4, 8, 16)
POOL_HALO = 16

DIFF_HEADS = 8
DIFF_STRIP = 32
DIFF_HEADS_PER_STEP = 2
REL_BUCKETS = 32
REL_MAX_DIST = 128

VMEM_LIMIT_BYTES = 58 * 1024 * 1024


def _cparams(*sem):
    return pltpu.CompilerParams(dimension_semantics=sem, vmem_limit_bytes=VMEM_LIMIT_BYTES)


def _rms(x, g):
    ms = jnp.mean(x * x, axis=-1, keepdims=True)
    return x * lax.rsqrt(ms + EPS) * g


def _dot(a, b):
    return jnp.dot(a, b, preferred_element_type=F32)


def _dot_nt(a, b):
    return lax.dot_general(a, b, (((1,), (1,)), ((), ())), preferred_element_type=F32)


def _dot_tn(a, b):
    return lax.dot_general(a, b, (((0,), (0,)), ((), ())), preferred_element_type=F32)


def _norm_matmul_kernel(x_hbm, g_ref, w_ref, hg_ref, o_ref, h_ref, xbuf, xsem, *,
                        head_norm_blocks, head_dim, row_split):
    i, j = pl.program_id(0), pl.program_id(1)
    bm = xbuf.shape[0]

    def x_copy(row_block):
        return pltpu.make_async_copy(x_hbm.at[pl.ds(row_block * bm, bm), :], xbuf, xsem)

    @pl.when((i == 0) & (j == 0))
    def _():
        x_copy(0).start()

    def column_block(first, head_norm):
        if first:
            x_copy(i).wait()
            h_ref[...] = _rms(xbuf[...], g_ref[...]).astype(BF16)
        for r0 in range(0, bm, bm // row_split):
            rows = slice(r0, r0 + bm // row_split)
            acc = _dot(h_ref[rows, :], w_ref[...])
            if head_norm:
                for c in range(acc.shape[1] // head_dim):
                    sl = slice(c * head_dim, (c + 1) * head_dim)
                    o_ref[rows, sl] = _rms(acc[:, sl], hg_ref[:, sl]).astype(o_ref.dtype)
            else:
                o_ref[rows, :] = acc.astype(o_ref.dtype)

    pl.when(j == 0)(functools.partial(column_block, True, head_norm_blocks > 0))
    if head_norm_blocks > 1:
        pl.when((j > 0) & (j < head_norm_blocks))(functools.partial(column_block, False, True))
    pl.when(j >= max(head_norm_blocks, 1))(functools.partial(column_block, False, False))

    @pl.when((j == 1) & (i + 1 < pl.num_programs(0)))
    def _():
        x_copy(i + 1).start()


def norm_matmul(x, g, w, layer, head_gain=None, head_norm_cols=0, head_dim=128, bm=2048, bn=1024):
    M, D = x.shape
    N = w.shape[2]
    bm = min(bm, M)
    assert M % bm == 0 and N % bn == 0 and head_norm_cols % bn == 0
    assert N // bn >= 2
    if head_gain is None:
        head_gain = jnp.ones((N,), F32)
    kern = functools.partial(_norm_matmul_kernel, head_norm_blocks=head_norm_cols // bn,
                             head_dim=head_dim, row_split=2 if bm % 32 == 0 else 1)
    return pl.pallas_call(
        kern,
        grid=(M // bm, N // bn),
        in_specs=[
            pl.BlockSpec(memory_space=pl.ANY),
            pl.BlockSpec((1, D), lambda i, j: (0, 0)),
            pl.BlockSpec((None, D, bn), lambda i, j: (layer, 0, j)),
            pl.BlockSpec((1, bn), lambda i, j: (0, j)),
        ],
        out_specs=pl.BlockSpec((bm, bn), lambda i, j: (i, j)),
        out_shape=jax.ShapeDtypeStruct((M, N), BF16),
        scratch_shapes=[pltpu.VMEM((bm, D), BF16), pltpu.VMEM((bm, D), F32),
                        pltpu.SemaphoreType.DMA(())],
        compiler_params=_cparams("arbitrary", "arbitrary"),
        name="norm_matmul",
    )(x, g.reshape(1, D), w, head_gain.reshape(1, N))


def _matmul_residual_kernel(a_ref, w_ref, x_ref, o_ref):
    o_ref[...] = x_ref[...] + _dot(a_ref[...], w_ref[...])


def matmul_residual(a, w, layer, x, bm=512, bn=2048):
    M, K = a.shape
    N = w.shape[2]
    bm = min(bm, M)
    assert M % bm == 0 and N % bn == 0
    return pl.pallas_call(
        _matmul_residual_kernel,
        grid=(M // bm, N // bn),
        in_specs=[
            pl.BlockSpec((bm, K), lambda i, j: (i, 0)),
            pl.BlockSpec((None, K, bn), lambda i, j: (layer, 0, j)),
            pl.BlockSpec((bm, bn), lambda i, j: (i, j)),
        ],
        out_specs=pl.BlockSpec((bm, bn), lambda i, j: (i, j)),
        out_shape=jax.ShapeDtypeStruct((M, N), F32),
        compiler_params=_cparams("parallel", "arbitrary"),
        name="matmul_residual",
    )(a, w, x)


def _ffn_kernel(x_hbm, g_ref, wg_ref, wu_ref, wd_ref, *rest, cast_next):
    if cast_next:
        ngu_ref, ndn_ref, o_ref, ogu_ref, odn_ref, h_ref, xbuf, xsem = rest
    else:
        o_ref, h_ref, xbuf, xsem = rest
    i, j = pl.program_id(0), pl.program_id(1)
    bm = xbuf.shape[0]

    def x_copy(row_block):
        return pltpu.make_async_copy(x_hbm.at[pl.ds(row_block * bm, bm), :], xbuf, xsem)

    @pl.when((i == 0) & (j == 0))
    def _():
        x_copy(0).start()

    def hidden_chunk(first):
        if first:
            x_copy(i).wait()
            h_ref[...] = _rms(xbuf[...], g_ref[...]).astype(BF16)
        h = h_ref[...]
        gate = _dot(h, wg_ref[...])
        up = _dot(h, wu_ref[...])
        act = (gate * jax.nn.sigmoid(gate) * up).astype(BF16)
        down = _dot(act, wd_ref[...])
        if first:
            o_ref[...] = xbuf[...] + down
        else:
            o_ref[...] += down
        if cast_next:
            ogu_ref[...] = ngu_ref[...].astype(BF16)
            odn_ref[...] = ndn_ref[...].astype(BF16)

    pl.when(j == 0)(functools.partial(hidden_chunk, True))
    pl.when(j > 0)(functools.partial(hidden_chunk, False))

    @pl.when((j == 1) & (i + 1 < pl.num_programs(0)))
    def _():
        x_copy(i + 1).start()


def ffn(x, g, w_gu, w_down, layer, next_f32=None, bm=1024, bh=512):
    M, D = x.shape
    FH = w_down.shape[1]
    bm = min(bm, M)
    assert M % bm == 0 and FH % bh == 0
    ni, nh = M // bm, FH // bh
    assert nh >= 2
    in_specs = [
        pl.BlockSpec(memory_space=pl.ANY),
        pl.BlockSpec((1, D), lambda i, j: (0, 0)),
        pl.BlockSpec((None, D, bh), lambda i, j: (layer, 0, j)),
        pl.BlockSpec((None, D, bh), lambda i, j: (layer, 0, j + nh)),
        pl.BlockSpec((None, bh, D), lambda i, j: (layer, j, 0)),
    ]
    out_specs = pl.BlockSpec((bm, D), lambda i, j: (i, 0))
    out_shape = jax.ShapeDtypeStruct((M, D), F32)
    args = (x, g.reshape(1, D), w_gu, w_gu, w_down)
    if next_f32 is not None:
        gu32, dn32, nxt = next_f32
        gu_blk = (D // ni, 2 * FH // nh)
        dn_rows = FH // (ni * nh)
        assert D % ni == 0 and (2 * FH) % nh == 0 and FH % (ni * nh) == 0
        assert gu_blk[0] % 16 == 0 and gu_blk[1] % 128 == 0 and dn_rows % 16 == 0
        in_specs += [
            pl.BlockSpec((None,) + gu_blk, lambda i, j: (nxt, i, j)),
            pl.BlockSpec((None, dn_rows, D), lambda i, j: (nxt, i * nh + j, 0)),
        ]
        out_specs = (out_specs,
                     pl.BlockSpec((None,) + gu_blk, lambda i, j: (0, i, j)),
                     pl.BlockSpec((None, dn_rows, D), lambda i, j: (0, i * nh + j, 0)))
        out_shape = (out_shape, jax.ShapeDtypeStruct((1, D, 2 * FH), BF16),
                     jax.ShapeDtypeStruct((1, FH, D), BF16))
        args = args + (gu32, dn32)
    return pl.pallas_call(
        functools.partial(_ffn_kernel, cast_next=next_f32 is not None),
        grid=(ni, nh),
        in_specs=in_specs,
        out_specs=out_specs,
        out_shape=out_shape,
        scratch_shapes=[pltpu.VMEM((bm, D), BF16), pltpu.VMEM((bm, D), F32),
                        pltpu.SemaphoreType.DMA(())],
        compiler_params=_cparams("arbitrary", "arbitrary"),
        name="ffn",
    )(*args)


def _block_rows(ref, rows, span):
    return jnp.concatenate(
        [jnp.broadcast_to(ref[r:r + 1, :], (span, ref.shape[1])) for r in rows], axis=0)


def _gla_kernel(q_ref, k_ref, v_ref, r_ref, a_ref, wa_ref, ba_ref, gn_ref, *rest,
                chunk, diag, scale, hk, hv, heads, n_cast):
    cast_in, (o_ref,), cast_out = rest[:n_cast], rest[n_cast:n_cast + 1], rest[n_cast + 1:2 * n_cast + 1]
    st_ref, b_ref, kf_ref = rest[2 * n_cast + 1:]
    C = chunk

    @pl.when(pl.program_id(2) == 0)
    def _():
        st_ref[...] = jnp.zeros_like(st_ref)

    row = lax.broadcasted_iota(jnp.int32, (C, C), 0)
    col = lax.broadcasted_iota(jnp.int32, (C, C), 1)
    rix = lax.broadcasted_iota(jnp.int32, (C, 1), 0)
    dcol = lax.broadcasted_iota(jnp.int32, (hk, C), 1)
    tri = (row >= col).astype(BF16)

    for src, dst in zip(cast_in, cast_out):
        dst[...] = src[...].astype(BF16)

    chains = range(q_ref.shape[0] * heads)
    cols = [(slice((n % heads) * hk, (n % heads + 1) * hk),
             slice((n % heads) * hv, (n % heads + 1) * hv)) for n in chains]
    bbs = [n // heads for n in chains]
    qs, ks_f, bs_ = {}, {}, {}
    for n in chains:
        ks_, _ = cols[n]
        qs[n] = q_ref[bbs[n], :, ks_].astype(F32) * scale
        ks_f[n] = k_ref[bbs[n], :, ks_].astype(F32)
        z = _dot(a_ref[bbs[n]], wa_ref[:, ks_]) + ba_ref[:, ks_]
        log_a = (jnp.minimum(z, 0.0) - jnp.log(1.0 + jnp.exp(-jnp.abs(z)))) * (LOG2_E / GLA_TAU)
        a_hi = log_a.astype(BF16)
        rem = log_a - a_hi.astype(F32)
        a_mid = rem.astype(BF16)
        a_lo = (rem - a_mid.astype(F32)).astype(BF16)
        b = _dot(tri, a_hi) + _dot(tri, a_mid) + _dot(tri, a_lo)
        bs_[n] = b
        b_ref[n] = b
        kf_ref[n] = ks_f[n]

    outs, scores = {}, {}
    for n in chains:
        outs[n] = _dot_nt((qs[n] * jnp.exp2(bs_[n])).astype(BF16), st_ref[n].astype(BF16))
        scores[n] = jnp.zeros((C, C), F32)

    w = C // 2
    while w >= diag:
        blk = 2 * w
        right = (rix % blk) >= w
        keep = ((row // blk) == (col // blk)) & ((row % blk) >= w) & ((col % blk) < w)
        for n in chains:
            beta = _block_rows(b_ref.at[n], range(w, C, blk), blk)
            m = (jnp.where(right, qs[n], ks_f[n]) * jnp.exp2(-jnp.abs(bs_[n] - beta))).astype(BF16)
            scores[n] = scores[n] + jnp.where(keep, _dot_nt(m, m), 0.0)
        w //= 2

    dsum = {n: jnp.zeros((C, C), F32) for n in chains}
    for s in range(diag):
        rows = range(s, C, diag)
        sel = ((dcol % diag) == s).astype(BF16)
        for n in chains:
            bs = _block_rows(b_ref.at[n], rows, diag)
            ks = _block_rows(kf_ref.at[n], rows, diag)
            e = jnp.exp2(jnp.where((rix % diag) >= s, bs_[n] - bs, -jnp.inf))
            p = (qs[n] * ks * e).astype(BF16)
            dsum[n] = dsum[n] + _dot(p, sel)

    for n in chains:
        _, vs_ = cols[n]
        v = v_ref[bbs[n], :, vs_]
        sc = scores[n] + jnp.where((row // diag) == (col // diag), dsum[n], 0.0)
        o = outs[n] + _dot(sc.astype(BF16), v)

        b = bs_[n]
        b_last = b[C - 1:C, :]
        k_dec = (ks_f[n] * jnp.exp2(b_last - b)).astype(BF16)
        st_ref[n] = st_ref[n] * jnp.exp2(b_last) + _dot_tn(v, k_dec)

        r = r_ref[bbs[n], :, vs_].astype(F32)
        o_ref[bbs[n], :, vs_] = (_rms(o, gn_ref[...]) * (r * jax.nn.sigmoid(r))).astype(o_ref.dtype)


def gla_core(proj, w_a2p, layer, b_a, g_norm, *, dk, dv, cast=()):
    B, S, _ = proj.shape
    H = GLA_HEADS
    hk, hv = dk // H, dv // H
    C = GLA_CHUNK
    HS = GLA_HEADS_PER_STEP
    NB = min(GLA_BATCH_PER_STEP, B)
    assert S % C == 0 and H % HS == 0 and B % NB == 0
    wk, wv = HS * hk, HS * hv
    k_blk0 = dk // wk
    v_blk0 = 2 * dk // wv
    r_blk0 = (2 * dk + dv) // wv
    a_blk = (2 * dk + 2 * dv) // GLA_RANK_PAD
    kern = functools.partial(_gla_kernel, chunk=C, diag=GLA_DIAG, scale=hk ** -0.5,
                             hk=hk, hv=hv, heads=HS, n_cast=len(cast))
    grid = (B // NB, H // HS, S // C)
    n_steps = grid[0] * grid[1] * grid[2]
    cast_in_specs, cast_out_specs, cast_shapes = [], [], []
    for arr, lyr in cast:
        _, R, N = arr.shape
        band = R // n_steps
        assert R % n_steps == 0 and band % 16 == 0
        cast_in_specs.append(pl.BlockSpec(
            (None, band, N), lambda b, h, c, lyr=lyr: (lyr, (b * grid[1] + h) * grid[2] + c, 0)))
        cast_out_specs.append(pl.BlockSpec(
            (None, band, N), lambda b, h, c: (0, (b * grid[1] + h) * grid[2] + c, 0)))
        cast_shapes.append(jax.ShapeDtypeStruct((1, R, N), BF16))
    outs = pl.pallas_call(
        kern,
        grid=grid,
        in_specs=[
            pl.BlockSpec((NB, C, wk), lambda b, h, c: (b, c, h)),
            pl.BlockSpec((NB, C, wk), lambda b, h, c: (b, c, k_blk0 + h)),
            pl.BlockSpec((NB, C, wv), lambda b, h, c: (b, c, v_blk0 + h)),
            pl.BlockSpec((NB, C, wv), lambda b, h, c: (b, c, r_blk0 + h)),
            pl.BlockSpec((NB, C, GLA_RANK_PAD), lambda b, h, c: (b, c, a_blk)),
            pl.BlockSpec((None, GLA_RANK_PAD, wk), lambda b, h, c: (layer, 0, h)),
            pl.BlockSpec((1, wk), lambda b, h, c: (0, h)),
            pl.BlockSpec((1, hv), lambda b, h, c: (0, 0)),
        ] + cast_in_specs,
        out_specs=[pl.BlockSpec((NB, C, wv), lambda b, h, c: (b, c, h))] + cast_out_specs,
        out_shape=[jax.ShapeDtypeStruct((B, S, dv), BF16)] + cast_shapes,
        scratch_shapes=[pltpu.VMEM((NB * HS, hv, hk), F32), pltpu.VMEM((NB * HS, C, hk), F32),
                        pltpu.VMEM((NB * HS, C, hk), F32)],
        compiler_params=_cparams("arbitrary", "arbitrary", "arbitrary"),
        name="gla_core",
    )(proj, proj, proj, proj, proj, w_a2p, b_a.reshape(1, dk), g_norm.reshape(1, hv),
      *[arr for arr, _ in cast])
    return outs[0], tuple(outs[1:])


def gla_weights(w_in, w_a2, dv):
    n_pad = 2 * w_a2.shape[2] + 2 * dv + GLA_IN_PAD
    w_in_p = jnp.pad(w_in.astype(BF16), ((0, 0), (0, 0), (0, n_pad - w_in.shape[2])))
    w_a2p = jnp.pad(w_a2, ((0, 0), (0, GLA_RANK_PAD - w_a2.shape[1]), (0, 0))).astype(BF16)
    return w_in_p, w_a2p


def gla_mixer(x, g, weights, layer, b_a, g_norm, w_out=None, cast=(), w_out_cast=None):
    w_in_p, w_a2p = weights
    B, S, D = x.shape
    dk = w_a2p.shape[2]
    n_pad = w_in_p.shape[2]
    dv = (n_pad - GLA_IN_PAD - 2 * dk) // 2
    x2 = x.reshape(B * S, D)
    proj = norm_matmul(x2, g, w_in_p, layer, bn=n_pad // GLA_IN_COL_STEPS)
    o, cast_out = gla_core(proj.reshape(B, S, n_pad), w_a2p, layer, b_a, g_norm, dk=dk, dv=dv,
                           cast=cast)
    if w_out_cast is not None:
        w_out = cast_out[w_out_cast]
    return matmul_residual(o.reshape(B * S, dv), w_out, 0, x2).reshape(B, S, D), cast_out


def _pool_kernel(x_ref, g_ref, w_ref, sc_ref, o_ref, hb_ref, *, bm, gw):
    si = pl.program_id(1)
    HL = POOL_HALO

    @pl.when(si == 0)
    def _():
        hb_ref[0:HL, :] = jnp.zeros((HL, hb_ref.shape[1]), F32)

    @pl.when(si > 0)
    def _():
        hb_ref[0:HL, :] = hb_ref[bm:bm + HL, :]

    x = x_ref[0]
    hb_ref[HL:HL + bm, :] = _rms(x, g_ref[...])

    t = si * bm + lax.broadcasted_iota(jnp.int32, (bm, 1), 0)
    for gi, win in enumerate(POOL_WINDOWS):
        cs = slice(gi * gw, (gi + 1) * gw)
        ext = hb_ref[:, cs]
        d = 1
        while d < win:
            ext = ext + pltpu.roll(ext, d, 0)
            d *= 2
        h = hb_ref[HL:HL + bm, cs]
        acc = ext[HL:, :]
        count = jnp.minimum(t + 1, win).astype(F32)
        y = (acc / count - h).astype(BF16)
        o_ref[0, :, cs] = x[:, cs] + _dot(y, w_ref[gi]) * sc_ref[:, cs]


def pool_mixer(x, g, w_pool, scale, bm=512):
    B, S, D = x.shape
    G, gw, _ = w_pool.shape
    assert S % bm == 0 and G == len(POOL_WINDOWS)
    assert all(w & (w - 1) == 0 and w <= POOL_HALO for w in POOL_WINDOWS)
    kern = functools.partial(_pool_kernel, bm=bm, gw=gw)
    return pl.pallas_call(
        kern,
        grid=(B, S // bm),
        in_specs=[
            pl.BlockSpec((1, bm, D), lambda b, s: (b, s, 0)),
            pl.BlockSpec((1, D), lambda b, s: (0, 0)),
            pl.BlockSpec((G, gw, gw), lambda b, s: (0, 0, 0)),
            pl.BlockSpec((1, D), lambda b, s: (0, 0)),
        ],
        out_specs=pl.BlockSpec((1, bm, D), lambda b, s: (b, s, 0)),
        out_shape=jax.ShapeDtypeStruct((B, S, D), F32),
        scratch_shapes=[pltpu.VMEM((POOL_HALO + bm, D), F32)],
        compiler_params=_cparams("parallel", "arbitrary"),
        name="pool_mixer",
    )(x, g.reshape(1, D), w_pool.astype(BF16), scale.reshape(1, D))


def _rel_bucket(rel):
    n = jnp.maximum(rel, 0)
    max_exact = REL_BUCKETS // 2
    nf = jnp.maximum(n, 1).astype(F32)
    large = max_exact + (jnp.log(nf / max_exact) / math.log(REL_MAX_DIST / max_exact)
                         * (REL_BUCKETS - max_exact)).astype(jnp.int32)
    large = jnp.minimum(large, REL_BUCKETS - 1)
    return jnp.where(n < max_exact, n, large)


def _diff_attn_kernel(q_ref, k_ref, v_ref, rb_ref, lam_ref, sg_ref, o_ref,
                      m_ref, l_ref, acc_ref, bias_ref, s_ref, p_ref, a_ref, *, blk, hd, lam_init):
    qi = pl.program_id(2)
    L = 2 * blk
    lanes = m_ref.shape[-1]
    units = m_ref.shape[0]
    vd = 2 * hd

    @pl.when((qi == 0) & (pl.program_id(1) == 0))
    def _():
        for u in range(units):
            for off in range(2):
                r = jnp.broadcast_to(rb_ref[u, off:off + 1, :], (blk, L))
                bias_ref[u, off] = pltpu.roll(r, L - blk + 1, 1, stride=1, stride_axis=0)[:, :blk]

    m_ref[...] = jnp.full(m_ref.shape, -jnp.inf, F32)
    l_ref[...] = jnp.zeros(l_ref.shape, F32)
    acc_ref[...] = jnp.zeros(acc_ref.shape, F32)

    def logits(j, u):
        start = pl.multiple_of(j * blk, blk)
        cols = slice(u * hd, (u + 1) * hd)
        s_ref[u] = _dot_nt(q_ref[0, :, cols], k_ref[0, pl.ds(start, blk), cols])

    def step(j, off, has_next=True):
        start = pl.multiple_of(j * blk, blk)
        for u in range(units):
            for r0 in range(0, blk, DIFF_STRIP):
                rows = slice(r0, r0 + DIFF_STRIP)
                s = s_ref[u, rows, :]
                if off is not None:
                    s = s + bias_ref[u, off, rows, :]
                m_old = m_ref[u, rows, :]
                m_new = jnp.maximum(m_old, jnp.max(s, axis=-1, keepdims=True))
                alpha = jnp.exp2(m_old - m_new)
                p = jnp.exp2(s - jnp.concatenate([m_new] * (blk // lanes), axis=1))
                psum = p[:, :lanes]
                for t in range(1, blk // lanes):
                    psum = psum + p[:, t * lanes:(t + 1) * lanes]
                l_ref[u, rows, :] = alpha * l_ref[u, rows, :] + psum
                p_ref[u, rows, :] = p.astype(BF16)
                a_ref[u, rows, :] = alpha
                m_ref[u, rows, :] = m_new
            if has_next:
                logits(j + 1, u)
            alpha = jnp.concatenate([a_ref[u]] * (vd // lanes), axis=1)
            vb = v_ref[0, pl.ds(start, blk), (u // 2) * vd:(u // 2 + 1) * vd]
            acc_ref[u] = alpha * acc_ref[u] + _dot(p_ref[u], vb)

    def far_body(j, carry):
        step(j, None)
        return carry

    for u in range(units):
        logits(0, u)
    lax.fori_loop(0, jnp.maximum(qi - 1, 0), far_body, 0)

    @pl.when(qi >= 1)
    def _():
        step(qi - 1, 1)

    step(qi, 0, has_next=False)

    lp = lam_ref[...]
    lam = (jnp.exp(jnp.sum(lp[0:1] * lp[1:2], axis=-1, keepdims=True))
           - jnp.exp(jnp.sum(lp[2:3] * lp[3:4], axis=-1, keepdims=True)) + lam_init)
    for h in range(units // 2):
        l0 = jnp.sum(l_ref[2 * h], axis=-1, keepdims=True)
        l1 = jnp.sum(l_ref[2 * h + 1], axis=-1, keepdims=True)
        o = acc_ref[2 * h] / l0 - lam * (acc_ref[2 * h + 1] / l1)
        o_ref[0, :, h * vd:(h + 1) * vd] = (_rms(o, sg_ref[...]) * (1.0 - lam_init)).astype(o_ref.dtype)


def diff_attention(proj, rel_vecs, lam_params, sub_gain, *, d_model, lam_init, blk):
    B, S, _ = proj.shape
    H = DIFF_HEADS
    HS = DIFF_HEADS_PER_STEP
    vd = d_model // H
    hd = vd // 2
    lanes = 128
    units = 2 * HS
    wd = HS * vd
    HG = H // HS
    assert S % blk == 0 and blk % lanes == 0 and H % HS == 0
    kern = functools.partial(_diff_attn_kernel, blk=blk, hd=hd, lam_init=lam_init)
    return pl.pallas_call(
        kern,
        grid=(HG, B, S // blk),
        in_specs=[
            pl.BlockSpec((1, blk, wd), lambda h, b, i: (b, i, h)),
            pl.BlockSpec((1, S, wd), lambda h, b, i: (b, 0, HG + h)),
            pl.BlockSpec((1, S, wd), lambda h, b, i: (b, 0, 2 * HG + h)),
            pl.BlockSpec((units, 2, 2 * blk), lambda h, b, i: (h, 0, 0)),
            pl.BlockSpec((4, hd), lambda h, b, i: (0, 0)),
            pl.BlockSpec((1, vd), lambda h, b, i: (0, 0)),
        ],
        out_specs=pl.BlockSpec((1, blk, wd), lambda h, b, i: (b, i, h)),
        out_shape=jax.ShapeDtypeStruct((B, S, d_model), BF16),
        scratch_shapes=[
            pltpu.VMEM((units, blk, lanes), F32),
            pltpu.VMEM((units, blk, lanes), F32),
            pltpu.VMEM((units, blk, vd), F32),
            pltpu.VMEM((units, 2, blk, blk), F32),
            pltpu.VMEM((units, blk, blk), F32),
            pltpu.VMEM((units, blk, blk), BF16),
            pltpu.VMEM((units, blk, lanes), F32),
        ],
        compiler_params=_cparams("arbitrary", "arbitrary", "arbitrary"),
        name="diff_attention",
    )(proj, proj, proj, rel_vecs, lam_params, sub_gain.reshape(1, vd))


def _rel_bias_vectors(rel_table, blk):
    H2 = rel_table.shape[1]
    far = rel_table[REL_BUCKETS - 1]
    y = jnp.arange(2 * blk)
    vecs = []
    for off in (0, 1):
        rel = off * blk + blk - 1 - y
        v = rel_table[_rel_bucket(rel)].astype(F32) - far
        vecs.append(jnp.where((rel >= 0)[:, None], v * LOG2_E, -jnp.inf).T)
    return jnp.stack(vecs, axis=1)


def diff_attn_mixer(x, g, w_in, w_out, layer, q_gain, k_gain, lam_params, sub_gain, rel_table,
                    layer_idx, blk=512):
    B, S, D = x.shape
    assert blk >= REL_MAX_DIST
    hd = D // DIFF_HEADS // 2
    lam_init = 0.8 - 0.6 * math.exp(-0.3 * layer_idx)
    n_heads2 = D // hd
    head_gain = jnp.concatenate([jnp.tile(q_gain, n_heads2) * (hd ** -0.5 * LOG2_E),
                                 jnp.tile(k_gain, n_heads2), jnp.ones((D,), F32)])
    x2 = x.reshape(B * S, D)
    proj = norm_matmul(x2, g, w_in, layer, head_gain=head_gain, head_norm_cols=2 * D, head_dim=hd)
    o = diff_attention(proj.reshape(B, S, 3 * D), _rel_bias_vectors(rel_table, blk), lam_params,
                       sub_gain, d_model=D, lam_init=lam_init, blk=blk)
    return matmul_residual(o.reshape(B * S, D), w_out, layer, x2).reshape(B, S, D)


def kernel(x, norm_g, gla_w_in, gla_w_a2, gla_b_a, gla_g_norm, gla_w_out, pool_w, pool_scale,
           diff_w_in, diff_q_gain, diff_k_gain, diff_lambda, diff_sub_gain, diff_w_out, rel_bias,
           ffn_w_gu, ffn_w_down):
    B, S, D = x.shape
    depth = norm_g.shape[0]
    n_gla, n_diff = gla_w_in.shape[0], diff_w_in.shape[0]
    gla_w = gla_weights(gla_w_in, gla_w_a2, gla_w_out.shape[1])
    riders = [(ffn_w_gu, 0), (ffn_w_down, 0)]
    riders += [(diff_w_in, s) for s in range(n_diff)] + [(diff_w_out, s) for s in range(n_diff)]
    riders += [(gla_w_out, s) for s in range(n_gla)]
    for i in range(depth):
        kind, slot = i % N_MIXERS, i // N_MIXERS
        if kind == 0 and i == 0:
            x, cast_out = gla_mixer(x, norm_g[i, 0], gla_w, slot, gla_b_a[slot], gla_g_norm[slot],
                                    cast=tuple(riders), w_out_cast=2 + 2 * n_diff + slot)
            ffn_w = cast_out[:2]
            diff_w_in_b, diff_w_out_b = cast_out[2:2 + n_diff], cast_out[2 + n_diff:2 + 2 * n_diff]
            gla_out_b = cast_out[2 + 2 * n_diff:]
        elif kind == 0:
            x, _ = gla_mixer(x, norm_g[i, 0], gla_w, slot, gla_b_a[slot], gla_g_norm[slot],
                             w_out=gla_out_b[slot])
        elif kind == 1:
            x = pool_mixer(x, norm_g[i, 0], pool_w[slot], pool_scale[slot])
        else:
            x = diff_attn_mixer(x, norm_g[i, 0], diff_w_in_b[slot], diff_w_out_b[slot], 0,
                                diff_q_gain[slot], diff_k_gain[slot], diff_lambda[slot],
                                diff_sub_gain[slot], rel_bias, i)
        x2 = x.reshape(B * S, D)
        if i + 1 < depth:
            x2, *ffn_w_next = ffn(x2, norm_g[i, 1], ffn_w[0], ffn_w[1], 0,
                                  next_f32=(ffn_w_gu, ffn_w_down, i + 1))
            ffn_w = tuple(ffn_w_next)
        else:
            x2 = ffn(x2, norm_g[i, 1], ffn_w[0], ffn_w[1], 0)
        x = x2.reshape(B, S, D)
    return x
```

```python
import functools
import math

import jax
import jax.numpy as jnp
from jax import lax
from jax.experimental import pallas as pl
from jax.experimental.pallas import tpu as pltpu

F32 = jnp.float32
BF16 = jnp.bfloat16

EPS = 1e-6
N_MIXERS = 3
LOG2_E = math.log2(math.e)

GLA_HEADS = 4
GLA_TAU = 16.0
GLA_CHUNK = 128
GLA_DIAG = 8
GLA_RANK_PAD = 128
GLA_HEADS_PER_STEP = 4
GLA_BATCH_PER_STEP = 4
GLA_IN_PAD = 256
GLA_IN_COL_STEPS = 5

POOL_WINDOWS = (2, 4, 8, 16)
POOL_HALO = 16

DIFF_HEADS = 8
DIFF_STRIP = 32
DIFF_HEADS_PER_STEP = 2
REL_BUCKETS = 32
REL_MAX_DIST = 128

VMEM_LIMIT_BYTES = 58 * 1024 * 1024


def _cparams(*sem):
    return pltpu.CompilerParams(dimension_semantics=sem, vmem_limit_bytes=VMEM_LIMIT_BYTES)


def _rms(x, g):
    ms = jnp.mean(x * x, axis=-1, keepdims=True)
    return x * lax.rsqrt(ms + EPS) * g


def _dot(a, b):
    return jnp.dot(a, b, preferred_element_type=F32)


def _dot_nt(a, b):
    return lax.dot_general(a, b, (((1,), (1,)), ((), ())), preferred_element_type=F32)


def _dot_tn(a, b):
    return lax.dot_general(a, b, (((0,), (0,)), ((), ())), preferred_element_type=F32)


def _norm_matmul_kernel(x_hbm, g_ref, w_ref, hg_ref, o_ref, h_ref, xbuf, xsem, *,
                        head_norm_blocks, head_dim, row_split):
    i, j = pl.program_id(0), pl.program_id(1)
    bm = xbuf.shape[0]

    def x_copy(row_block):
        return pltpu.make_async_copy(x_hbm.at[pl.ds(row_block * bm, bm), :], xbuf, xsem)

    @pl.when((i == 0) & (j == 0))
    def _():
        x_copy(0).start()

    def column_block(first, head_norm):
        if first:
            x_copy(i).wait()
            h_ref[...] = _rms(xbuf[...], g_ref[...]).astype(BF16)
        for r0 in range(0, bm, bm // row_split):
            rows = slice(r0, r0 + bm // row_split)
            acc = _dot(h_ref[rows, :], w_ref[...])
            if head_norm:
                for c in range(acc.shape[1] // head_dim):
                    sl = slice(c * head_dim, (c + 1) * head_dim)
                    o_ref[rows, sl] = _rms(acc[:, sl], hg_ref[:, sl]).astype(o_ref.dtype)
            else:
                o_ref[rows, :] = acc.astype(o_ref.dtype)

    pl.when(j == 0)(functools.partial(column_block, True, head_norm_blocks > 0))
    if head_norm_blocks > 1:
        pl.when((j > 0) & (j < head_norm_blocks))(functools.partial(column_block, False, True))
    pl.when(j >= max(head_norm_blocks, 1))(functools.partial(column_block, False, False))

    @pl.when((j == 1) & (i + 1 < pl.num_programs(0)))
    def _():
        x_copy(i + 1).start()


def norm_matmul(x, g, w, layer, head_gain=None, head_norm_cols=0, head_dim=128, bm=2048, bn=1024):
    M, D = x.shape
    N = w.shape[2]
    bm = min(bm, M)
    assert M % bm == 0 and N % bn == 0 and head_norm_cols % bn == 0
    assert N // bn >= 2
    if head_gain is None:
        head_gain = jnp.ones((N,), F32)
    kern = functools.partial(_norm_matmul_kernel, head_norm_blocks=head_norm_cols // bn,
                             head_dim=head_dim, row_split=2 if bm % 32 == 0 else 1)
    return pl.pallas_call(
        kern,
        grid=(M // bm, N // bn),
        in_specs=[
            pl.BlockSpec(memory_space=pl.ANY),
            pl.BlockSpec((1, D), lambda i, j: (0, 0)),
            pl.BlockSpec((None, D, bn), lambda i, j: (layer, 0, j)),
            pl.BlockSpec((1, bn), lambda i, j: (0, j)),
        ],
        out_specs=pl.BlockSpec((bm, bn), lambda i, j: (i, j)),
        out_shape=jax.ShapeDtypeStruct((M, N), BF16),
        scratch_shapes=[pltpu.VMEM((bm, D), BF16), pltpu.VMEM((bm, D), F32),
                        pltpu.SemaphoreType.DMA(())],
        compiler_params=_cparams("arbitrary", "arbitrary"),
        name="norm_matmul",
    )(x, g.reshape(1, D), w, head_gain.reshape(1, N))


def _matmul_residual_kernel(a_ref, w_ref, x_ref, o_ref):
    o_ref[...] = x_ref[...] + _dot(a_ref[...], w_ref[...])


def matmul_residual(a, w, layer, x, bm=512, bn=2048):
    M, K = a.shape
    N = w.shape[2]
    bm = min(bm, M)
    assert M % bm == 0 and N % bn == 0
    return pl.pallas_call(
        _matmul_residual_kernel,
        grid=(M // bm, N // bn),
        in_specs=[
            pl.BlockSpec((bm, K), lambda i, j: (i, 0)),
            pl.BlockSpec((None, K, bn), lambda i, j: (layer, 0, j)),
            pl.BlockSpec((bm, bn), lambda i, j: (i, j)),
        ],
        out_specs=pl.BlockSpec((bm, bn), lambda i, j: (i, j)),
        out_shape=jax.ShapeDtypeStruct((M, N), F32),
        compiler_params=_cparams("parallel", "arbitrary"),
        name="matmul_residual",
    )(a, w, x)


def _ffn_kernel(x_hbm, g_ref, wg_ref, wu_ref, wd_ref, *rest, cast_next):
    if cast_next:
        ngu_ref, ndn_ref, o_ref, ogu_ref, odn_ref, h_ref, xbuf, xsem = rest
    else:
        o_ref, h_ref, xbuf, xsem = rest
    i, j = pl.program_id(0), pl.program_id(1)
    bm = xbuf.shape[0]

    def x_copy(row_block):
        return pltpu.make_async_copy(x_hbm.at[pl.ds(row_block * bm, bm), :], xbuf, xsem)

    @pl.when((i == 0) & (j == 0))
    def _():
        x_copy(0).start()

    def hidden_chunk(first):
        if first:
            x_copy(i).wait()
            h_ref[...] = _rms(xbuf[...], g_ref[...]).astype(BF16)
        h = h_ref[...]
        gate = _dot(h, wg_ref[...])
        up = _dot(h, wu_ref[...])
        act = (gate * jax.nn.sigmoid(gate) * up).astype(BF16)
        down = _dot(act, wd_ref[...])
        if first:
            o_ref[...] = xbuf[...] + down
        else:
            o_ref[...] += down
        if cast_next:
            ogu_ref[...] = ngu_ref[...].astype(BF16)
            odn_ref[...] = ndn_ref[...].astype(BF16)

    pl.when(j == 0)(functools.partial(hidden_chunk, True))
    pl.when(j > 0)(functools.partial(hidden_chunk, False))

    @pl.when((j == 1) & (i + 1 < pl.num_programs(0)))
    def _():
        x_copy(i + 1).start()


def ffn(x, g, w_gu, w_down, layer, next_f32=None, bm=1024, bh=512):
    M, D = x.shape
    FH = w_down.shape[1]
    bm = min(bm, M)
    assert M % bm == 0 and FH % bh == 0
    ni, nh = M // bm, FH // bh
    assert nh >= 2
    in_specs = [
        pl.BlockSpec(memory_space=pl.ANY),
        pl.BlockSpec((1, D), lambda i, j: (0, 0)),
        pl.BlockSpec((None, D, bh), lambda i, j: (layer, 0, j)),
        pl.BlockSpec((None, D, bh), lambda i, j: (layer, 0, j + nh)),
        pl.BlockSpec((None, bh, D), lambda i, j: (layer, j, 0)),
    ]
    out_specs = pl.BlockSpec((bm, D), lambda i, j: (i, 0))
    out_shape = jax.ShapeDtypeStruct((M, D), F32)
    args = (x, g.reshape(1, D), w_gu, w_gu, w_down)
    if next_f32 is not None:
        gu32, dn32, nxt = next_f32
        gu_blk = (D // ni, 2 * FH // nh)
        dn_rows = FH // (ni * nh)
        assert D % ni == 0 and (2 * FH) % nh == 0 and FH % (ni * nh) == 0
        assert gu_blk[0] % 16 == 0 and gu_blk[1] % 128 == 0 and dn_rows % 16 == 0
        in_specs += [
            pl.BlockSpec((None,) + gu_blk, lambda i, j: (nxt, i, j)),
            pl.BlockSpec((None, dn_rows, D), lambda i, j: (nxt, i * nh + j, 0)),
        ]
        out_specs = (out_specs,
                     pl.BlockSpec((None,) + gu_blk, lambda i, j: (0, i, j)),
                     pl.BlockSpec((None, dn_rows, D), lambda i, j: (0, i * nh + j, 0)))
        out_shape = (out_shape, jax.ShapeDtypeStruct((1, D, 2 * FH), BF16),
                     jax.ShapeDtypeStruct((1, FH, D), BF16))
        args = args + (gu32, dn32)
    return pl.pallas_call(
        functools.partial(_ffn_kernel, cast_next=next_f32 is not None),
        grid=(ni, nh),
        in_specs=in_specs,
        out_specs=out_specs,
        out_shape=out_shape,
        scratch_shapes=[pltpu.VMEM((bm, D), BF16), pltpu.VMEM((bm, D), F32),
                        pltpu.SemaphoreType.DMA(())],
        compiler_params=_cparams("arbitrary", "arbitrary"),
        name="ffn",
    )(*args)


def _block_rows(ref, rows, span):
    return jnp.concatenate(
        [jnp.broadcast_to(ref[r:r + 1, :], (span, ref.shape[1])) for r in rows], axis=0)


def _gla_kernel(q_ref, k_ref, v_ref, r_ref, a_ref, wa_ref, ba_ref, gn_ref, *rest,
                chunk, diag, scale, hk, hv, heads, n_cast):
    cast_in, (o_ref,), cast_out = rest[:n_cast], rest[n_cast:n_cast + 1], rest[n_cast + 1:2 * n_cast + 1]
    st_ref, b_ref, kf_ref = rest[2 * n_cast + 1:]
    C = chunk

    @pl.when(pl.program_id(2) == 0)
    def _():
        st_ref[...] = jnp.zeros_like(st_ref)

    row = lax.broadcasted_iota(jnp.int32, (C, C), 0)
    col = lax.broadcasted_iota(jnp.int32, (C, C), 1)
    rix = lax.broadcasted_iota(jnp.int32, (C, 1), 0)
    dcol = lax.broadcasted_iota(jnp.int32, (hk, C), 1)
    tri = (row >= col).astype(BF16)

    for src, dst in zip(cast_in, cast_out):
        dst[...] = src[...].astype(BF16)

    chains = range(q_ref.shape[0] * heads)
    cols = [(slice((n % heads) * hk, (n % heads + 1) * hk),
             slice((n % heads) * hv, (n % heads + 1) * hv)) for n in chains]
    bbs = [n // heads for n in chains]
    qs, ks_f, bs_ = {}, {}, {}
    for n in chains:
        ks_, _ = cols[n]
        qs[n] = q_ref[bbs[n], :, ks_].astype(F32) * scale
        ks_f[n] = k_ref[bbs[n], :, ks_].astype(F32)
        z = _dot(a_ref[bbs[n]], wa_ref[:, ks_]) + ba_ref[:, ks_]
        log_a = (jnp.minimum(z, 0.0) - jnp.log(1.0 + jnp.exp(-jnp.abs(z)))) * (LOG2_E / GLA_TAU)
        a_hi = log_a.astype(BF16)
        rem = log_a - a_hi.astype(F32)
        a_mid = rem.astype(BF16)
        a_lo = (rem - a_mid.astype(F32)).astype(BF16)
        b = _dot(tri, a_hi) + _dot(tri, a_mid) + _dot(tri, a_lo)
        bs_[n] = b
        b_ref[n] = b
        kf_ref[n] = ks_f[n]

    outs, scores = {}, {}
    for n in chains:
        outs[n] = _dot_nt((qs[n] * jnp.exp2(bs_[n])).astype(BF16), st_ref[n].astype(BF16))
        scores[n] = jnp.zeros((C, C), F32)

    w = C // 2
    while w >= diag:
        blk = 2 * w
        right = (rix % blk) >= w
        keep = ((row // blk) == (col // blk)) & ((row % blk) >= w) & ((col % blk) < w)
        for n in chains:
            beta = _block_rows(b_ref.at[n], range(w, C, blk), blk)
            m = (jnp.where(right, qs[n], ks_f[n]) * jnp.exp2(-jnp.abs(bs_[n] - beta))).astype(BF16)
            scores[n] = scores[n] + jnp.where(keep, _dot_nt(m, m), 0.0)
        w //= 2

    dsum = {n: jnp.zeros((C, C), F32) for n in chains}
    for s in range(diag):
        rows = range(s, C, diag)
        sel = ((dcol % diag) == s).astype(BF16)
        for n in chains:
            bs = _block_rows(b_ref.at[n], rows, diag)
            ks = _block_rows(kf_ref.at[n], rows, diag)
            e = jnp.exp2(jnp.where((rix % diag) >= s, bs_[n] - bs, -jnp.inf))
            p = (qs[n] * ks * e).astype(BF16)
            dsum[n] = dsum[n] + _dot(p, sel)

    for n in chains:
        _, vs_ = cols[n]
        v = v_ref[bbs[n], :, vs_]
        sc = scores[n] + jnp.where((row // diag) == (col // diag), dsum[n], 0.0)
        o = outs[n] + _dot(sc.astype(BF16), v)

        b = bs_[n]
        b_last = b[C - 1:C, :]
        k_dec = (ks_f[n] * jnp.exp2(b_last - b)).astype(BF16)
        st_ref[n] = st_ref[n] * jnp.exp2(b_last) + _dot_tn(v, k_dec)

        r = r_ref[bbs[n], :, vs_].astype(F32)
        o_ref[bbs[n], :, vs_] = (_rms(o, gn_ref[...]) * (r * jax.nn.sigmoid(r))).astype(o_ref.dtype)


def gla_core(proj, w_a2p, layer, b_a, g_norm, *, dk, dv, cast=()):
    B, S, _ = proj.shape
    H = GLA_HEADS
    hk, hv = dk // H, dv // H
    C = GLA_CHUNK
    HS = GLA_HEADS_PER_STEP
    NB = min(GLA_BATCH_PER_STEP, B)
    assert S % C == 0 and H % HS == 0 and B % NB == 0
    wk, wv = HS * hk, HS * hv
    k_blk0 = dk // wk
    v_blk0 = 2 * dk // wv
    r_blk0 = (2 * dk + dv) // wv
    a_blk = (2 * dk + 2 * dv) // GLA_RANK_PAD
    kern = functools.partial(_gla_kernel, chunk=C, diag=GLA_DIAG, scale=hk ** -0.5,
                             hk=hk, hv=hv, heads=HS, n_cast=len(cast))
    grid = (B // NB, H // HS, S // C)
    n_steps = grid[0] * grid[1] * grid[2]
    cast_in_specs, cast_out_specs, cast_shapes = [], [], []
    for arr, lyr in cast:
        _, R, N = arr.shape
        band = R // n_steps
        assert R % n_steps == 0 and band % 16 == 0
        cast_in_specs.append(pl.BlockSpec(
            (None, band, N), lambda b, h, c, lyr=lyr: (lyr, (b * grid[1] + h) * grid[2] + c, 0)))
        cast_out_specs.append(pl.BlockSpec(
            (None, band, N), lambda b, h, c: (0, (b * grid[1] + h) * grid[2] + c, 0)))
        cast_shapes.append(jax.ShapeDtypeStruct((1, R, N), BF16))
    outs = pl.pallas_call(
        kern,
        grid=grid,
        in_specs=[
            pl.BlockSpec((NB, C, wk), lambda b, h, c: (b, c, h)),
            pl.BlockSpec((NB, C, wk), lambda b, h, c: (b, c, k_blk0 + h)),
            pl.BlockSpec((NB, C, wv), lambda b, h, c: (b, c, v_blk0 + h)),
            pl.BlockSpec((NB, C, wv), lambda b, h, c: (b, c, r_blk0 + h)),
            pl.BlockSpec((NB, C, GLA_RANK_PAD), lambda b, h, c: (b, c, a_blk)),
            pl.BlockSpec((None, GLA_RANK_PAD, wk), lambda b, h, c: (layer, 0, h)),
            pl.BlockSpec((1, wk), lambda b, h, c: (0, h)),
            pl.BlockSpec((1, hv), lambda b, h, c: (0, 0)),
        ] + cast_in_specs,
        out_specs=[pl.BlockSpec((NB, C, wv), lambda b, h, c: (b, c, h))] + cast_out_specs,
        out_shape=[jax.ShapeDtypeStruct((B, S, dv), BF16)] + cast_shapes,
        scratch_shapes=[pltpu.VMEM((NB * HS, hv, hk), F32), pltpu.VMEM((NB * HS, C, hk), F32),
                        pltpu.VMEM((NB * HS, C, hk), F32)],
        compiler_params=_cparams("arbitrary", "arbitrary", "arbitrary"),
        name="gla_core",
    )(proj, proj, proj, proj, proj, w_a2p, b_a.reshape(1, dk), g_norm.reshape(1, hv),
      *[arr for arr, _ in cast])
    return outs[0], tuple(outs[1:])


def gla_weights(w_in, w_a2, dv):
    n_pad = 2 * w_a2.shape[2] + 2 * dv + GLA_IN_PAD
    zeros = jnp.zeros(w_in.shape[:2] + (n_pad - w_in.shape[2],), BF16)
    w_in_p = jnp.concatenate([w_in.astype(BF16), zeros], axis=2)
    w_a2p = jnp.pad(w_a2, ((0, 0), (0, GLA_RANK_PAD - w_a2.shape[1]), (0, 0))).astype(BF16)
    return w_in_p, w_a2p


def gla_mixer(x, g, weights, layer, b_a, g_norm, w_out=None, cast=(), w_out_cast=None):
    w_in_p, w_a2p = weights
    B, S, D = x.shape
    dk = w_a2p.shape[2]
    n_pad = w_in_p.shape[2]
    dv = (n_pad - GLA_IN_PAD - 2 * dk) // 2
    x2 = x.reshape(B * S, D)
    proj = norm_matmul(x2, g, w_in_p, layer, bn=n_pad // GLA_IN_COL_STEPS)
    o, cast_out = gla_core(proj.reshape(B, S, n_pad), w_a2p, layer, b_a, g_norm, dk=dk, dv=dv,
                           cast=cast)
    if w_out_cast is not None:
        w_out = cast_out[w_out_cast]
    return matmul_residual(o.reshape(B * S, dv), w_out, 0, x2).reshape(B, S, D), cast_out


def _pool_kernel(x_ref, g_ref, w_ref, sc_ref, o_ref, hb_ref, *, bm, gw):
    si = pl.program_id(1)
    HL = POOL_HALO

    @pl.when(si == 0)
    def _():
        hb_ref[0:HL, :] = jnp.zeros((HL, hb_ref.shape[1]), F32)

    @pl.when(si > 0)
    def _():
        hb_ref[0:HL, :] = hb_ref[bm:bm + HL, :]

    x = x_ref[0]
    hb_ref[HL:HL + bm, :] = _rms(x, g_ref[...])

    t = si * bm + lax.broadcasted_iota(jnp.int32, (bm, 1), 0)
    for gi, win in enumerate(POOL_WINDOWS):
        cs = slice(gi * gw, (gi + 1) * gw)
        ext = hb_ref[:, cs]
        d = 1
        while d < win:
            ext = ext + pltpu.roll(ext, d, 0)
            d *= 2
        h = hb_ref[HL:HL + bm, cs]
        acc = ext[HL:, :]
        count = jnp.minimum(t + 1, win).astype(F32)
        y = (acc / count - h).astype(BF16)
        o_ref[0, :, cs] = x[:, cs] + _dot(y, w_ref[gi]) * sc_ref[:, cs]


def pool_mixer(x, g, w_pool, scale, bm=512):
    B, S, D = x.shape
    G, gw, _ = w_pool.shape
    assert S % bm == 0 and G == len(POOL_WINDOWS)
    assert all(w & (w - 1) == 0 and w <= POOL_HALO for w in POOL_WINDOWS)
    kern = functools.partial(_pool_kernel, bm=bm, gw=gw)
    return pl.pallas_call(
        kern,
        grid=(B, S // bm),
        in_specs=[
            pl.BlockSpec((1, bm, D), lambda b, s: (b, s, 0)),
            pl.BlockSpec((1, D), lambda b, s: (0, 0)),
            pl.BlockSpec((G, gw, gw), lambda b, s: (0, 0, 0)),
            pl.BlockSpec((1, D), lambda b, s: (0, 0)),
        ],
        out_specs=pl.BlockSpec((1, bm, D), lambda b, s: (b, s, 0)),
        out_shape=jax.ShapeDtypeStruct((B, S, D), F32),
        scratch_shapes=[pltpu.VMEM((POOL_HALO + bm, D), F32)],
        compiler_params=_cparams("parallel", "arbitrary"),
        name="pool_mixer",
    )(x, g.reshape(1, D), w_pool.astype(BF16), scale.reshape(1, D))


def _rel_bucket(rel):
    n = jnp.maximum(rel, 0)
    max_exact = REL_BUCKETS // 2
    nf = jnp.maximum(n, 1).astype(F32)
    large = max_exact + (jnp.log(nf / max_exact) / math.log(REL_MAX_DIST / max_exact)
                         * (REL_BUCKETS - max_exact)).astype(jnp.int32)
    large = jnp.minimum(large, REL_BUCKETS - 1)
    return jnp.where(n < max_exact, n, large)


def _diff_attn_kernel(q_ref, k_ref, v_ref, rb_ref, lam_ref, sg_ref, o_ref,
                      m_ref, l_ref, acc_ref, bias_ref, s_ref, p_ref, a_ref, *, blk, hd, lam_init):
    qi = pl.program_id(2)
    L = 2 * blk
    lanes = m_ref.shape[-1]
    units = m_ref.shape[0]
    vd = 2 * hd

    @pl.when((qi == 0) & (pl.program_id(1) == 0))
    def _():
        for u in range(units):
            for off in range(2):
                r = jnp.broadcast_to(rb_ref[u, off:off + 1, :], (blk, L))
                bias_ref[u, off] = pltpu.roll(r, L - blk + 1, 1, stride=1, stride_axis=0)[:, :blk]

    m_ref[...] = jnp.full(m_ref.shape, -jnp.inf, F32)
    l_ref[...] = jnp.zeros(l_ref.shape, F32)
    acc_ref[...] = jnp.zeros(acc_ref.shape, F32)

    def logits(j, u):
        start = pl.multiple_of(j * blk, blk)
        cols = slice(u * hd, (u + 1) * hd)
        s_ref[u] = _dot_nt(q_ref[0, :, cols], k_ref[0, pl.ds(start, blk), cols])

    def step(j, off, has_next=True):
        start = pl.multiple_of(j * blk, blk)
        for u in range(units):
            for r0 in range(0, blk, DIFF_STRIP):
                rows = slice(r0, r0 + DIFF_STRIP)
                s = s_ref[u, rows, :]
                if off is not None:
                    s = s + bias_ref[u, off, rows, :]
                m_old = m_ref[u, rows, :]
                m_new = jnp.maximum(m_old, jnp.max(s, axis=-1, keepdims=True))
                alpha = jnp.exp2(m_old - m_new)
                p = jnp.exp2(s - jnp.concatenate([m_new] * (blk // lanes), axis=1))
                psum = p[:, :lanes]
                for t in range(1, blk // lanes):
                    psum = psum + p[:, t * lanes:(t + 1) * lanes]
                l_ref[u, rows, :] = alpha * l_ref[u, rows, :] + psum
                p_ref[u, rows, :] = p.astype(BF16)
                a_ref[u, rows, :] = alpha
                m_ref[u, rows, :] = m_new
            if has_next:
                logits(j + 1, u)
            alpha = jnp.concatenate([a_ref[u]] * (vd // lanes), axis=1)
            vb = v_ref[0, pl.ds(start, blk), (u // 2) * vd:(u // 2 + 1) * vd]
            acc_ref[u] = alpha * acc_ref[u] + _dot(p_ref[u], vb)

    def far_body(j, carry):
        step(j, None)
        return carry

    for u in range(units):
        logits(0, u)
    lax.fori_loop(0, jnp.maximum(qi - 1, 0), far_body, 0)

    @pl.when(qi >= 1)
    def _():
        step(qi - 1, 1)

    step(qi, 0, has_next=False)

    lp = lam_ref[...]
    lam = (jnp.exp(jnp.sum(lp[0:1] * lp[1:2], axis=-1, keepdims=True))
           - jnp.exp(jnp.sum(lp[2:3] * lp[3:4], axis=-1, keepdims=True)) + lam_init)
    for h in range(units // 2):
        l0 = jnp.sum(l_ref[2 * h], axis=-1, keepdims=True)
        l1 = jnp.sum(l_ref[2 * h + 1], axis=-1, keepdims=True)
        o = acc_ref[2 * h] / l0 - lam * (acc_ref[2 * h + 1] / l1)
        o_ref[0, :, h * vd:(h + 1) * vd] = (_rms(o, sg_ref[...]) * (1.0 - lam_init)).astype(o_ref.dtype)


def diff_attention(proj, rel_vecs, lam_params, sub_gain, *, d_model, lam_init, blk):
    B, S, _ = proj.shape
    H = DIFF_HEADS
    HS = DIFF_HEADS_PER_STEP
    vd = d_model // H
    hd = vd // 2
    lanes = 128
    units = 2 * HS
    wd = HS * vd
    HG = H // HS
    assert S % blk == 0 and blk % lanes == 0 and H % HS == 0
    kern = functools.partial(_diff_attn_kernel, blk=blk, hd=hd, lam_init=lam_init)
    return pl.pallas_call(
        kern,
        grid=(HG, B, S // blk),
        in_specs=[
            pl.BlockSpec((1, blk, wd), lambda h, b, i: (b, i, h)),
            pl.BlockSpec((1, S, wd), lambda h, b, i: (b, 0, HG + h)),
            pl.BlockSpec((1, S, wd), lambda h, b, i: (b, 0, 2 * HG + h)),
            pl.BlockSpec((units, 2, 2 * blk), lambda h, b, i: (h, 0, 0)),
            pl.BlockSpec((4, hd), lambda h, b, i: (0, 0)),
            pl.BlockSpec((1, vd), lambda h, b, i: (0, 0)),
        ],
        out_specs=pl.BlockSpec((1, blk, wd), lambda h, b, i: (b, i, h)),
        out_shape=jax.ShapeDtypeStruct((B, S, d_model), BF16),
        scratch_shapes=[
            pltpu.VMEM((units, blk, lanes), F32),
            pltpu.VMEM((units, blk, lanes), F32),
            pltpu.VMEM((units, blk, vd), F32),
            pltpu.VMEM((units, 2, blk, blk), F32),
            pltpu.VMEM((units, blk, blk), F32),
            pltpu.VMEM((units, blk, blk), BF16),
            pltpu.VMEM((units, blk, lanes), F32),
        ],
        compiler_params=_cparams("arbitrary", "arbitrary", "arbitrary"),
        name="diff_attention",
    )(proj, proj, proj, rel_vecs, lam_params, sub_gain.reshape(1, vd))


def _rel_bias_vectors(rel_table, blk):
    H2 = rel_table.shape[1]
    far = rel_table[REL_BUCKETS - 1]
    y = jnp.arange(2 * blk)
    vecs = []
    for off in (0, 1):
        rel = off * blk + blk - 1 - y
        v = rel_table[_rel_bucket(rel)].astype(F32) - far
        vecs.append(jnp.where((rel >= 0)[:, None], v * LOG2_E, -jnp.inf).T)
    return jnp.stack(vecs, axis=1)


def diff_attn_mixer(x, g, w_in, w_out, layer, q_gain, k_gain, lam_params, sub_gain, rel_table,
                    layer_idx, blk=512):
    B, S, D = x.shape
    assert blk >= REL_MAX_DIST
    hd = D // DIFF_HEADS // 2
    lam_init = 0.8 - 0.6 * math.exp(-0.3 * layer_idx)
    n_heads2 = D // hd
    head_gain = jnp.concatenate([jnp.tile(q_gain, n_heads2) * (hd ** -0.5 * LOG2_E),
                                 jnp.tile(k_gain, n_heads2), jnp.ones((D,), F32)])
    x2 = x.reshape(B * S, D)
    proj = norm_matmul(x2, g, w_in, layer, head_gain=head_gain, head_norm_cols=2 * D, head_dim=hd)
    o = diff_attention(proj.reshape(B, S, 3 * D), _rel_bias_vectors(rel_table, blk), lam_params,
                       sub_gain, d_model=D, lam_init=lam_init, blk=blk)
    return matmul_residual(o.reshape(B * S, D), w_out, layer, x2).reshape(B, S, D)


def kernel(x, norm_g, gla_w_in, gla_w_a2, gla_b_a, gla_g_norm, gla_w_out, pool_w, pool_scale,
           diff_w_in, diff_q_gain, diff_k_gain, diff_lambda, diff_sub_gain, diff_w_out, rel_bias,
           ffn_w_gu, ffn_w_down):
    B, S, D = x.shape
    depth = norm_g.shape[0]
    n_gla, n_diff = gla_w_in.shape[0], diff_w_in.shape[0]
    gla_w = gla_weights(gla_w_in, gla_w_a2, gla_w_out.shape[1])
    riders = [(ffn_w_gu, 0), (ffn_w_down, 0)]
    riders += [(diff_w_in, s) for s in range(n_diff)] + [(diff_w_out, s) for s in range(n_diff)]
    riders += [(gla_w_out, s) for s in range(n_gla)]
    for i in range(depth):
        kind, slot = i % N_MIXERS, i // N_MIXERS
        if kind == 0 and i == 0:
            x, cast_out = gla_mixer(x, norm_g[i, 0], gla_w, slot, gla_b_a[slot], gla_g_norm[slot],
                                    cast=tuple(riders), w_out_cast=2 + 2 * n_diff + slot)
            ffn_w = cast_out[:2]
            diff_w_in_b, diff_w_out_b = cast_out[2:2 + n_diff], cast_out[2 + n_diff:2 + 2 * n_diff]
            gla_out_b = cast_out[2 + 2 * n_diff:]
        elif kind == 0:
            x, _ = gla_mixer(x, norm_g[i, 0], gla_w, slot, gla_b_a[slot], gla_g_norm[slot],
                             w_out=gla_out_b[slot])
        elif kind == 1:
            x = pool_mixer(x, norm_g[i, 0], pool_w[slot], pool_scale[slot])
        else:
            x = diff_attn_mixer(x, norm_g[i, 0], diff_w_in_b[slot], diff_w_out_b[slot], 0,
                                diff_q_gain[slot], diff_k_gain[slot], diff_lambda[slot],
                                diff_sub_gain[slot], rel_bias, i)
        x2 = x.reshape(B * S, D)
        if i + 1 < depth:
            x2, *ffn_w_next = ffn(x2, norm_g[i, 1], ffn_w[0], ffn_w[1], 0,
                                  next_f32=(ffn_w_gu, ffn_w_down, i + 1))
            ffn_w = tuple(ffn_w_next)
        else:
            x2 = ffn(x2, norm_g[i, 1], ffn_w[0], ffn_w[1], 0)
        x = x2.reshape(B, S, D)
    return x
```
